```python
import math
import jax
import jax.numpy as jnp
from jax import lax
import numpy as np


D_MODEL = 1024
BATCH = 4
SEQ = 4096
DEPTH = 2
DEC_BATCH = 32
DEC_SEQ = 1
PAST_LEN = 8192
PAGE_SIZE = 128

D_FF = 2816
CONV_CH = D_MODEL // 2
CONV_A_WIDTH = 31
LRU_CH = D_MODEL // 2
LRU_BLOCKS = 8
LRU_BLOCK = LRU_CH // LRU_BLOCKS
CONV_B_WIDTH = 4
LRU_C = 8.0
ATT_HEADS = 8
ATT_HD = 64
ATT_VD = 2 * ATT_HD
N_MEM = 256
XATT_HEADS = 4
XATT_HD = D_MODEL // XATT_HEADS
Q_BLOCK = 128
N_EVEN = (DEPTH + 1) // 2
N_ODD = DEPTH // 2
EPS = 1e-6
NEG_INF = -1e30

kernel_name = 'hybrid_conformer_rglru_diffattn_step'


def rms_norm(x, g):
    xf = x.astype(jnp.float32)
    y = xf * lax.rsqrt(jnp.mean(xf * xf, axis=-1, keepdims=True) + EPS)
    return (y * g.astype(jnp.float32)).astype(x.dtype)


def layer_norm(x, g, b):
    xf = x.astype(jnp.float32)
    mu = jnp.mean(xf, axis=-1, keepdims=True)
    var = jnp.mean(jnp.square(xf - mu), axis=-1, keepdims=True)
    y = (xf - mu) * lax.rsqrt(var + EPS) * g.astype(jnp.float32) + b.astype(jnp.float32)
    return y.astype(x.dtype)


def swiglu(x, w_in, w_out):
    g, u = jnp.split(x @ w_in, 2, axis=-1)
    return (jax.nn.silu(g) * u) @ w_out


def dwconv_valid(x_ext, w, b):
    y = lax.conv_general_dilated(x_ext, w[:, None, :].astype(x_ext.dtype), window_strides=(1,), padding='VALID', dimension_numbers=('NWC', 'WIO', 'NWC'), feature_group_count=x_ext.shape[-1])
    return y + b


def block_diag(x, w, b):
    bsz, t, c = x.shape
    y = jnp.einsum('btki,kij->btkj', x.reshape(bsz, t, LRU_BLOCKS, LRU_BLOCK), w)
    return y.reshape(bsz, t, c) + b


def rg_lru(x, h0, w_a, b_a, w_x, b_x, lam):
    r = jax.nn.sigmoid(block_diag(x, w_a, b_a).astype(jnp.float32))
    i = jax.nn.sigmoid(block_diag(x, w_x, b_x).astype(jnp.float32))
    log_a = -LRU_C * r * jax.nn.softplus(-lam.astype(jnp.float32))
    a = jnp.exp(log_a)
    beta = jnp.sqrt(jnp.maximum(-jnp.expm1(2.0 * log_a), 0.0))
    u = beta * i * x.astype(jnp.float32)
    u = u.at[:, 0].add(a[:, 0] * h0.astype(jnp.float32))

    def combine(c1, c2):
        return c1[0] * c2[0], c2[0] * c1[1] + c2[1]

    _, h = lax.associative_scan(combine, (a, u), axis=1)
    return h, h[:, -1]


def even_mixer(xn, hist_a, hist_b, h0, p, e):
    z = xn @ p['even_w_in'][e]
    a_val, a_gate, b_rec, b_gate = jnp.split(z, [CONV_CH, 2 * CONV_CH, 2 * CONV_CH + LRU_CH], axis=-1)
    u = a_val * jax.nn.sigmoid(a_gate)
    u_ext = jnp.concatenate([hist_a.astype(u.dtype), u], axis=1)
    ya = dwconv_valid(u_ext, p['conv_a_w'][e], p['conv_a_b'][e])
    ya = jax.nn.silu(layer_norm(ya, p['conv_a_ln_g'][e], p['conv_a_ln_b'][e]))
    r_ext = jnp.concatenate([hist_b.astype(b_rec.dtype), b_rec], axis=1)
    xr = dwconv_valid(r_ext, p['conv_b_w'][e], p['conv_b_b'][e])
    h, h_last = rg_lru(xr, h0, p['lru_w_a'][e], p['lru_b_a'][e], p['lru_w_x'][e], p['lru_b_x'][e], p['lru_lambda'][e])
    yb = h.astype(xn.dtype) * jax.nn.gelu(b_gate)
    out = jnp.concatenate([ya, yb], axis=-1) @ p['even_w_out'][e]
    return out, u_ext[:, -(CONV_A_WIDTH - 1):], r_ext[:, -(CONV_B_WIDTH - 1):], h_last


def diff_attn_block(q, k, v, q_pos, k_pos, lam):
    s = jnp.einsum('bqhmd,bkhmd->bhmqk', q, k).astype(jnp.float32) * (ATT_HD ** -0.5)
    s = jnp.where(k_pos[None, :] <= q_pos[:, None], s, NEG_INF)
    pr = jax.nn.softmax(s, axis=-1)
    w = pr[:, :, 0] - lam * pr[:, :, 1]
    return jnp.einsum('bhqk,bkhe->bqhe', w.astype(v.dtype), v)


def odd_mixer(xn, k_past, v_past, p, o, layer_idx):
    bsz, t, _ = xn.shape
    past = k_past.shape[1]
    q, k, v = jnp.split(xn @ p['attn_w_in'][o], 3, axis=-1)
    q = q.reshape(bsz, t, ATT_HEADS, 2, ATT_HD)
    k_new = k.reshape(bsz, t, ATT_HEADS, 2 * ATT_HD)
    v_new = v.reshape(bsz, t, ATT_HEADS, ATT_VD)
    k_all = jnp.concatenate([k_past.astype(k_new.dtype), k_new], axis=1).reshape(bsz, past + t, ATT_HEADS, 2, ATT_HD)
    v_all = jnp.concatenate([v_past.astype(v_new.dtype), v_new], axis=1)
    lam_init = 0.8 - 0.6 * math.exp(-0.3 * layer_idx)
    f32 = jnp.float32
    lam = (jnp.exp(jnp.sum(p['lam_q1'][o].astype(f32) * p['lam_k1'][o].astype(f32)))
           - jnp.exp(jnp.sum(p['lam_q2'][o].astype(f32) * p['lam_k2'][o].astype(f32))) + lam_init)
    q_pos = past + jnp.arange(t)
    k_pos = jnp.arange(past + t)
    if t > Q_BLOCK and t % Q_BLOCK == 0:
        nb = t // Q_BLOCK
        qb = jnp.moveaxis(q.reshape(bsz, nb, Q_BLOCK, ATT_HEADS, 2, ATT_HD), 1, 0)
        pb = q_pos.reshape(nb, Q_BLOCK)
        ob = lax.map(lambda qp: diff_attn_block(qp[0], k_all, v_all, qp[1], k_pos, lam), (qb, pb))
        att = jnp.moveaxis(ob, 0, 1).reshape(bsz, t, ATT_HEADS, ATT_VD)
    else:
        att = diff_attn_block(q, k_all, v_all, q_pos, k_pos, lam)
    att = rms_norm(att, p['attn_subln_g'][o]) * (1.0 - lam_init)
    out = att.reshape(bsz, t, ATT_HEADS * ATT_VD) @ p['attn_w_out'][o]
    return out, k_new, v_new


def cross_attn(xn, mem_k, mem_v, w_q, w_out):
    bsz, t, _ = xn.shape
    q = (xn @ w_q).reshape(bsz, t, XATT_HEADS, XATT_HD)
    s = jnp.einsum('bqhd,bkhd->bhqk', q, mem_k.astype(q.dtype)).astype(jnp.float32) * (XATT_HD ** -0.5)
    pr = jax.nn.softmax(s, axis=-1)
    o = jnp.einsum('bhqk,bkhd->bqhd', pr.astype(xn.dtype), mem_v.astype(xn.dtype))
    return o.reshape(bsz, t, D_MODEL) @ w_out


def trunk(x, mem_k, mem_v, hist_a, hist_b, lru_h0, k_past, v_past, p):
    conv_a, conv_b, lru_h, k_rows, v_rows = [], [], [], [], []
    for l in range(DEPTH):
        x = x + 0.5 * swiglu(rms_norm(x, p['ffn1_g'][l]), p['ffn1_w_in'][l], p['ffn1_w_out'][l])
        xn = rms_norm(x, p['mix_g'][l])
        if l % 2 == 0:
            e = l // 2
            m, ca, cb, hl = even_mixer(xn, hist_a[e], hist_b[e], lru_h0[e], p, e)
            conv_a.append(ca)
            conv_b.append(cb)
            lru_h.append(hl)
        else:
            o = l // 2
            m, kn, vn = odd_mixer(xn, k_past[o], v_past[o], p, o, l)
            k_rows.append(kn)
            v_rows.append(vn)
        x = x + m
        x = x + cross_attn(rms_norm(x, p['xattn_g'][l]), mem_k[l], mem_v[l], p['xattn_w_q'][l], p['xattn_w_out'][l])
        x = x + 0.5 * swiglu(rms_norm(x, p['ffn2_g'][l]), p['ffn2_w_in'][l], p['ffn2_w_out'][l])
    y = rms_norm(x, p['final_g'])
    return y, jnp.stack(conv_a), jnp.stack(conv_b), jnp.stack(lru_h), jnp.stack(k_rows), jnp.stack(v_rows)


def setup_inputs(seed: int = 0) -> dict:
    key = jax.random.key(seed)
    counter = [0]

    def nk():
        counter[0] += 1
        return jax.random.fold_in(key, counter[0])

    def nrm(shape, scale=1.0):
        return jax.random.normal(nk(), shape, jnp.float32) * scale

    def gain(shape):
        return 1.0 + nrm(shape, 0.05)

    n_pages = PAST_LEN // PAGE_SIZE
    n_used = DEC_BATCH * n_pages
    n_pool = n_used + n_used // 4
    page_table = jax.random.permutation(nk(), n_pool)[:n_used].reshape(DEC_BATCH, n_pages).astype(jnp.int32)
    a_base = jax.random.uniform(nk(), (N_EVEN, LRU_CH), jnp.float32, 0.9, 0.999) ** (1.0 / LRU_C)
    lru_lambda = jnp.log(a_base) - jnp.log1p(-a_base)
    d_mix = CONV_CH + LRU_CH
    return {
        'x_prompt': nrm((BATCH, SEQ, D_MODEL)),
        'x_sample': nrm((DEC_BATCH, DEC_SEQ, D_MODEL)),
        'state_conv_a': nrm((N_EVEN, DEC_BATCH, CONV_A_WIDTH - 1, CONV_CH), 0.5),
        'state_conv_b': nrm((N_EVEN, DEC_BATCH, CONV_B_WIDTH - 1, LRU_CH)),
        'state_lru': nrm((N_EVEN, DEC_BATCH, LRU_CH), 0.5),
        'cache_k': nrm((N_ODD, n_pool, PAGE_SIZE, ATT_HEADS, 2 * ATT_HD)),
        'cache_v': nrm((N_ODD, n_pool, PAGE_SIZE, ATT_HEADS, ATT_VD)),
        'cache_mem_k': nrm((DEPTH, DEC_BATCH, N_MEM, XATT_HEADS, XATT_HD)),
        'cache_mem_v': nrm((DEPTH, DEC_BATCH, N_MEM, XATT_HEADS, XATT_HD)),
        'page_table': page_table,
        'mem_prompt': nrm((BATCH, N_MEM, D_MODEL)),
        'ffn1_g': gain((DEPTH, D_MODEL)),
        'ffn1_w_in': nrm((DEPTH, D_MODEL, 2 * D_FF), D_MODEL ** -0.5),
        'ffn1_w_out': nrm((DEPTH, D_FF, D_MODEL), D_FF ** -0.5),
        'mix_g': gain((DEPTH, D_MODEL)),
        'even_w_in': nrm((N_EVEN, D_MODEL, 2 * d_mix), D_MODEL ** -0.5),
        'conv_a_w': nrm((N_EVEN, CONV_A_WIDTH, CONV_CH), CONV_A_WIDTH ** -0.5),
        'conv_a_b': nrm((N_EVEN, CONV_CH), 0.02),
        'conv_a_ln_g': gain((N_EVEN, CONV_CH)),
        'conv_a_ln_b': nrm((N_EVEN, CONV_CH), 0.02),
        'conv_b_w': nrm((N_EVEN, CONV_B_WIDTH, LRU_CH), CONV_B_WIDTH ** -0.5),
        'conv_b_b': nrm((N_EVEN, LRU_CH), 0.02),
        'lru_w_a': nrm((N_EVEN, LRU_BLOCKS, LRU_BLOCK, LRU_BLOCK), LRU_BLOCK ** -0.5),
        'lru_b_a': nrm((N_EVEN, LRU_CH), 0.02),
        'lru_w_x': nrm((N_EVEN, LRU_BLOCKS, LRU_BLOCK, LRU_BLOCK), LRU_BLOCK ** -0.5),
        'lru_b_x': nrm((N_EVEN, LRU_CH), 0.02),
        'lru_lambda': lru_lambda,
        'even_w_out': nrm((N_EVEN, d_mix, D_MODEL), d_mix ** -0.5),
        'attn_w_in': nrm((N_ODD, D_MODEL, 3 * ATT_HEADS * ATT_VD), D_MODEL ** -0.5),
        'lam_q1': nrm((N_ODD, ATT_HD), 0.1),
        'lam_k1': nrm((N_ODD, ATT_HD), 0.1),
        'lam_q2': nrm((N_ODD, ATT_HD), 0.1),
        'lam_k2': nrm((N_ODD, ATT_HD), 0.1),
        'attn_subln_g': gain((N_ODD, ATT_VD)),
        'attn_w_out': nrm((N_ODD, ATT_HEADS * ATT_VD, D_MODEL), (ATT_HEADS * ATT_VD) ** -0.5),
        'xattn_g': gain((DEPTH, D_MODEL)),
        'xattn_w_q': nrm((DEPTH, D_MODEL, D_MODEL), D_MODEL ** -0.5),
        'xattn_w_kv': nrm((DEPTH, D_MODEL, 2 * D_MODEL), D_MODEL ** -0.5),
        'xattn_w_out': nrm((DEPTH, D_MODEL, D_MODEL), D_MODEL ** -0.5),
        'ffn2_g': gain((DEPTH, D_MODEL)),
        'ffn2_w_in': nrm((DEPTH, D_MODEL, 2 * D_FF), D_MODEL ** -0.5),
        'ffn2_w_out': nrm((DEPTH, D_FF, D_MODEL), D_FF ** -0.5),
        'final_g': gain((D_MODEL,)),
    }


def reference(x_prompt, x_sample, state_conv_a, state_conv_b, state_lru, cache_k, cache_v, cache_mem_k, cache_mem_v, page_table, mem_prompt, ffn1_g, ffn1_w_in, ffn1_w_out, mix_g, even_w_in, conv_a_w, conv_a_b, conv_a_ln_g, conv_a_ln_b, conv_b_w, conv_b_b, lru_w_a, lru_b_a, lru_w_x, lru_b_x, lru_lambda, even_w_out, attn_w_in, lam_q1, lam_k1, lam_q2, lam_k2, attn_subln_g, attn_w_out, xattn_g, xattn_w_q, xattn_w_kv, xattn_w_out, ffn2_g, ffn2_w_in, ffn2_w_out, final_g):
    p = {
        'ffn1_g': ffn1_g, 'ffn1_w_in': ffn1_w_in, 'ffn1_w_out': ffn1_w_out, 'mix_g': mix_g,
        'even_w_in': even_w_in, 'conv_a_w': conv_a_w, 'conv_a_b': conv_a_b,
        'conv_a_ln_g': conv_a_ln_g, 'conv_a_ln_b': conv_a_ln_b, 'conv_b_w': conv_b_w, 'conv_b_b': conv_b_b,
        'lru_w_a': lru_w_a, 'lru_b_a': lru_b_a, 'lru_w_x': lru_w_x, 'lru_b_x': lru_b_x,
        'lru_lambda': lru_lambda, 'even_w_out': even_w_out, 'attn_w_in': attn_w_in,
        'lam_q1': lam_q1, 'lam_k1': lam_k1, 'lam_q2': lam_q2, 'lam_k2': lam_k2,
        'attn_subln_g': attn_subln_g, 'attn_w_out': attn_w_out, 'xattn_g': xattn_g,
        'xattn_w_q': xattn_w_q, 'xattn_w_out': xattn_w_out, 'ffn2_g': ffn2_g,
        'ffn2_w_in': ffn2_w_in, 'ffn2_w_out': ffn2_w_out, 'final_g': final_g,
    }
    bsz = x_prompt.shape[0]
    dt = x_prompt.dtype
    mem_kv = jnp.einsum('bmd,ldf->lbmf', mem_prompt, xattn_w_kv)
    p_mem_k, p_mem_v = jnp.split(mem_kv, 2, axis=-1)
    p_mem_k = p_mem_k.reshape(DEPTH, bsz, N_MEM, XATT_HEADS, XATT_HD)
    p_mem_v = p_mem_v.reshape(DEPTH, bsz, N_MEM, XATT_HEADS, XATT_HD)
    y_prompt, p_conv_a, p_conv_b, p_lru, p_k, p_v = trunk(
        x_prompt, p_mem_k, p_mem_v,
        jnp.zeros((N_EVEN, bsz, CONV_A_WIDTH - 1, CONV_CH), dt),
        jnp.zeros((N_EVEN, bsz, CONV_B_WIDTH - 1, LRU_CH), dt),
        jnp.zeros((N_EVEN, bsz, LRU_CH), jnp.float32),
        jnp.zeros((N_ODD, bsz, 0, ATT_HEADS, 2 * ATT_HD), dt),
        jnp.zeros((N_ODD, bsz, 0, ATT_HEADS, ATT_VD), dt), p)
    dbsz, n_pages = page_table.shape
    past_len = n_pages * cache_k.shape[2]
    k_past = cache_k[:, page_table].reshape(N_ODD, dbsz, past_len, ATT_HEADS, 2 * ATT_HD)
    v_past = cache_v[:, page_table].reshape(N_ODD, dbsz, past_len, ATT_HEADS, ATT_VD)
    y_sample, s_conv_a, s_conv_b, s_lru, s_k, s_v = trunk(
        x_sample, cache_mem_k, cache_mem_v, state_conv_a, state_conv_b, state_lru, k_past, v_past, p)
    return (y_prompt, y_sample, p_conv_a, p_conv_b, p_lru, p_k, p_v, p_mem_k, p_mem_v, s_conv_a, s_conv_b, s_lru, s_k, s_v)
```

```python
import functools
import math

import jax
import jax.numpy as jnp
from jax import lax
from jax.experimental import pallas as pl
from jax.experimental.pallas import tpu as pltpu

F32 = jnp.float32
BF16 = jnp.bfloat16

D_MODEL = 1024
D_FF = 2816
CONV_CH = 512
CONV_A_WIDTH = 31
LRU_CH = 512
LRU_BLOCKS = 8
CONV_B_WIDTH = 4
LRU_C = 8.0
ATT_HEADS = 8
ATT_HD = 64
ATT_VD = 128
N_MEM = 256
XATT_HEADS = 4
XATT_HD = 256
PAGE_SIZE = 128
EPS = 1e-6
NEG_INF = -1e30

SUBLANES = 8
VMEM_LIMIT = 56 * 1024 * 1024

FFN_TF = 1408
ROW_TILE = 512
SEQ_TILE_CONV = 512
SEQ_TILE_LRU = 256
ATT_TQ = 512
DEC_PAGES = 8


def _params(sem):
    return pltpu.CompilerParams(dimension_semantics=sem, vmem_limit_bytes=VMEM_LIMIT)


def _rms(x, g):
    return x * lax.rsqrt(jnp.mean(x * x, axis=-1, keepdims=True) + EPS) * g


def _dot(a, b):
    return jnp.dot(a, b, preferred_element_type=F32)


def _dot_nt(a, b):
    return lax.dot_general(a, b, (((1,), (1,)), ((), ())), preferred_element_type=F32)


def _row_tile(m):
    return ROW_TILE if m % ROW_TILE == 0 else m


def _ffn_body(*refs, nf, final):
    if final:
        x_ref, g_ref, wg_ref, wu_ref, wo_ref, gf_ref, o_ref, xn_ref, acc_ref = refs
    else:
        x_ref, g_ref, wg_ref, wu_ref, wo_ref, o_ref, xn_ref, acc_ref = refs
    j = pl.program_id(1)

    @pl.when(j == 0)
    def _():
        xn_ref[...] = _rms(x_ref[...], g_ref[...]).astype(BF16)

    xn = xn_ref[...]
    gate = _dot(xn, wg_ref[...])
    up = _dot(xn, wu_ref[...])
    h = (gate * jax.nn.sigmoid(gate) * up).astype(BF16)
    part = _dot(h, wo_ref[...])

    @pl.when(j == 0)
    def _():
        acc_ref[...] = part

    @pl.when(j != 0)
    def _():
        acc_ref[...] += part

    @pl.when(j == nf - 1)
    def _():
        y = x_ref[...] + 0.5 * acc_ref[...]
        if final:
            y = _rms(y, gf_ref[...])
        o_ref[...] = y


def _ffn(x, g, w_in, w_out, final_g=None):
    m, d = x.shape
    f = w_out.shape[0]
    tm = _row_tile(m)
    nf = f // FFN_TF
    final = final_g is not None
    in_specs = [
        pl.BlockSpec((tm, d), lambda i, j: (i, 0)),
        pl.BlockSpec((1, d), lambda i, j: (0, 0)),
        pl.BlockSpec((d, FFN_TF), lambda i, j: (0, j)),
        pl.BlockSpec((d, FFN_TF), lambda i, j: (0, j + nf)),
        pl.BlockSpec((FFN_TF, d), lambda i, j: (j, 0)),
    ]
    args = [x, g.reshape(1, d), w_in, w_in, w_out]
    if final:
        in_specs.append(pl.BlockSpec((1, d), lambda i, j: (0, 0)))
        args.append(final_g.reshape(1, d))
    return pl.pallas_call(
        functools.partial(_ffn_body, nf=nf, final=final),
        grid=(m // tm, nf),
        in_specs=in_specs,
        out_specs=pl.BlockSpec((tm, d), lambda i, j: (i, 0)),
        out_shape=jax.ShapeDtypeStruct((m, d), F32),
        scratch_shapes=[pltpu.VMEM((tm, d), BF16), pltpu.VMEM((tm, d), F32)],
        compiler_params=_params(("parallel", "arbitrary")),
        name="ffn",
    )(*args)


def _proj_body(*refs, has_g, n_w, out_meta):
    x_ref = refs[0]
    pos = 1
    if has_g:
        g_ref = refs[pos]
        pos += 1
    w_refs = refs[pos:pos + n_w]
    o_refs = refs[pos + n_w:]
    x = x_ref[...]
    if has_g:
        x = _rms(x, g_ref[...])
    xb = x.astype(BF16)
    ys = [_dot(xb, w_ref[...]) for w_ref in w_refs]
    for o_ref, (wi, scale) in zip(o_refs, out_meta):
        y = ys[wi]
        if scale != 1.0:
            y = y * scale
        o_ref[...] = y.astype(o_ref.dtype)


def _proj(x, g, w, width, blocks, outs):
    m, k = x.shape
    tm = _row_tile(m)
    has_g = g is not None
    in_specs = [pl.BlockSpec((tm, k), lambda i: (i, 0))]
    args = [x]
    if has_g:
        in_specs.append(pl.BlockSpec((1, k), lambda i: (0, 0)))
        args.append(g.reshape(1, k))
    for b in blocks:
        in_specs.append(pl.BlockSpec((k, width), lambda i, b=b: (0, b)))
        args.append(w)
    return pl.pallas_call(
        functools.partial(_proj_body, has_g=has_g, n_w=len(blocks),
                          out_meta=tuple((wi, sc) for wi, _, sc in outs)),
        grid=(m // tm,),
        in_specs=in_specs,
        out_specs=[pl.BlockSpec((tm, width), lambda i: (i, 0)) for _ in outs],
        out_shape=[jax.ShapeDtypeStruct((m, width), dt) for _, dt, _ in outs],
        compiler_params=_params(("parallel",)),
        name="proj",
    )(*args)


def _outproj_body(*refs, n):
    x_ref = refs[0]
    y_refs = refs[1:1 + n]
    w_refs = refs[1 + n:1 + 2 * n]
    o_ref = refs[1 + 2 * n]
    acc = x_ref[...]
    for y_ref, w_ref in zip(y_refs, w_refs):
        acc = acc + _dot(y_ref[...], w_ref[...])
    o_ref[...] = acc


def _outproj(x, ys, ws):
    m, d = x.shape
    tm = _row_tile(m)
    n = len(ys)
    in_specs = [pl.BlockSpec((tm, d), lambda i: (i, 0))]
    in_specs += [pl.BlockSpec((tm, y.shape[1]), lambda i: (i, 0)) for y in ys]
    in_specs += [pl.BlockSpec(w.shape, lambda i: (0, 0)) for w in ws]
    return pl.pallas_call(
        functools.partial(_outproj_body, n=n),
        grid=(m // tm,),
        in_specs=in_specs,
        out_specs=pl.BlockSpec((tm, d), lambda i: (i, 0)),
        out_shape=jax.ShapeDtypeStruct((m, d), F32),
        compiler_params=_params(("parallel",)),
        name="outproj",
    )(x, *ys, *ws)


def _even_in_body(x_ref, g_ref, w_ref, u_ref, br_ref, gb_ref):
    xn = _rms(x_ref[...], g_ref[...]).astype(BF16)
    z = _dot(xn, w_ref[...])
    c = CONV_CH
    u_ref[...] = z[:, :c] * jax.nn.sigmoid(z[:, c:2 * c])
    br_ref[...] = z[:, 2 * c:2 * c + LRU_CH]
    gb_ref[...] = jax.nn.gelu(z[:, 2 * c + LRU_CH:])


def _even_in(x, g, w):
    m, d = x.shape
    tm = _row_tile(m)
    n = w.shape[1]
    return pl.pallas_call(
        _even_in_body,
        grid=(m // tm,),
        in_specs=[pl.BlockSpec((tm, d), lambda i: (i, 0)),
                  pl.BlockSpec((1, d), lambda i: (0, 0)),
                  pl.BlockSpec((d, n), lambda i: (0, 0))],
        out_specs=[pl.BlockSpec((tm, CONV_CH), lambda i: (i, 0)) for _ in range(3)],
        out_shape=[jax.ShapeDtypeStruct((m, CONV_CH), F32) for _ in range(3)],
        compiler_params=_params(("parallel",)),
        name="even_in",
    )(x, g.reshape(1, d), w)


def _layer_norm_silu(y, g, b):
    mu = jnp.mean(y, axis=-1, keepdims=True)
    yc = y - mu
    var = jnp.mean(yc * yc, axis=-1, keepdims=True)
    yn = yc * lax.rsqrt(var + EPS) * g + b
    return yn * jax.nn.sigmoid(yn)


CONV_HALO = 32
CONV_CHUNK = 32


def _conv_a_body(u_ref, prev_ref, cw_ref, cb_ref, lg_ref, lb_ref, o_ref, ext_ref, rot_ref, *, tt):
    i = pl.program_id(1)
    c = CONV_CH
    n_ext = tt + CONV_HALO

    @pl.when(i == 0)
    def _():
        ext_ref[0:CONV_HALO, :] = jnp.zeros((CONV_HALO, c), F32)

    @pl.when(i != 0)
    def _():
        ext_ref[0:CONV_HALO, :] = prev_ref[0]

    ext_ref[CONV_HALO:n_ext, :] = u_ref[0]
    ext_ref[n_ext:n_ext + SUBLANES, :] = jnp.zeros((SUBLANES, c), F32)
    for r in range(SUBLANES):
        rot_ref[r] = ext_ref[pl.ds(r, n_ext), :]
    first = CONV_HALO - (CONV_A_WIDTH - 1)
    cb = cb_ref[...]
    lg = lg_ref[...]
    lb = lb_ref[...]
    for r0 in range(0, tt, CONV_CHUNK):
        acc = jnp.zeros((CONV_CHUNK, c), F32)
        for w in range(CONV_A_WIDTH):
            s = first + w
            acc = acc + rot_ref[s % SUBLANES, pl.ds(r0 + (s // SUBLANES) * SUBLANES, CONV_CHUNK), :] * cw_ref[w:w + 1, :]
        o_ref[0, r0:r0 + CONV_CHUNK, :] = _layer_norm_silu(acc + cb, lg, lb).astype(o_ref.dtype)


def _conv_a(u, cw, cb, lg, lb):
    b, t, c = u.shape
    tt = SEQ_TILE_CONV
    halo_blocks = tt // CONV_HALO
    vec = lambda a: a.reshape(1, c)
    return pl.pallas_call(
        functools.partial(_conv_a_body, tt=tt),
        grid=(b, t // tt),
        in_specs=[pl.BlockSpec((1, tt, c), lambda bi, i: (bi, i, 0)),
                  pl.BlockSpec((1, CONV_HALO, c), lambda bi, i: (bi, jnp.maximum(i * halo_blocks - 1, 0), 0)),
                  pl.BlockSpec((CONV_A_WIDTH, c), lambda bi, i: (0, 0)),
                  pl.BlockSpec((1, c), lambda bi, i: (0, 0)),
                  pl.BlockSpec((1, c), lambda bi, i: (0, 0)),
                  pl.BlockSpec((1, c), lambda bi, i: (0, 0))],
        out_specs=pl.BlockSpec((1, tt, c), lambda bi, i: (bi, i, 0)),
        out_shape=jax.ShapeDtypeStruct((b, t, c), BF16),
        scratch_shapes=[pltpu.VMEM((tt + CONV_HALO + SUBLANES, c), F32),
                        pltpu.VMEM((SUBLANES, tt + CONV_HALO, c), F32)],
        compiler_params=_params(("parallel", "arbitrary")),
        name="conv_a",
    )(u, u, cw, vec(cb), vec(lg), vec(lb))


def _lru_gates(xr, wg_ref, bg_ref, lam_ref):
    gates = _dot(xr.astype(BF16), wg_ref[...]) + bg_ref[...]
    r = jax.nn.sigmoid(gates[:, :LRU_CH])
    ig = jax.nn.sigmoid(gates[:, LRU_CH:])
    nl = -lam_ref[...]
    softplus = jnp.maximum(nl, 0.0) + jnp.log1p(jnp.exp(-jnp.abs(nl)))
    log_a = -LRU_C * r * softplus
    a = jnp.exp(log_a)
    beta = jnp.sqrt(jnp.maximum(-jnp.tanh(log_a) * (a * a + 1.0), 0.0))
    return a, beta * ig * xr


def _lru_body(br_ref, prev_ref, gb_ref, cw_ref, cb_ref, wg_ref, bg_ref, lam_ref,
              yb_ref, hl_ref, ext_ref, h_ref, *, tt):
    i = pl.program_id(1)
    c = LRU_CH

    @pl.when(i == 0)
    def _():
        h_ref[...] = jnp.zeros((1, c), F32)
        ext_ref[0:SUBLANES, :] = jnp.zeros((SUBLANES, c), F32)

    @pl.when(i != 0)
    def _():
        ext_ref[0:SUBLANES, :] = prev_ref[0]

    ext_ref[SUBLANES:SUBLANES + tt, :] = br_ref[0]
    first = SUBLANES - (CONV_B_WIDTH - 1)
    xr = cb_ref[...]
    for w in range(CONV_B_WIDTH):
        xr = xr + ext_ref[pl.ds(first + w, tt), :] * cw_ref[w:w + 1, :]
    a, u = _lru_gates(xr, wg_ref, bg_ref, lam_ref)

    row = lax.broadcasted_iota(jnp.int32, (tt, c), 0)
    d = 1
    while d < tt:
        if d < SUBLANES:
            keep = row >= d
            a_sh = jnp.where(keep, pltpu.roll(a, d, 0), 1.0)
            u_sh = jnp.where(keep, pltpu.roll(u, d, 0), 0.0)
        else:
            a_sh = jnp.concatenate([jnp.ones((d, c), F32), a[:tt - d]], axis=0)
            u_sh = jnp.concatenate([jnp.zeros((d, c), F32), u[:tt - d]], axis=0)
        u = a * u_sh + u
        a = a * a_sh
        d *= 2
    h = a * h_ref[...] + u
    yb_ref[0] = (h * gb_ref[0]).astype(yb_ref.dtype)
    h_last = h[tt - 1:tt, :]
    h_ref[...] = h_last
    hl_ref[0] = h_last


def _lru(br, gb, cw, cb, wg, bg, lam):
    b, t, c = br.shape
    tt = SEQ_TILE_LRU
    halo_blocks = tt // SUBLANES
    vec = lambda a: a.reshape(1, -1)
    return pl.pallas_call(
        functools.partial(_lru_body, tt=tt),
        grid=(b, t // tt),
        in_specs=[pl.BlockSpec((1, tt, c), lambda bi, i: (bi, i, 0)),
                  pl.BlockSpec((1, SUBLANES, c), lambda bi, i: (bi, jnp.maximum(i * halo_blocks - 1, 0), 0)),
                  pl.BlockSpec((1, tt, c), lambda bi, i: (bi, i, 0)),
                  pl.BlockSpec((CONV_B_WIDTH, c), lambda bi, i: (0, 0)),
                  pl.BlockSpec((1, c), lambda bi, i: (0, 0)),
                  pl.BlockSpec((c, 2 * c), lambda bi, i: (0, 0)),
                  pl.BlockSpec((1, 2 * c), lambda bi, i: (0, 0)),
                  pl.BlockSpec((1, c), lambda bi, i: (0, 0))],
        out_specs=[pl.BlockSpec((1, tt, c), lambda bi, i: (bi, i, 0)),
                   pl.BlockSpec((1, 1, c), lambda bi, i: (bi, 0, 0))],
        out_shape=[jax.ShapeDtypeStruct((b, t, c), BF16),
                   jax.ShapeDtypeStruct((b, 1, c), F32)],
        scratch_shapes=[pltpu.VMEM((tt + SUBLANES, c), F32), pltpu.VMEM((1, c), F32)],
        compiler_params=_params(("parallel", "arbitrary")),
        name="lru",
    )(br, br, gb, cw, vec(cb), wg, vec(bg), vec(lam))


def _sample_mix_body(u_ref, br_ref, gb_ref, ha_ref, hb_ref, h0_ref,
                     cwa_ref, cba_ref, lg_ref, lb_ref, cwb_ref, cbb_ref, wg_ref, bg_ref, lam_ref,
                     ya_ref, yb_ref, h_ref):
    u = u_ref[...]
    acc = cba_ref[...] + u * cwa_ref[CONV_A_WIDTH - 1:CONV_A_WIDTH, :]
    for w in range(CONV_A_WIDTH - 1):
        acc = acc + ha_ref[w] * cwa_ref[w:w + 1, :]
    ya_ref[...] = _layer_norm_silu(acc, lg_ref[...], lb_ref[...]).astype(ya_ref.dtype)

    br = br_ref[...]
    xr = cbb_ref[...] + br * cwb_ref[CONV_B_WIDTH - 1:CONV_B_WIDTH, :]
    for w in range(CONV_B_WIDTH - 1):
        xr = xr + hb_ref[w] * cwb_ref[w:w + 1, :]
    a, x_in = _lru_gates(xr, wg_ref, bg_ref, lam_ref)
    h = a * h0_ref[...] + x_in
    h_ref[...] = h
    yb_ref[...] = (h * gb_ref[...]).astype(yb_ref.dtype)


def _sample_mix(u, br, gb, hist_a_t, hist_b_t, h0, cwa, cba, lg, lb, cwb, cbb, wg, bg, lam):
    n, c = u.shape
    vec = lambda a: a.reshape(1, -1)
    args = [u, br, gb, hist_a_t, hist_b_t, h0, cwa, vec(cba), vec(lg), vec(lb), cwb, vec(cbb), wg, vec(bg), vec(lam)]
    full = lambda a: pl.BlockSpec(a.shape, lambda i, nd=a.ndim: (0,) * nd)
    return pl.pallas_call(
        _sample_mix_body,
        grid=(1,),
        in_specs=[full(a) for a in args],
        out_specs=[pl.BlockSpec((n, c), lambda i: (0, 0)) for _ in range(3)],
        out_shape=[jax.ShapeDtypeStruct((n, c), BF16), jax.ShapeDtypeStruct((n, c), BF16),
                   jax.ShapeDtypeStruct((n, c), F32)],
        compiler_params=_params(("arbitrary",)),
        name="sample_mix",
    )(*args)


def _diff_lambda(lq1_ref, lk1_ref, lq2_ref, lk2_ref, lam_init):
    e1 = jnp.exp(jnp.sum(lq1_ref[...] * lk1_ref[...], axis=-1, keepdims=True))
    e2 = jnp.exp(jnp.sum(lq2_ref[...] * lk2_ref[...], axis=-1, keepdims=True))
    return e1 - e2 + lam_init


def _flash_body(q_ref, k_ref, v_ref, lq1_ref, lk1_ref, lq2_ref, lk2_ref, g_ref, o_ref,
                qs_ref, m_ref, l_ref, acc_ref, *, tq, lam_init):
    qi = pl.program_id(1)
    ki = pl.program_id(2)
    vd = ATT_VD

    @pl.when(ki == 0)
    def _():
        lane = lax.broadcasted_iota(jnp.int32, (tq, vd), 1)
        for h in range(ATT_HEADS):
            qh = q_ref[0, :, h * vd:(h + 1) * vd]
            zero = jnp.zeros_like(qh)
            qs_ref[h, 0:tq, :] = jnp.where(lane < ATT_HD, qh, zero)
            qs_ref[h, tq:2 * tq, :] = jnp.where(lane >= ATT_HD, qh, zero)
        m_ref[...] = jnp.full(m_ref.shape, NEG_INF, F32)
        l_ref[...] = jnp.zeros(l_ref.shape, F32)
        acc_ref[...] = jnp.zeros(acc_ref.shape, F32)

    def step(diagonal):
        if diagonal:
            row = lax.broadcasted_iota(jnp.int32, (2 * tq, tq), 0)
            col = lax.broadcasted_iota(jnp.int32, (2 * tq, tq), 1)
            visible = col <= jnp.where(row >= tq, row - tq, row)
        for h in range(ATT_HEADS):
            kh = k_ref[0, :, h * vd:(h + 1) * vd]
            vh = v_ref[0, :, h * vd:(h + 1) * vd]
            s = _dot_nt(qs_ref[h], kh)
            if diagonal:
                s = jnp.where(visible, s, NEG_INF)
            m_prev = m_ref[h]
            m_new = jnp.maximum(m_prev, jnp.max(s, axis=-1, keepdims=True))
            alpha = jnp.exp(m_prev - m_new)
            p = jnp.exp(s - m_new)
            l_ref[h] = alpha * l_ref[h] + jnp.sum(p, axis=-1, keepdims=True)
            acc_ref[h] = alpha * acc_ref[h] + _dot(p.astype(BF16), vh)
            m_ref[h] = m_new

    @pl.when(ki < qi)
    def _():
        step(False)

    @pl.when(ki == qi)
    def _():
        step(True)
        lam = _diff_lambda(lq1_ref, lk1_ref, lq2_ref, lk2_ref, lam_init)
        g = g_ref[...]
        for h in range(ATT_HEADS):
            o = acc_ref[h] / l_ref[h]
            att = o[0:tq] - lam * o[tq:2 * tq]
            att = _rms(att, g) * (1.0 - lam_init)
            o_ref[0, :, h * vd:(h + 1) * vd] = att.astype(o_ref.dtype)


def _flash(q, k, v, lam_p, g, lam_init):
    b, t, d = q.shape
    tq = ATT_TQ
    nq = t // tq
    vec = lambda a: a.reshape(1, -1)
    small = lambda n: pl.BlockSpec((1, n), lambda bi, qi, ki: (0, 0))
    kv_spec = pl.BlockSpec((1, tq, d), lambda bi, qi, ki: (bi, jnp.minimum(ki, qi), 0))
    return pl.pallas_call(
        functools.partial(_flash_body, tq=tq, lam_init=lam_init),
        grid=(b, nq, nq),
        in_specs=[pl.BlockSpec((1, tq, d), lambda bi, qi, ki: (bi, qi, 0)), kv_spec, kv_spec,
                  small(ATT_HD), small(ATT_HD), small(ATT_HD), small(ATT_HD), small(ATT_VD)],
        out_specs=pl.BlockSpec((1, tq, d), lambda bi, qi, ki: (bi, qi, 0)),
        out_shape=jax.ShapeDtypeStruct((b, t, d), BF16),
        scratch_shapes=[pltpu.VMEM((ATT_HEADS, 2 * tq, ATT_VD), BF16),
                        pltpu.VMEM((ATT_HEADS, 2 * tq, 1), F32),
                        pltpu.VMEM((ATT_HEADS, 2 * tq, 1), F32),
                        pltpu.VMEM((ATT_HEADS, 2 * tq, ATT_VD), F32)],
        compiler_params=_params(("parallel", "parallel", "arbitrary")),
        name="flash_diff_attn",
    )(q, k, v, *[vec(a) for a in lam_p], vec(g))


def _decode_body(pt_ref, q_ref, kn_ref, vn_ref, lq1_ref, lk1_ref, lq2_ref, lk2_ref, g_ref, *rest,
                 n_pages, lam_init):
    k_refs = rest[:n_pages]
    v_refs = rest[n_pages:2 * n_pages]
    o_ref, qt_ref, m_ref, l_ref, acc_ref = rest[2 * n_pages:]
    j = pl.program_id(1)
    d = ATT_HEADS * ATT_VD
    n_maps = 2 * ATT_HEADS
    row = lax.broadcasted_iota(jnp.int32, (n_maps, d), 0)
    col = lax.broadcasted_iota(jnp.int32, (n_maps, d), 1)

    @pl.when(j == 0)
    def _():
        qf = jnp.broadcast_to(q_ref[0].astype(F32), (n_maps, d))
        qt = jnp.where(col // ATT_HD == row, qf, 0.0)
        qt_ref[...] = qt.astype(BF16)
        kn = kn_ref[0].astype(BF16).astype(F32)
        m_ref[...] = jnp.sum(qt * kn, axis=-1, keepdims=True)
        l_ref[...] = jnp.ones(l_ref.shape, F32)
        acc_ref[...] = jnp.broadcast_to(vn_ref[0].astype(BF16).astype(F32), (n_maps, d))

    qt = qt_ref[...]
    s = jnp.concatenate([_dot_nt(qt, k_ref[0].astype(BF16)) for k_ref in k_refs], axis=1)
    m_prev = m_ref[...]
    m_new = jnp.maximum(m_prev, jnp.max(s, axis=-1, keepdims=True))
    alpha = jnp.exp(m_prev - m_new)
    p = jnp.exp(s - m_new)
    l_ref[...] = alpha * l_ref[...] + jnp.sum(p, axis=-1, keepdims=True)
    pb = p.astype(BF16)
    acc = alpha * acc_ref[...]
    for i, v_ref in enumerate(v_refs):
        acc = acc + _dot(pb[:, i * PAGE_SIZE:(i + 1) * PAGE_SIZE], v_ref[0].astype(BF16))
    acc_ref[...] = acc
    m_ref[...] = m_new

    @pl.when(j == pl.num_programs(1) - 1)
    def _():
        lam = _diff_lambda(lq1_ref, lk1_ref, lq2_ref, lk2_ref, lam_init)
        coef = jnp.where(row % 2 == 0, 1.0, -lam)
        own = col // ATT_VD == row // 2
        o = jnp.where(own, acc_ref[...] / l_ref[...] * coef, 0.0)
        att = jnp.sum(o, axis=0, keepdims=True)
        hrow = lax.broadcasted_iota(jnp.int32, (ATT_HEADS, d), 0)
        hcol = lax.broadcasted_iota(jnp.int32, (ATT_HEADS, d), 1)
        in_head = hcol // ATT_VD == hrow
        sq = jnp.where(in_head, jnp.broadcast_to(att * att, (ATT_HEADS, d)), 0.0)
        inv = lax.rsqrt(jnp.sum(sq, axis=-1, keepdims=True) * (1.0 / ATT_VD) + EPS)
        inv_cols = jnp.sum(jnp.where(in_head, inv, 0.0), axis=0, keepdims=True)
        o_ref[0] = (att * inv_cols * g_ref[...] * (1.0 - lam_init)).astype(o_ref.dtype)


def _decode(q, k_new, v_new, cache_k, cache_v, page_table, lam_p, g_tiled, lam_init):
    b, _, d = q.shape
    n_pages_total = page_table.shape[1]
    pp = DEC_PAGES
    vec = lambda a: a.reshape(1, -1)
    row_spec = pl.BlockSpec((1, 1, d), lambda bi, j, pt: (bi, 0, 0))
    small = lambda n: pl.BlockSpec((1, n), lambda bi, j, pt: (0, 0))
    page_specs = [pl.BlockSpec((1, PAGE_SIZE, d), lambda bi, j, pt, i=i: (pt[bi, j * pp + i], 0, 0))
                  for i in range(pp)]
    n_maps = 2 * ATT_HEADS
    grid_spec = pltpu.PrefetchScalarGridSpec(
        num_scalar_prefetch=1,
        grid=(b, n_pages_total // pp),
        in_specs=[row_spec, row_spec, row_spec,
                  small(ATT_HD), small(ATT_HD), small(ATT_HD), small(ATT_HD), small(d)]
                 + page_specs + page_specs,
        out_specs=pl.BlockSpec((1, 1, d), lambda bi, j, pt: (bi, 0, 0)),
        scratch_shapes=[pltpu.VMEM((n_maps, d), BF16),
                        pltpu.VMEM((n_maps, 1), F32),
                        pltpu.VMEM((n_maps, 1), F32),
                        pltpu.VMEM((n_maps, d), F32)],
    )
    return pl.pallas_call(
        functools.partial(_decode_body, n_pages=pp, lam_init=lam_init),
        grid_spec=grid_spec,
        out_shape=jax.ShapeDtypeStruct((b, 1, d), BF16),
        compiler_params=_params(("parallel", "arbitrary")),
        name="paged_diff_attn",
    )(page_table, q, k_new, v_new, *[vec(a) for a in lam_p], vec(g_tiled),
      *([cache_k] * pp), *([cache_v] * pp))


def _xattn_body(q_ref, k_ref, v_ref, o_ref):
    hd = XATT_HD
    for h in range(XATT_HEADS):
        q = q_ref[0, :, h * hd:(h + 1) * hd]
        k = k_ref[0, :, h * hd:(h + 1) * hd].astype(BF16)
        v = v_ref[0, :, h * hd:(h + 1) * hd].astype(BF16)
        s = _dot_nt(q, k)
        p = jnp.exp(s - jnp.max(s, axis=-1, keepdims=True))
        p = p / jnp.sum(p, axis=-1, keepdims=True)
        o_ref[0, :, h * hd:(h + 1) * hd] = _dot(p.astype(BF16), v).astype(o_ref.dtype)


def _xattn(q, mem_k, mem_v):
    b, t, d = q.shape
    tq = ROW_TILE if t % ROW_TILE == 0 else t
    kv_spec = pl.BlockSpec((1, N_MEM, d), lambda bi, i: (bi, 0, 0))
    return pl.pallas_call(
        _xattn_body,
        grid=(b, t // tq),
        in_specs=[pl.BlockSpec((1, tq, d), lambda bi, i: (bi, i, 0)), kv_spec, kv_spec],
        out_specs=pl.BlockSpec((1, tq, d), lambda bi, i: (bi, i, 0)),
        out_shape=jax.ShapeDtypeStruct((b, t, d), BF16),
        compiler_params=_params(("parallel", "parallel")),
        name="xattn",
    )(q, mem_k, mem_v)


def _block_diag(w):
    nb, n, _ = w.shape
    eye = jnp.eye(nb, dtype=w.dtype)
    return jnp.einsum('kij,kl->kilj', w, eye).reshape(nb * n, nb * n)


def _stack(arrays):
    return arrays[0][None] if len(arrays) == 1 else jnp.stack(arrays)


def _trunk(x3, mem_k, mem_v, p, even_mix, odd_mix):
    b, t, d = x3.shape
    x = x3.reshape(b * t, d)
    depth = p['ffn1_g'].shape[0]
    for l in range(depth):
        x = _ffn(x, p['ffn1_g'][l], p['ffn1_w_in'][l], p['ffn1_w_out'][l])
        if l % 2 == 0:
            x = even_mix(x, l // 2)
        else:
            x = odd_mix(x, l // 2, l)
        (q,) = _proj(x, p['xattn_g'][l], p['xattn_w_q'][l], d, [0], [(0, BF16, XATT_HD ** -0.5)])
        o = _xattn(q.reshape(b, t, d), mem_k[l], mem_v[l])
        x = _outproj(x, [o.reshape(b * t, d)], [p['xattn_w_out'][l]])
        last = l == depth - 1
        x = _ffn(x, p['ffn2_g'][l], p['ffn2_w_in'][l], p['ffn2_w_out'][l],
                 final_g=p['final_g'] if last else None)
    return x.reshape(b, t, d)


def kernel(x_prompt, x_sample, state_conv_a, state_conv_b, state_lru, cache_k, cache_v, cache_mem_k, cache_mem_v, page_table, mem_prompt, ffn1_g, ffn1_w_in, ffn1_w_out, mix_g, even_w_in, conv_a_w, conv_a_b, conv_a_ln_g, conv_a_ln_b, conv_b_w, conv_b_b, lru_w_a, lru_b_a, lru_w_x, lru_b_x, lru_lambda, even_w_out, attn_w_in, lam_q1, lam_k1, lam_q2, lam_k2, attn_subln_g, attn_w_out, xattn_g, xattn_w_q, xattn_w_kv, xattn_w_out, ffn2_g, ffn2_w_in, ffn2_w_out, final_g):
    bsz, seq, d = x_prompt.shape
    dbsz, dseq, _ = x_sample.shape
    depth = ffn1_g.shape[0]
    n_even = even_w_in.shape[0]
    n_odd = attn_w_in.shape[0]
    bf = lambda w: w.astype(BF16)
    p = {
        'ffn1_g': ffn1_g, 'ffn1_w_in': bf(ffn1_w_in), 'ffn1_w_out': bf(ffn1_w_out),
        'ffn2_g': ffn2_g, 'ffn2_w_in': bf(ffn2_w_in), 'ffn2_w_out': bf(ffn2_w_out),
        'xattn_g': xattn_g, 'xattn_w_q': bf(xattn_w_q), 'xattn_w_out': bf(xattn_w_out),
        'final_g': final_g,
    }
    even_w_in_b = bf(even_w_in)
    even_w_out_b = bf(even_w_out)
    attn_w_in_b = bf(attn_w_in)
    attn_w_out_b = bf(attn_w_out)
    xattn_w_kv_b = bf(xattn_w_kv)
    lru_wg = [bf(jnp.concatenate([_block_diag(lru_w_a[e]), _block_diag(lru_w_x[e])], axis=1)) for e in range(n_even)]
    lru_bg = [jnp.concatenate([lru_b_a[e], lru_b_x[e]]) for e in range(n_even)]
    c = CONV_CH
    qkv_outs = [(0, BF16, ATT_HD ** -0.5), (1, F32, 1.0), (2, F32, 1.0), (1, BF16, 1.0), (2, BF16, 1.0)]
    lam_inits = [0.8 - 0.6 * math.exp(-0.3 * (2 * o + 1)) for o in range(n_odd)]

    mem2 = mem_prompt.reshape(bsz * N_MEM, d)
    p_mem_k, p_mem_v = [], []
    for l in range(depth):
        mk, mv = _proj(mem2, None, xattn_w_kv_b[l], d, [0, 1], [(0, F32, 1.0), (1, F32, 1.0)])
        p_mem_k.append(mk.reshape(bsz, N_MEM, d))
        p_mem_v.append(mv.reshape(bsz, N_MEM, d))

    p_conv_a, p_conv_b, p_lru, p_k, p_v = [], [], [], [], []

    def p_even(x, e):
        u, br, gb = _even_in(x, mix_g[2 * e], even_w_in_b[e])
        u3, br3, gb3 = (a.reshape(bsz, seq, c) for a in (u, br, gb))
        ya = _conv_a(u3, conv_a_w[e], conv_a_b[e], conv_a_ln_g[e], conv_a_ln_b[e])
        yb, h_last = _lru(br3, gb3, conv_b_w[e], conv_b_b[e], lru_wg[e], lru_bg[e], lru_lambda[e])
        p_conv_a.append(u3[:, seq - (CONV_A_WIDTH - 1):])
        p_conv_b.append(br3[:, seq - (CONV_B_WIDTH - 1):])
        p_lru.append(h_last.reshape(bsz, c))
        return _outproj(x, [ya.reshape(-1, c), yb.reshape(-1, c)], [even_w_out_b[e][:c], even_w_out_b[e][c:]])

    def p_odd(x, o, l):
        q, k, v, kb, vb = _proj(x, mix_g[l], attn_w_in_b[o], d, [0, 1, 2], qkv_outs)
        p_k.append(k.reshape(bsz, seq, ATT_HEADS, ATT_VD))
        p_v.append(v.reshape(bsz, seq, ATT_HEADS, ATT_VD))
        att = _flash(q.reshape(bsz, seq, d), kb.reshape(bsz, seq, d), vb.reshape(bsz, seq, d),
                     (lam_q1[o], lam_k1[o], lam_q2[o], lam_k2[o]), attn_subln_g[o], lam_inits[o])
        return _outproj(x, [att.reshape(-1, d)], [attn_w_out_b[o]])

    y_prompt = _trunk(x_prompt, p_mem_k, p_mem_v, p, p_even, p_odd)

    s_conv_a, s_conv_b, s_lru, s_k, s_v = [], [], [], [], []
    pool = cache_k.shape[1]

    def s_even(x, e):
        u, br, gb = _even_in(x, mix_g[2 * e], even_w_in_b[e])
        ya, yb, h_new = _sample_mix(
            u, br, gb, jnp.swapaxes(state_conv_a[e], 0, 1), jnp.swapaxes(state_conv_b[e], 0, 1), state_lru[e],
            conv_a_w[e], conv_a_b[e], conv_a_ln_g[e], conv_a_ln_b[e],
            conv_b_w[e], conv_b_b[e], lru_wg[e], lru_bg[e], lru_lambda[e])
        s_conv_a.append(jnp.concatenate([state_conv_a[e][:, 1:], u[:, None, :]], axis=1))
        s_conv_b.append(jnp.concatenate([state_conv_b[e][:, 1:], br[:, None, :]], axis=1))
        s_lru.append(h_new)
        return _outproj(x, [ya, yb], [even_w_out_b[e][:c], even_w_out_b[e][c:]])

    def s_odd(x, o, l):
        q, k, v = _proj(x, mix_g[l], attn_w_in_b[o], d, [0, 1, 2],
                        [(0, BF16, ATT_HD ** -0.5), (1, F32, 1.0), (2, F32, 1.0)])
        s_k.append(k.reshape(dbsz, dseq, ATT_HEADS, ATT_VD))
        s_v.append(v.reshape(dbsz, dseq, ATT_HEADS, ATT_VD))
        att = _decode(q.reshape(dbsz, 1, d), k.reshape(dbsz, 1, d), v.reshape(dbsz, 1, d),
                      cache_k[o].reshape(pool, PAGE_SIZE, d), cache_v[o].reshape(pool, PAGE_SIZE, d),
                      page_table, (lam_q1[o], lam_k1[o], lam_q2[o], lam_k2[o]),
                      jnp.tile(attn_subln_g[o], ATT_HEADS), lam_inits[o])
        return _outproj(x, [att.reshape(dbsz, d)], [attn_w_out_b[o]])

    s_mem_k = [cache_mem_k[l].reshape(dbsz, N_MEM, d) for l in range(depth)]
    s_mem_v = [cache_mem_v[l].reshape(dbsz, N_MEM, d) for l in range(depth)]
    y_sample = _trunk(x_sample, s_mem_k, s_mem_v, p, s_even, s_odd)

    hx = (XATT_HEADS, XATT_HD)
    return (y_prompt, y_sample,
            _stack(p_conv_a), _stack(p_conv_b), _stack(p_lru),
            _stack(p_k), _stack(p_v),
            _stack(p_mem_k).reshape(depth, bsz, N_MEM, *hx),
            _stack(p_mem_v).reshape(depth, bsz, N_MEM, *hx),
            _stack(s_conv_a), _stack(s_conv_b), _stack(s_lru),
            _stack(s_k), _stack(s_v))
```

```python
import functools
import math

import jax
import jax.numpy as jnp
from jax import lax
from jax.experimental import pallas as pl
from jax.experimental.pallas import tpu as pltpu

F32 = jnp.float32
BF16 = jnp.bfloat16

D_MODEL = 1024
D_FF = 2816
CONV_CH = 512
CONV_A_WIDTH = 31
LRU_CH = 512
LRU_BLOCKS = 8
CONV_B_WIDTH = 4
LRU_C = 8.0
ATT_HEADS = 8
ATT_HD = 64
ATT_VD = 128
N_MEM = 256
XATT_HEADS = 4
XATT_HD = 256
PAGE_SIZE = 128
EPS = 1e-6
NEG_INF = -1e30

SUBLANES = 8
VMEM_LIMIT = 56 * 1024 * 1024

FFN_TF = 1408
ROW_TILE = 512
SEQ_TILE_CONV = 512
SEQ_TILE_LRU = 256
ATT_TQ = 512
DEC_PAGES = 8


def _params(sem):
    return pltpu.CompilerParams(dimension_semantics=sem, vmem_limit_bytes=VMEM_LIMIT)


def _rms(x, g):
    return x * lax.rsqrt(jnp.mean(x * x, axis=-1, keepdims=True) + EPS) * g


def _dot(a, b):
    return jnp.dot(a, b, preferred_element_type=F32)


def _dot_nt(a, b):
    return lax.dot_general(a, b, (((1,), (1,)), ((), ())), preferred_element_type=F32)


def _row_tile(m):
    return ROW_TILE if m % ROW_TILE == 0 else m


def _ffn_body(*refs, nf, final):
    if final:
        x_ref, g_ref, wg_ref, wu_ref, wo_ref, gf_ref, o_ref, xn_ref, acc_ref = refs
    else:
        x_ref, g_ref, wg_ref, wu_ref, wo_ref, o_ref, xn_ref, acc_ref = refs
    j = pl.program_id(1)

    @pl.when(j == 0)
    def _():
        xn_ref[...] = _rms(x_ref[...], g_ref[...]).astype(BF16)

    xn = xn_ref[...]
    gate = _dot(xn, wg_ref[...])
    up = _dot(xn, wu_ref[...])
    h = (gate * jax.nn.sigmoid(gate) * up).astype(BF16)
    part = _dot(h, wo_ref[...])

    @pl.when(j == 0)
    def _():
        acc_ref[...] = part

    @pl.when(j != 0)
    def _():
        acc_ref[...] += part

    @pl.when(j == nf - 1)
    def _():
        y = x_ref[...] + 0.5 * acc_ref[...]
        if final:
            y = _rms(y, gf_ref[...])
        o_ref[...] = y


def _ffn(x, g, w_in, w_out, final_g=None):
    m, d = x.shape
    f = w_out.shape[0]
    tm = _row_tile(m)
    nf = f // FFN_TF
    final = final_g is not None
    in_specs = [
        pl.BlockSpec((tm, d), lambda i, j: (i, 0)),
        pl.BlockSpec((1, d), lambda i, j: (0, 0)),
        pl.BlockSpec((d, FFN_TF), lambda i, j: (0, j)),
        pl.BlockSpec((d, FFN_TF), lambda i, j: (0, j + nf)),
        pl.BlockSpec((FFN_TF, d), lambda i, j: (j, 0)),
    ]
    args = [x, g.reshape(1, d), w_in, w_in, w_out]
    if final:
        in_specs.append(pl.BlockSpec((1, d), lambda i, j: (0, 0)))
        args.append(final_g.reshape(1, d))
    return pl.pallas_call(
        functools.partial(_ffn_body, nf=nf, final=final),
        grid=(m // tm, nf),
        in_specs=in_specs,
        out_specs=pl.BlockSpec((tm, d), lambda i, j: (i, 0)),
        out_shape=jax.ShapeDtypeStruct((m, d), F32),
        scratch_shapes=[pltpu.VMEM((tm, d), BF16), pltpu.VMEM((tm, d), F32)],
        compiler_params=_params(("parallel", "arbitrary")),
        name="ffn",
    )(*args)


def _proj_body(*refs, has_g, n_w, out_meta):
    x_ref = refs[0]
    pos = 1
    if has_g:
        g_ref = refs[pos]
        pos += 1
    w_refs = refs[pos:pos + n_w]
    o_refs = refs[pos + n_w:]
    x = x_ref[...]
    if has_g:
        x = _rms(x, g_ref[...])
    xb = x.astype(BF16)
    ys = [_dot(xb, w_ref[...]) for w_ref in w_refs]
    for o_ref, (wi, scale) in zip(o_refs, out_meta):
        y = ys[wi]
        if scale != 1.0:
            y = y * scale
        o_ref[...] = y.astype(o_ref.dtype)


def _proj(x, g, w, width, blocks, outs):
    m, k = x.shape
    tm = _row_tile(m)
    has_g = g is not None
    in_specs = [pl.BlockSpec((tm, k), lambda i: (i, 0))]
    args = [x]
    if has_g:
        in_specs.append(pl.BlockSpec((1, k), lambda i: (0, 0)))
        args.append(g.reshape(1, k))
    for b in blocks:
        in_specs.append(pl.BlockSpec((k, width), lambda i, b=b: (0, b)))
        args.append(w)
    return pl.pallas_call(
        functools.partial(_proj_body, has_g=has_g, n_w=len(blocks),
                          out_meta=tuple((wi, sc) for wi, _, sc in outs)),
        grid=(m // tm,),
        in_specs=in_specs,
        out_specs=[pl.BlockSpec((tm, width), lambda i: (i, 0)) for _ in outs],
        out_shape=[jax.ShapeDtypeStruct((m, width), dt) for _, dt, _ in outs],
        compiler_params=_params(("parallel",)),
        name="proj",
    )(*args)


def _outproj_body(*refs, n):
    x_ref = refs[0]
    y_refs = refs[1:1 + n]
    w_refs = refs[1 + n:1 + 2 * n]
    o_ref = refs[1 + 2 * n]
    acc = x_ref[...]
    for y_ref, w_ref in zip(y_refs, w_refs):
        acc = acc + _dot(y_ref[...], w_ref[...])
    o_ref[...] = acc


def _outproj(x, ys, ws):
    m, d = x.shape
    tm = _row_tile(m)
    n = len(ys)
    in_specs = [pl.BlockSpec((tm, d), lambda i: (i, 0))]
    in_specs += [pl.BlockSpec((tm, y.shape[1]), lambda i: (i, 0)) for y in ys]
    in_specs += [pl.BlockSpec(w.shape, lambda i: (0, 0)) for w in ws]
    return pl.pallas_call(
        functools.partial(_outproj_body, n=n),
        grid=(m // tm,),
        in_specs=in_specs,
        out_specs=pl.BlockSpec((tm, d), lambda i: (i, 0)),
        out_shape=jax.ShapeDtypeStruct((m, d), F32),
        compiler_params=_params(("parallel",)),
        name="outproj",
    )(x, *ys, *ws)


def _even_in_body(x_ref, g_ref, w_ref, u_ref, br_ref, gb_ref):
    xn = _rms(x_ref[...], g_ref[...]).astype(BF16)
    z = _dot(xn, w_ref[...])
    c = CONV_CH
    u_ref[...] = z[:, :c] * jax.nn.sigmoid(z[:, c:2 * c])
    br_ref[...] = z[:, 2 * c:2 * c + LRU_CH]
    gb_ref[...] = jax.nn.gelu(z[:, 2 * c + LRU_CH:])


def _even_in(x, g, w):
    m, d = x.shape
    tm = _row_tile(m)
    n = w.shape[1]
    return pl.pallas_call(
        _even_in_body,
        grid=(m // tm,),
        in_specs=[pl.BlockSpec((tm, d), lambda i: (i, 0)),
                  pl.BlockSpec((1, d), lambda i: (0, 0)),
                  pl.BlockSpec((d, n), lambda i: (0, 0))],
        out_specs=[pl.BlockSpec((tm, CONV_CH), lambda i: (i, 0)) for _ in range(3)],
        out_shape=[jax.ShapeDtypeStruct((m, CONV_CH), F32) for _ in range(3)],
        compiler_params=_params(("parallel",)),
        name="even_in",
    )(x, g.reshape(1, d), w)


def _layer_norm_silu(y, g, b):
    mu = jnp.mean(y, axis=-1, keepdims=True)
    yc = y - mu
    var = jnp.mean(yc * yc, axis=-1, keepdims=True)
    yn = yc * lax.rsqrt(var + EPS) * g + b
    return yn * jax.nn.sigmoid(yn)


CONV_HALO = 32
CONV_CHUNK = 32


def _conv_a_body(u_ref, prev_ref, cw_ref, cb_ref, lg_ref, lb_ref, o_ref, ext_ref, rot_ref, *, tt):
    i = pl.program_id(1)
    c = CONV_CH
    n_ext = tt + CONV_HALO

    @pl.when(i == 0)
    def _():
        ext_ref[0:CONV_HALO, :] = jnp.zeros((CONV_HALO, c), F32)

    @pl.when(i != 0)
    def _():
        ext_ref[0:CONV_HALO, :] = prev_ref[0]

    ext_ref[CONV_HALO:n_ext, :] = u_ref[0]
    ext_ref[n_ext:n_ext + SUBLANES, :] = jnp.zeros((SUBLANES, c), F32)
    for r in range(SUBLANES):
        rot_ref[r] = ext_ref[pl.ds(r, n_ext), :]
    first = CONV_HALO - (CONV_A_WIDTH - 1)
    cb = cb_ref[...]
    lg = lg_ref[...]
    lb = lb_ref[...]
    for r0 in range(0, tt, CONV_CHUNK):
        acc = jnp.zeros((CONV_CHUNK, c), F32)
        for w in range(CONV_A_WIDTH):
            s = first + w
            acc = acc + rot_ref[s % SUBLANES, pl.ds(r0 + (s // SUBLANES) * SUBLANES, CONV_CHUNK), :] * cw_ref[w:w + 1, :]
        o_ref[0, r0:r0 + CONV_CHUNK, :] = _layer_norm_silu(acc + cb, lg, lb).astype(o_ref.dtype)


def _conv_a(u, cw, cb, lg, lb):
    b, t, c = u.shape
    tt = SEQ_TILE_CONV
    halo_blocks = tt // CONV_HALO
    vec = lambda a: a.reshape(1, c)
    return pl.pallas_call(
        functools.partial(_conv_a_body, tt=tt),
        grid=(b, t // tt),
        in_specs=[pl.BlockSpec((1, tt, c), lambda bi, i: (bi, i, 0)),
                  pl.BlockSpec((1, CONV_HALO, c), lambda bi, i: (bi, jnp.maximum(i * halo_blocks - 1, 0), 0)),
                  pl.BlockSpec((CONV_A_WIDTH, c), lambda bi, i: (0, 0)),
                  pl.BlockSpec((1, c), lambda bi, i: (0, 0)),
                  pl.BlockSpec((1, c), lambda bi, i: (0, 0)),
                  pl.BlockSpec((1, c), lambda bi, i: (0, 0))],
        out_specs=pl.BlockSpec((1, tt, c), lambda bi, i: (bi, i, 0)),
        out_shape=jax.ShapeDtypeStruct((b, t, c), BF16),
        scratch_shapes=[pltpu.VMEM((tt + CONV_HALO + SUBLANES, c), F32),
                        pltpu.VMEM((SUBLANES, tt + CONV_HALO, c), F32)],
        compiler_params=_params(("parallel", "arbitrary")),
        name="conv_a",
    )(u, u, cw, vec(cb), vec(lg), vec(lb))


def _lru_gates(xr, wg_ref, bg_ref, lam_ref):
    gates = _dot(xr.astype(BF16), wg_ref[...]) + bg_ref[...]
    r = jax.nn.sigmoid(gates[:, :LRU_CH])
    ig = jax.nn.sigmoid(gates[:, LRU_CH:])
    nl = -lam_ref[...]
    softplus = jnp.maximum(nl, 0.0) + jnp.log1p(jnp.exp(-jnp.abs(nl)))
    log_a = -LRU_C * r * softplus
    a = jnp.exp(log_a)
    beta = jnp.sqrt(jnp.maximum(-jnp.tanh(log_a) * (a * a + 1.0), 0.0))
    return a, beta * ig * xr


def _lru_body(br_ref, prev_ref, gb_ref, cw_ref, cb_ref, wg_ref, bg_ref, lam_ref,
              yb_ref, hl_ref, ext_ref, h_ref, *, tt):
    i = pl.program_id(1)
    c = LRU_CH

    @pl.when(i == 0)
    def _():
        h_ref[...] = jnp.zeros((1, c), F32)
        ext_ref[0:SUBLANES, :] = jnp.zeros((SUBLANES, c), F32)

    @pl.when(i != 0)
    def _():
        ext_ref[0:SUBLANES, :] = prev_ref[0]

    ext_ref[SUBLANES:SUBLANES + tt, :] = br_ref[0]
    first = SUBLANES - (CONV_B_WIDTH - 1)
    xr = cb_ref[...]
    for w in range(CONV_B_WIDTH):
        xr = xr + ext_ref[pl.ds(first + w, tt), :] * cw_ref[w:w + 1, :]
    a, u = _lru_gates(xr, wg_ref, bg_ref, lam_ref)

    row = lax.broadcasted_iota(jnp.int32, (tt, c), 0)
    d = 1
    while d < tt:
        if d < SUBLANES:
            keep = row >= d
            a_sh = jnp.where(keep, pltpu.roll(a, d, 0), 1.0)
            u_sh = jnp.where(keep, pltpu.roll(u, d, 0), 0.0)
        else:
            a_sh = jnp.concatenate([jnp.ones((d, c), F32), a[:tt - d]], axis=0)
            u_sh = jnp.concatenate([jnp.zeros((d, c), F32), u[:tt - d]], axis=0)
        u = a * u_sh + u
        a = a * a_sh
        d *= 2
    h = a * h_ref[...] + u
    yb_ref[0] = (h * gb_ref[0]).astype(yb_ref.dtype)
    h_last = h[tt - 1:tt, :]
    h_ref[...] = h_last
    hl_ref[0] = h_last


def _lru(br, gb, cw, cb, wg, bg, lam):
    b, t, c = br.shape
    tt = SEQ_TILE_LRU
    halo_blocks = tt // SUBLANES
    vec = lambda a: a.reshape(1, -1)
    return pl.pallas_call(
        functools.partial(_lru_body, tt=tt),
        grid=(b, t // tt),
        in_specs=[pl.BlockSpec((1, tt, c), lambda bi, i: (bi, i, 0)),
                  pl.BlockSpec((1, SUBLANES, c), lambda bi, i: (bi, jnp.maximum(i * halo_blocks - 1, 0), 0)),
                  pl.BlockSpec((1, tt, c), lambda bi, i: (bi, i, 0)),
                  pl.BlockSpec((CONV_B_WIDTH, c), lambda bi, i: (0, 0)),
                  pl.BlockSpec((1, c), lambda bi, i: (0, 0)),
                  pl.BlockSpec((c, 2 * c), lambda bi, i: (0, 0)),
                  pl.BlockSpec((1, 2 * c), lambda bi, i: (0, 0)),
                  pl.BlockSpec((1, c), lambda bi, i: (0, 0))],
        out_specs=[pl.BlockSpec((1, tt, c), lambda bi, i: (bi, i, 0)),
                   pl.BlockSpec((1, 1, c), lambda bi, i: (bi, 0, 0))],
        out_shape=[jax.ShapeDtypeStruct((b, t, c), BF16),
                   jax.ShapeDtypeStruct((b, 1, c), F32)],
        scratch_shapes=[pltpu.VMEM((tt + SUBLANES, c), F32), pltpu.VMEM((1, c), F32)],
        compiler_params=_params(("parallel", "arbitrary")),
        name="lru",
    )(br, br, gb, cw, vec(cb), wg, vec(bg), vec(lam))


def _sample_mix_body(u_ref, br_ref, gb_ref, ha_ref, hb_ref, h0_ref,
                     cwa_ref, cba_ref, lg_ref, lb_ref, cwb_ref, cbb_ref, wg_ref, bg_ref, lam_ref,
                     ya_ref, yb_ref, h_ref):
    u = u_ref[...]
    acc = cba_ref[...] + u * cwa_ref[CONV_A_WIDTH - 1:CONV_A_WIDTH, :]
    for w in range(CONV_A_WIDTH - 1):
        acc = acc + ha_ref[w] * cwa_ref[w:w + 1, :]
    ya_ref[...] = _layer_norm_silu(acc, lg_ref[...], lb_ref[...]).astype(ya_ref.dtype)

    br = br_ref[...]
    xr = cbb_ref[...] + br * cwb_ref[CONV_B_WIDTH - 1:CONV_B_WIDTH, :]
    for w in range(CONV_B_WIDTH - 1):
        xr = xr + hb_ref[w] * cwb_ref[w:w + 1, :]
    a, x_in = _lru_gates(xr, wg_ref, bg_ref, lam_ref)
    h = a * h0_ref[...] + x_in
    h_ref[...] = h
    yb_ref[...] = (h * gb_ref[...]).astype(yb_ref.dtype)


def _sample_mix(u, br, gb, hist_a_t, hist_b_t, h0, cwa, cba, lg, lb, cwb, cbb, wg, bg, lam):
    n, c = u.shape
    vec = lambda a: a.reshape(1, -1)
    args = [u, br, gb, hist_a_t, hist_b_t, h0, cwa, vec(cba), vec(lg), vec(lb), cwb, vec(cbb), wg, vec(bg), vec(lam)]
    full = lambda a: pl.BlockSpec(a.shape, lambda i, nd=a.ndim: (0,) * nd)
    return pl.pallas_call(
        _sample_mix_body,
        grid=(1,),
        in_specs=[full(a) for a in args],
        out_specs=[pl.BlockSpec((n, c), lambda i: (0, 0)) for _ in range(3)],
        out_shape=[jax.ShapeDtypeStruct((n, c), BF16), jax.ShapeDtypeStruct((n, c), BF16),
                   jax.ShapeDtypeStruct((n, c), F32)],
        compiler_params=_params(("arbitrary",)),
        name="sample_mix",
    )(*args)


def _diff_lambda(lq1_ref, lk1_ref, lq2_ref, lk2_ref, lam_init):
    e1 = jnp.exp(jnp.sum(lq1_ref[...] * lk1_ref[...], axis=-1, keepdims=True))
    e2 = jnp.exp(jnp.sum(lq2_ref[...] * lk2_ref[...], axis=-1, keepdims=True))
    return e1 - e2 + lam_init


Q_SCALE = ATT_HD ** -0.5 * math.log2(math.e)


def _qkv_body(x_ref, g_ref, wq_ref, wk_ref, wv_ref, qt_ref, k_ref, v_ref, kb_ref, vt_ref):
    xn = _rms(x_ref[...], g_ref[...]).astype(BF16)
    q = _dot(xn, wq_ref[...]) * Q_SCALE
    qt_ref[0] = q.T.astype(BF16)
    k = _dot(xn, wk_ref[...])
    k_ref[...] = k
    kb_ref[...] = k.astype(BF16)
    v = _dot(xn, wv_ref[...])
    v_ref[...] = v
    vt_ref[0] = v.T.astype(BF16)


def _qkv(x, g, w, bsz, seq):
    m, d = x.shape
    tm = ROW_TILE
    per_seq = seq // tm
    row = pl.BlockSpec((tm, d), lambda i: (i, 0))
    col = pl.BlockSpec((1, d, tm), lambda i: (i // per_seq, 0, i % per_seq))
    wspec = lambda b: pl.BlockSpec((d, d), lambda i: (0, b))
    return pl.pallas_call(
        _qkv_body,
        grid=(m // tm,),
        in_specs=[row, pl.BlockSpec((1, d), lambda i: (0, 0)), wspec(0), wspec(1), wspec(2)],
        out_specs=[col, row, row, row, col],
        out_shape=[jax.ShapeDtypeStruct((bsz, d, seq), BF16),
                   jax.ShapeDtypeStruct((m, d), F32),
                   jax.ShapeDtypeStruct((m, d), F32),
                   jax.ShapeDtypeStruct((m, d), BF16),
                   jax.ShapeDtypeStruct((bsz, d, seq), BF16)],
        compiler_params=_params(("parallel",)),
        name="qkv",
    )(x, g.reshape(1, d), w, w, w)


def _flash_body(qt_ref, k_ref, vt_ref, lq1_ref, lk1_ref, lq2_ref, lk2_ref, g_ref, o_ref,
                qs_ref, m_ref, l_ref, acc_ref, *, tq, lam_init):
    qi = pl.program_id(1)
    ki = pl.program_id(2)
    vd = ATT_VD

    @pl.when(ki == 0)
    def _():
        dim = lax.broadcasted_iota(jnp.int32, (vd, tq), 0)
        for h in range(ATT_HEADS):
            qh = qt_ref[0, h * vd:(h + 1) * vd, :]
            zero = jnp.zeros_like(qh)
            qs_ref[h, :, 0:tq] = jnp.where(dim < ATT_HD, qh, zero)
            qs_ref[h, :, tq:2 * tq] = jnp.where(dim >= ATT_HD, qh, zero)
        m_ref[...] = jnp.full(m_ref.shape, NEG_INF, F32)
        l_ref[...] = jnp.zeros(l_ref.shape, F32)
        acc_ref[...] = jnp.zeros(acc_ref.shape, F32)

    def step(diagonal):
        if diagonal:
            key = lax.broadcasted_iota(jnp.int32, (tq, 2 * tq), 0)
            qry = lax.broadcasted_iota(jnp.int32, (tq, 2 * tq), 1)
            visible = key <= jnp.where(qry >= tq, qry - tq, qry)
        for h in range(ATT_HEADS):
            kh = k_ref[0, :, h * vd:(h + 1) * vd]
            s = _dot(kh, qs_ref[h])
            if diagonal:
                s = jnp.where(visible, s, NEG_INF)
            m_prev = m_ref[h:h + 1, :]
            m_new = jnp.maximum(m_prev, jnp.max(s, axis=0, keepdims=True))
            alpha = jnp.exp2(m_prev - m_new)
            p = jnp.exp2(s - m_new)
            l_ref[h:h + 1, :] = alpha * l_ref[h:h + 1, :] + jnp.sum(p, axis=0, keepdims=True)
            vth = vt_ref[0, h * vd:(h + 1) * vd, :]
            acc_ref[h] = alpha * acc_ref[h] + _dot(vth, p.astype(BF16))
            m_ref[h:h + 1, :] = m_new

    @pl.when(ki < qi)
    def _():
        step(False)

    @pl.when(ki == qi)
    def _():
        step(True)
        lam = _diff_lambda(lq1_ref, lk1_ref, lq2_ref, lk2_ref, lam_init)
        g = g_ref[...]
        for h in range(ATT_HEADS):
            o = acc_ref[h] / l_ref[h:h + 1, :]
            att = o[:, 0:tq] - lam * o[:, tq:2 * tq]
            ms = jnp.mean(att * att, axis=0, keepdims=True)
            att = att * lax.rsqrt(ms + EPS) * g * (1.0 - lam_init)
            o_ref[0, :, h * vd:(h + 1) * vd] = att.T.astype(o_ref.dtype)


def _flash(qt, k, vt, lam_p, g, lam_init):
    b, d, t = qt.shape
    tq = ATT_TQ
    nq = t // tq
    vec = lambda a: a.reshape(1, -1)
    small = lambda n: pl.BlockSpec((1, n), lambda bi, qi, ki: (0, 0))
    return pl.pallas_call(
        functools.partial(_flash_body, tq=tq, lam_init=lam_init),
        grid=(b, nq, nq),
        in_specs=[pl.BlockSpec((1, d, tq), lambda bi, qi, ki: (bi, 0, qi)),
                  pl.BlockSpec((1, tq, d), lambda bi, qi, ki: (bi, jnp.minimum(ki, qi), 0)),
                  pl.BlockSpec((1, d, tq), lambda bi, qi, ki: (bi, 0, jnp.minimum(ki, qi))),
                  small(ATT_HD), small(ATT_HD), small(ATT_HD), small(ATT_HD),
                  pl.BlockSpec((ATT_VD, 1), lambda bi, qi, ki: (0, 0))],
        out_specs=pl.BlockSpec((1, tq, d), lambda bi, qi, ki: (bi, qi, 0)),
        out_shape=jax.ShapeDtypeStruct((b, t, d), BF16),
        scratch_shapes=[pltpu.VMEM((ATT_HEADS, ATT_VD, 2 * tq), BF16),
                        pltpu.VMEM((ATT_HEADS, 2 * tq), F32),
                        pltpu.VMEM((ATT_HEADS, 2 * tq), F32),
                        pltpu.VMEM((ATT_HEADS, ATT_VD, 2 * tq), F32)],
        compiler_params=_params(("parallel", "parallel", "arbitrary")),
        name="flash_diff_attn",
    )(qt, k, vt, *[vec(a) for a in lam_p], g.reshape(ATT_VD, 1))


def _decode_body(pt_ref, q_ref, kn_ref, vn_ref, lq1_ref, lk1_ref, lq2_ref, lk2_ref, g_ref, *rest,
                 n_pages, lam_init):
    k_refs = rest[:n_pages]
    v_refs = rest[n_pages:2 * n_pages]
    o_ref, qx_ref, m_ref, l_ref, acc_ref = rest[2 * n_pages:]
    j = pl.program_id(1)
    nh = ATT_HEADS
    rows = PAGE_SIZE * nh

    @pl.when(j == 0)
    def _():
        q8 = q_ref[0].astype(F32)
        lane = lax.broadcasted_iota(jnp.int32, (nh, ATT_VD), 1)
        qx = jnp.concatenate([jnp.where(lane < ATT_HD, q8, 0.0), jnp.where(lane >= ATT_HD, q8, 0.0)], axis=0)
        qx_ref[...] = qx.astype(BF16)
        kn = kn_ref[0].astype(BF16).astype(F32)
        vn = vn_ref[0].astype(BF16).astype(F32)
        m_ref[...] = jnp.sum(qx * jnp.concatenate([kn, kn], axis=0), axis=-1, keepdims=True)
        l_ref[...] = jnp.ones(l_ref.shape, F32)
        acc_ref[...] = jnp.concatenate([vn, vn], axis=0)

    qx = qx_ref[...]
    s = jnp.concatenate([_dot_nt(qx, k_ref[0].reshape(rows, ATT_VD).astype(BF16)) for k_ref in k_refs], axis=1)
    r = lax.broadcasted_iota(jnp.int32, s.shape, 0)
    c = lax.broadcasted_iota(jnp.int32, s.shape, 1)
    s = jnp.where(c % nh == r % nh, s, NEG_INF)
    m_prev = m_ref[...]
    m_new = jnp.maximum(m_prev, jnp.max(s, axis=-1, keepdims=True))
    alpha = jnp.exp2(m_prev - m_new)
    p = jnp.exp2(s - m_new)
    l_ref[...] = alpha * l_ref[...] + jnp.sum(p, axis=-1, keepdims=True)
    pb = p.astype(BF16)
    acc = alpha * acc_ref[...]
    for i, v_ref in enumerate(v_refs):
        acc = acc + _dot(pb[:, i * rows:(i + 1) * rows], v_ref[0].reshape(rows, ATT_VD).astype(BF16))
    acc_ref[...] = acc
    m_ref[...] = m_new

    @pl.when(j == pl.num_programs(1) - 1)
    def _():
        lam = _diff_lambda(lq1_ref, lk1_ref, lq2_ref, lk2_ref, lam_init)
        o = acc_ref[...] / l_ref[...]
        att = o[0:nh] - lam * o[nh:2 * nh]
        o_ref[0] = (_rms(att, g_ref[...]) * (1.0 - lam_init)).astype(o_ref.dtype)


def _decode(q, k_new, v_new, cache_k, cache_v, page_base, page_table, lam_p, g, lam_init):
    b = q.shape[0]
    nh, vd = ATT_HEADS, ATT_VD
    n_pages_total = page_table.shape[1]
    pp = DEC_PAGES
    vec = lambda a: a.reshape(1, -1)
    row_spec = pl.BlockSpec((1, nh, vd), lambda bi, j, pt: (bi, 0, 0))
    small = lambda n: pl.BlockSpec((1, n), lambda bi, j, pt: (0, 0))
    page_specs = [pl.BlockSpec((1, PAGE_SIZE, nh, vd),
                               lambda bi, j, pt, i=i: (page_base + pt[bi, j * pp + i], 0, 0, 0))
                  for i in range(pp)]
    grid_spec = pltpu.PrefetchScalarGridSpec(
        num_scalar_prefetch=1,
        grid=(b, n_pages_total // pp),
        in_specs=[row_spec, row_spec, row_spec,
                  small(ATT_HD), small(ATT_HD), small(ATT_HD), small(ATT_HD), small(vd)]
                 + page_specs + page_specs,
        out_specs=pl.BlockSpec((1, nh, vd), lambda bi, j, pt: (bi, 0, 0)),
        scratch_shapes=[pltpu.VMEM((2 * nh, vd), BF16),
                        pltpu.VMEM((2 * nh, 1), F32),
                        pltpu.VMEM((2 * nh, 1), F32),
                        pltpu.VMEM((2 * nh, vd), F32)],
    )
    return pl.pallas_call(
        functools.partial(_decode_body, n_pages=pp, lam_init=lam_init),
        grid_spec=grid_spec,
        out_shape=jax.ShapeDtypeStruct((b, nh, vd), BF16),
        compiler_params=_params(("parallel", "arbitrary")),
        name="paged_diff_attn",
    )(page_table, q, k_new, v_new, *[vec(a) for a in lam_p], vec(g),
      *([cache_k] * pp), *([cache_v] * pp))


def _xattn_body(q_ref, k_ref, v_ref, o_ref):
    hd = XATT_HD
    for h in range(XATT_HEADS):
        q = q_ref[0, :, h * hd:(h + 1) * hd]
        k = k_ref[0, :, h * hd:(h + 1) * hd].astype(BF16)
        v = v_ref[0, :, h * hd:(h + 1) * hd].astype(BF16)
        s = _dot_nt(q, k)
        p = jnp.exp(s - jnp.max(s, axis=-1, keepdims=True))
        p = p / jnp.sum(p, axis=-1, keepdims=True)
        o_ref[0, :, h * hd:(h + 1) * hd] = _dot(p.astype(BF16), v).astype(o_ref.dtype)


def _xattn(q, mem_k, mem_v):
    b, t, d = q.shape
    tq = ROW_TILE if t % ROW_TILE == 0 else t
    kv_spec = pl.BlockSpec((1, N_MEM, d), lambda bi, i: (bi, 0, 0))
    return pl.pallas_call(
        _xattn_body,
        grid=(b, t // tq),
        in_specs=[pl.BlockSpec((1, tq, d), lambda bi, i: (bi, i, 0)), kv_spec, kv_spec],
        out_specs=pl.BlockSpec((1, tq, d), lambda bi, i: (bi, i, 0)),
        out_shape=jax.ShapeDtypeStruct((b, t, d), BF16),
        compiler_params=_params(("parallel", "parallel")),
        name="xattn",
    )(q, mem_k, mem_v)


def _block_diag(w):
    nb, n, _ = w.shape
    eye = jnp.eye(nb, dtype=w.dtype)
    return jnp.einsum('kij,kl->kilj', w, eye).reshape(nb * n, nb * n)


def _stack(arrays):
    return arrays[0][None] if len(arrays) == 1 else jnp.stack(arrays)


def _trunk(x3, mem_k, mem_v, p, even_mix, odd_mix):
    b, t, d = x3.shape
    x = x3.reshape(b * t, d)
    depth = p['ffn1_g'].shape[0]
    for l in range(depth):
        x = _ffn(x, p['ffn1_g'][l], p['ffn1_w_in'][l], p['ffn1_w_out'][l])
        if l % 2 == 0:
            x = even_mix(x, l // 2)
        else:
            x = odd_mix(x, l // 2, l)
        (q,) = _proj(x, p['xattn_g'][l], p['xattn_w_q'][l], d, [0], [(0, BF16, XATT_HD ** -0.5)])
        o = _xattn(q.reshape(b, t, d), mem_k[l], mem_v[l])
        x = _outproj(x, [o.reshape(b * t, d)], [p['xattn_w_out'][l]])
        last = l == depth - 1
        x = _ffn(x, p['ffn2_g'][l], p['ffn2_w_in'][l], p['ffn2_w_out'][l],
                 final_g=p['final_g'] if last else None)
    return x.reshape(b, t, d)


def kernel(x_prompt, x_sample, state_conv_a, state_conv_b, state_lru, cache_k, cache_v, cache_mem_k, cache_mem_v, page_table, mem_prompt, ffn1_g, ffn1_w_in, ffn1_w_out, mix_g, even_w_in, conv_a_w, conv_a_b, conv_a_ln_g, conv_a_ln_b, conv_b_w, conv_b_b, lru_w_a, lru_b_a, lru_w_x, lru_b_x, lru_lambda, even_w_out, attn_w_in, lam_q1, lam_k1, lam_q2, lam_k2, attn_subln_g, attn_w_out, xattn_g, xattn_w_q, xattn_w_kv, xattn_w_out, ffn2_g, ffn2_w_in, ffn2_w_out, final_g):
    bsz, seq, d = x_prompt.shape
    dbsz, dseq, _ = x_sample.shape
    depth = ffn1_g.shape[0]
    n_even = even_w_in.shape[0]
    n_odd = attn_w_in.shape[0]
    bf = lambda w: w.astype(BF16)
    p = {
        'ffn1_g': ffn1_g, 'ffn1_w_in': bf(ffn1_w_in), 'ffn1_w_out': bf(ffn1_w_out),
        'ffn2_g': ffn2_g, 'ffn2_w_in': bf(ffn2_w_in), 'ffn2_w_out': bf(ffn2_w_out),
        'xattn_g': xattn_g, 'xattn_w_q': bf(xattn_w_q), 'xattn_w_out': bf(xattn_w_out),
        'final_g': final_g,
    }
    even_w_in_b = bf(even_w_in)
    even_w_out_b = bf(even_w_out)
    attn_w_in_b = bf(attn_w_in)
    attn_w_out_b = bf(attn_w_out)
    xattn_w_kv_b = bf(xattn_w_kv)
    lru_wg = [bf(jnp.concatenate([_block_diag(lru_w_a[e]), _block_diag(lru_w_x[e])], axis=1)) for e in range(n_even)]
    lru_bg = [jnp.concatenate([lru_b_a[e], lru_b_x[e]]) for e in range(n_even)]
    c = CONV_CH
    lam_inits = [0.8 - 0.6 * math.exp(-0.3 * (2 * o + 1)) for o in range(n_odd)]

    mem2 = mem_prompt.reshape(bsz * N_MEM, d)
    p_mem_k, p_mem_v = [], []
    for l in range(depth):
        mk, mv = _proj(mem2, None, xattn_w_kv_b[l], d, [0, 1], [(0, F32, 1.0), (1, F32, 1.0)])
        p_mem_k.append(mk.reshape(bsz, N_MEM, d))
        p_mem_v.append(mv.reshape(bsz, N_MEM, d))

    p_conv_a, p_conv_b, p_lru, p_k, p_v = [], [], [], [], []

    def p_even(x, e):
        u, br, gb = _even_in(x, mix_g[2 * e], even_w_in_b[e])
        u3, br3, gb3 = (a.reshape(bsz, seq, c) for a in (u, br, gb))
        ya = _conv_a(u3, conv_a_w[e], conv_a_b[e], conv_a_ln_g[e], conv_a_ln_b[e])
        yb, h_last = _lru(br3, gb3, conv_b_w[e], conv_b_b[e], lru_wg[e], lru_bg[e], lru_lambda[e])
        p_conv_a.append(u3[:, seq - (CONV_A_WIDTH - 1):])
        p_conv_b.append(br3[:, seq - (CONV_B_WIDTH - 1):])
        p_lru.append(h_last.reshape(bsz, c))
        return _outproj(x, [ya.reshape(-1, c), yb.reshape(-1, c)], [even_w_out_b[e][:c], even_w_out_b[e][c:]])

    def p_odd(x, o, l):
        qt, k, v, kb, vt = _qkv(x, mix_g[l], attn_w_in_b[o], bsz, seq)
        p_k.append(k.reshape(bsz, seq, ATT_HEADS, ATT_VD))
        p_v.append(v.reshape(bsz, seq, ATT_HEADS, ATT_VD))
        att = _flash(qt, kb.reshape(bsz, seq, d), vt,
                     (lam_q1[o], lam_k1[o], lam_q2[o], lam_k2[o]), attn_subln_g[o], lam_inits[o])
        return _outproj(x, [att.reshape(-1, d)], [attn_w_out_b[o]])

    y_prompt = _trunk(x_prompt, p_mem_k, p_mem_v, p, p_even, p_odd)

    s_conv_a, s_conv_b, s_lru, s_k, s_v = [], [], [], [], []
    pool = cache_k.shape[1]
    pages_k = cache_k.reshape(n_odd * pool, PAGE_SIZE, ATT_HEADS, ATT_VD)
    pages_v = cache_v.reshape(n_odd * pool, PAGE_SIZE, ATT_HEADS, ATT_VD)

    def s_even(x, e):
        u, br, gb = _even_in(x, mix_g[2 * e], even_w_in_b[e])
        ya, yb, h_new = _sample_mix(
            u, br, gb, jnp.swapaxes(state_conv_a[e], 0, 1), jnp.swapaxes(state_conv_b[e], 0, 1), state_lru[e],
            conv_a_w[e], conv_a_b[e], conv_a_ln_g[e], conv_a_ln_b[e],
            conv_b_w[e], conv_b_b[e], lru_wg[e], lru_bg[e], lru_lambda[e])
        s_conv_a.append(jnp.concatenate([state_conv_a[e][:, 1:], u[:, None, :]], axis=1))
        s_conv_b.append(jnp.concatenate([state_conv_b[e][:, 1:], br[:, None, :]], axis=1))
        s_lru.append(h_new)
        return _outproj(x, [ya, yb], [even_w_out_b[e][:c], even_w_out_b[e][c:]])

    def s_odd(x, o, l):
        q, k, v = _proj(x, mix_g[l], attn_w_in_b[o], d, [0, 1, 2],
                        [(0, BF16, Q_SCALE), (1, F32, 1.0), (2, F32, 1.0)])
        heads = (dbsz, ATT_HEADS, ATT_VD)
        s_k.append(k.reshape(dbsz, dseq, ATT_HEADS, ATT_VD))
        s_v.append(v.reshape(dbsz, dseq, ATT_HEADS, ATT_VD))
        att = _decode(q.reshape(heads), k.reshape(heads), v.reshape(heads), pages_k, pages_v, o * pool,
                      page_table, (lam_q1[o], lam_k1[o], lam_q2[o], lam_k2[o]), attn_subln_g[o], lam_inits[o])
        return _outproj(x, [att.reshape(dbsz, d)], [attn_w_out_b[o]])

    s_mem_k = [cache_mem_k[l].reshape(dbsz, N_MEM, d) for l in range(depth)]
    s_mem_v = [cache_mem_v[l].reshape(dbsz, N_MEM, d) for l in range(depth)]
    y_sample = _trunk(x_sample, s_mem_k, s_mem_v, p, s_even, s_odd)

    hx = (XATT_HEADS, XATT_HD)
    return (y_prompt, y_sample,
            _stack(p_conv_a), _stack(p_conv_b), _stack(p_lru),
            _stack(p_k), _stack(p_v),
            _stack(p_mem_k).reshape(depth, bsz, N_MEM, *hx),
            _stack(p_mem_v).reshape(depth, bsz, N_MEM, *hx),
            _stack(s_conv_a), _stack(s_conv_b), _stack(s_lru),
            _stack(s_k), _stack(s_v))
```

```python
import functools
import math

import jax
import jax.numpy as jnp
from jax import lax
from jax.experimental import pallas as pl
from jax.experimental.pallas import tpu as pltpu

F32 = jnp.float32
BF16 = jnp.bfloat16

D_MODEL = 1024
D_FF = 2816
CONV_CH = 512
CONV_A_WIDTH = 31
LRU_CH = 512
LRU_BLOCKS = 8
CONV_B_WIDTH = 4
LRU_C = 8.0
ATT_HEADS = 8
ATT_HD = 64
ATT_VD = 128
N_MEM = 256
XATT_HEADS = 4
XATT_HD = 256
PAGE_SIZE = 128
EPS = 1e-6
NEG_INF = -1e30

SUBLANES = 8
VMEM_LIMIT = 56 * 1024 * 1024

MXU_COLS = 256
FFN_CHUNK = 4 * MXU_COLS
XS_BATCH = 4
ROW_TILE = 512
SEQ_TILE_CONV = 512
SEQ_TILE_LRU = 256
ATT_TQ = 512
DEC_PAGES = 8


def _params(sem):
    return pltpu.CompilerParams(dimension_semantics=sem, vmem_limit_bytes=VMEM_LIMIT)


def _rms(x, g):
    return x * lax.rsqrt(jnp.mean(x * x, axis=-1, keepdims=True) + EPS) * g


def _dot(a, b):
    return jnp.dot(a, b, preferred_element_type=F32)


def _dot_nt(a, b):
    return lax.dot_general(a, b, (((1,), (1,)), ((), ())), preferred_element_type=F32)


def _row_tile(m):
    return ROW_TILE if m % ROW_TILE == 0 else m


def _store_heads(o_ref, y, n_heads):
    hd = y.shape[1] // n_heads
    for h in range(n_heads):
        o_ref[:, h, :] = y[:, h * hd:(h + 1) * hd]


def _ffn_chunks(f):
    return [(c0, min(FFN_CHUNK, f - c0)) for c0 in range(0, f, FFN_CHUNK)]


def _ffn_body(*refs, final):
    if final:
        x_ref, g_ref, wi_ref, wo_ref, gf_ref, o_ref = refs
    else:
        x_ref, g_ref, wi_ref, wo_ref, o_ref = refs
    f = wo_ref.shape[0]
    x = x_ref[...]
    xn = _rms(x, g_ref[...]).astype(BF16)
    acc = None
    for c0, cw in _ffn_chunks(f):
        gate = _dot(xn, wi_ref[:, c0:c0 + cw])
        up = _dot(xn, wi_ref[:, f + c0:f + c0 + cw])
        h = (gate * jax.nn.sigmoid(gate) * up).astype(BF16)
        part = _dot(h, wo_ref[c0:c0 + cw, :])
        acc = part if acc is None else acc + part
    y = x + 0.5 * acc
    if final:
        y = _rms(y, gf_ref[...])
    o_ref[...] = y


def _ffn(x, g, w_in, w_out, final_g=None):
    m, d = x.shape
    f = w_out.shape[0]
    tm = _row_tile(m)
    final = final_g is not None
    const = lambda shape: pl.BlockSpec(shape, lambda i: (0, 0), pipeline_mode=pl.Buffered(1))
    in_specs = [pl.BlockSpec((tm, d), lambda i: (i, 0)), const((1, d)), const((d, 2 * f)), const((f, d))]
    args = [x, g.reshape(1, d), w_in, w_out]
    if final:
        in_specs.append(const((1, d)))
        args.append(final_g.reshape(1, d))
    return pl.pallas_call(
        functools.partial(_ffn_body, final=final),
        grid=(m // tm,),
        in_specs=in_specs,
        out_specs=pl.BlockSpec((tm, d), lambda i: (i, 0)),
        out_shape=jax.ShapeDtypeStruct((m, d), F32),
        compiler_params=_params(("parallel",)),
        name="ffn",
    )(*args)


def _proj_body(*refs, has_g, n_w, out_meta):
    x_ref = refs[0]
    pos = 1
    if has_g:
        g_ref = refs[pos]
        pos += 1
    w_refs = refs[pos:pos + n_w]
    o_refs = refs[pos + n_w:]
    x = x_ref[...]
    if has_g:
        x = _rms(x, g_ref[...])
    xb = x.astype(BF16)
    ys = [_dot(xb, w_ref[...]) for w_ref in w_refs]
    for o_ref, (wi, scale) in zip(o_refs, out_meta):
        y = ys[wi]
        if scale != 1.0:
            y = y * scale
        o_ref[...] = y.astype(o_ref.dtype)


def _proj(x, g, w, width, blocks, outs):
    m, k = x.shape
    tm = _row_tile(m)
    has_g = g is not None
    in_specs = [pl.BlockSpec((tm, k), lambda i: (i, 0))]
    args = [x]
    if has_g:
        in_specs.append(pl.BlockSpec((1, k), lambda i: (0, 0)))
        args.append(g.reshape(1, k))
    for b in blocks:
        in_specs.append(pl.BlockSpec((k, width), lambda i, b=b: (0, b)))
        args.append(w)
    return pl.pallas_call(
        functools.partial(_proj_body, has_g=has_g, n_w=len(blocks),
                          out_meta=tuple((wi, sc) for wi, _, sc in outs)),
        grid=(m // tm,),
        in_specs=in_specs,
        out_specs=[pl.BlockSpec((tm, width), lambda i: (i, 0)) for _ in outs],
        out_shape=[jax.ShapeDtypeStruct((m, width), dt) for _, dt, _ in outs],
        compiler_params=_params(("parallel",)),
        name="proj",
    )(*args)


def _outproj_body(*refs, n):
    x_ref = refs[0]
    y_refs = refs[1:1 + n]
    w_refs = refs[1 + n:1 + 2 * n]
    o_ref = refs[1 + 2 * n]
    acc = x_ref[...]
    for y_ref, w_ref in zip(y_refs, w_refs):
        acc = acc + _dot(y_ref[...], w_ref[...])
    o_ref[...] = acc


def _outproj(x, ys, ws):
    m, d = x.shape
    tm = _row_tile(m)
    n = len(ys)
    in_specs = [pl.BlockSpec((tm, d), lambda i: (i, 0))]
    in_specs += [pl.BlockSpec((tm, y.shape[1]), lambda i: (i, 0)) for y in ys]
    in_specs += [pl.BlockSpec(w.shape, lambda i: (0, 0)) for w in ws]
    return pl.pallas_call(
        functools.partial(_outproj_body, n=n),
        grid=(m // tm,),
        in_specs=in_specs,
        out_specs=pl.BlockSpec((tm, d), lambda i: (i, 0)),
        out_shape=jax.ShapeDtypeStruct((m, d), F32),
        compiler_params=_params(("parallel",)),
        name="outproj",
    )(x, *ys, *ws)


def _even_in_body(x_ref, g_ref, w_ref, u_ref, br_ref, gb_ref):
    xn = _rms(x_ref[...], g_ref[...]).astype(BF16)
    z = _dot(xn, w_ref[...])
    c = CONV_CH
    u_ref[...] = z[:, :c] * jax.nn.sigmoid(z[:, c:2 * c])
    br_ref[...] = z[:, 2 * c:2 * c + LRU_CH]
    gb_ref[...] = jax.nn.gelu(z[:, 2 * c + LRU_CH:])


def _even_in(x, g, w):
    m, d = x.shape
    tm = _row_tile(m)
    n = w.shape[1]
    return pl.pallas_call(
        _even_in_body,
        grid=(m // tm,),
        in_specs=[pl.BlockSpec((tm, d), lambda i: (i, 0)),
                  pl.BlockSpec((1, d), lambda i: (0, 0)),
                  pl.BlockSpec((d, n), lambda i: (0, 0))],
        out_specs=[pl.BlockSpec((tm, CONV_CH), lambda i: (i, 0)) for _ in range(3)],
        out_shape=[jax.ShapeDtypeStruct((m, CONV_CH), F32) for _ in range(3)],
        compiler_params=_params(("parallel",)),
        name="even_in",
    )(x, g.reshape(1, d), w)


def _layer_norm_silu(y, g, b):
    mu = jnp.mean(y, axis=-1, keepdims=True)
    yc = y - mu
    var = jnp.mean(yc * yc, axis=-1, keepdims=True)
    yn = yc * lax.rsqrt(var + EPS) * g + b
    return yn * jax.nn.sigmoid(yn)


CONV_HALO = 32
CONV_CHUNK = 32


def _conv_a_body(u_ref, prev_ref, cw_ref, cb_ref, lg_ref, lb_ref, o_ref, ext_ref, rot_ref, *, tt):
    i = pl.program_id(1)
    c = CONV_CH
    n_ext = tt + CONV_HALO

    @pl.when(i == 0)
    def _():
        ext_ref[0:CONV_HALO, :] = jnp.zeros((CONV_HALO, c), F32)

    @pl.when(i != 0)
    def _():
        ext_ref[0:CONV_HALO, :] = prev_ref[0]

    ext_ref[CONV_HALO:n_ext, :] = u_ref[0]
    ext_ref[n_ext:n_ext + SUBLANES, :] = jnp.zeros((SUBLANES, c), F32)
    for r in range(SUBLANES):
        rot_ref[r] = ext_ref[pl.ds(r, n_ext), :]
    first = CONV_HALO - (CONV_A_WIDTH - 1)
    cb = cb_ref[...]
    lg = lg_ref[...]
    lb = lb_ref[...]
    for r0 in range(0, tt, CONV_CHUNK):
        acc = jnp.zeros((CONV_CHUNK, c), F32)
        for w in range(CONV_A_WIDTH):
            s = first + w
            acc = acc + rot_ref[s % SUBLANES, pl.ds(r0 + (s // SUBLANES) * SUBLANES, CONV_CHUNK), :] * cw_ref[w:w + 1, :]
        o_ref[0, r0:r0 + CONV_CHUNK, :] = _layer_norm_silu(acc + cb, lg, lb).astype(o_ref.dtype)


def _conv_a(u, cw, cb, lg, lb):
    b, t, c = u.shape
    tt = SEQ_TILE_CONV
    halo_blocks = tt // CONV_HALO
    vec = lambda a: a.reshape(1, c)
    return pl.pallas_call(
        functools.partial(_conv_a_body, tt=tt),
        grid=(b, t // tt),
        in_specs=[pl.BlockSpec((1, tt, c), lambda bi, i: (bi, i, 0)),
                  pl.BlockSpec((1, CONV_HALO, c), lambda bi, i: (bi, jnp.maximum(i * halo_blocks - 1, 0), 0)),
                  pl.BlockSpec((CONV_A_WIDTH, c), lambda bi, i: (0, 0)),
                  pl.BlockSpec((1, c), lambda bi, i: (0, 0)),
                  pl.BlockSpec((1, c), lambda bi, i: (0, 0)),
                  pl.BlockSpec((1, c), lambda bi, i: (0, 0))],
        out_specs=pl.BlockSpec((1, tt, c), lambda bi, i: (bi, i, 0)),
        out_shape=jax.ShapeDtypeStruct((b, t, c), BF16),
        scratch_shapes=[pltpu.VMEM((tt + CONV_HALO + SUBLANES, c), F32),
                        pltpu.VMEM((SUBLANES, tt + CONV_HALO, c), F32)],
        compiler_params=_params(("parallel", "arbitrary")),
        name="conv_a",
    )(u, u, cw, vec(cb), vec(lg), vec(lb))


def _lru_gates(xr, wg_ref, bg_ref, lam_ref):
    gates = _dot(xr.astype(BF16), wg_ref[...]) + bg_ref[...]
    r = jax.nn.sigmoid(gates[:, :LRU_CH])
    ig = jax.nn.sigmoid(gates[:, LRU_CH:])
    nl = -lam_ref[...]
    softplus = jnp.maximum(nl, 0.0) + jnp.log1p(jnp.exp(-jnp.abs(nl)))
    log_a = -LRU_C * r * softplus
    a = jnp.exp(log_a)
    beta = jnp.sqrt(jnp.maximum(-jnp.tanh(log_a) * (a * a + 1.0), 0.0))
    return a, beta * ig * xr


def _lru_body(br_ref, prev_ref, gb_ref, cw_ref, cb_ref, wg_ref, bg_ref, lam_ref,
              yb_ref, hl_ref, ext_ref, h_ref, *, tt):
    i = pl.program_id(1)
    c = LRU_CH

    @pl.when(i == 0)
    def _():
        h_ref[...] = jnp.zeros((1, c), F32)
        ext_ref[0:SUBLANES, :] = jnp.zeros((SUBLANES, c), F32)

    @pl.when(i != 0)
    def _():
        ext_ref[0:SUBLANES, :] = prev_ref[0]

    ext_ref[SUBLANES:SUBLANES + tt, :] = br_ref[0]
    first = SUBLANES - (CONV_B_WIDTH - 1)
    xr = cb_ref[...]
    for w in range(CONV_B_WIDTH):
        xr = xr + ext_ref[pl.ds(first + w, tt), :] * cw_ref[w:w + 1, :]
    a, u = _lru_gates(xr, wg_ref, bg_ref, lam_ref)

    row = lax.broadcasted_iota(jnp.int32, (tt, c), 0)
    d = 1
    while d < tt:
        if d < SUBLANES:
            keep = row >= d
            a_sh = jnp.where(keep, pltpu.roll(a, d, 0), 1.0)
            u_sh = jnp.where(keep, pltpu.roll(u, d, 0), 0.0)
        else:
            a_sh = jnp.concatenate([jnp.ones((d, c), F32), a[:tt - d]], axis=0)
            u_sh = jnp.concatenate([jnp.zeros((d, c), F32), u[:tt - d]], axis=0)
        u = a * u_sh + u
        a = a * a_sh
        d *= 2
    h = a * h_ref[...] + u
    yb_ref[0] = (h * gb_ref[0]).astype(yb_ref.dtype)
    h_last = h[tt - 1:tt, :]
    h_ref[...] = h_last
    hl_ref[0] = h_last


def _lru(br, gb, cw, cb, wg, bg, lam):
    b, t, c = br.shape
    tt = SEQ_TILE_LRU
    halo_blocks = tt // SUBLANES
    vec = lambda a: a.reshape(1, -1)
    return pl.pallas_call(
        functools.partial(_lru_body, tt=tt),
        grid=(b, t // tt),
        in_specs=[pl.BlockSpec((1, tt, c), lambda bi, i: (bi, i, 0)),
                  pl.BlockSpec((1, SUBLANES, c), lambda bi, i: (bi, jnp.maximum(i * halo_blocks - 1, 0), 0)),
                  pl.BlockSpec((1, tt, c), lambda bi, i: (bi, i, 0)),
                  pl.BlockSpec((CONV_B_WIDTH, c), lambda bi, i: (0, 0)),
                  pl.BlockSpec((1, c), lambda bi, i: (0, 0)),
                  pl.BlockSpec((c, 2 * c), lambda bi, i: (0, 0)),
                  pl.BlockSpec((1, 2 * c), lambda bi, i: (0, 0)),
                  pl.BlockSpec((1, c), lambda bi, i: (0, 0))],
        out_specs=[pl.BlockSpec((1, tt, c), lambda bi, i: (bi, i, 0)),
                   pl.BlockSpec((1, 1, c), lambda bi, i: (bi, 0, 0))],
        out_shape=[jax.ShapeDtypeStruct((b, t, c), BF16),
                   jax.ShapeDtypeStruct((b, 1, c), F32)],
        scratch_shapes=[pltpu.VMEM((tt + SUBLANES, c), F32), pltpu.VMEM((1, c), F32)],
        compiler_params=_params(("parallel", "arbitrary")),
        name="lru",
    )(br, br, gb, cw, vec(cb), wg, vec(bg), vec(lam))


def _sample_mix_body(u_ref, br_ref, gb_ref, ha_ref, hb_ref, h0_ref,
                     cwa_ref, cba_ref, lg_ref, lb_ref, cwb_ref, cbb_ref, wg_ref, bg_ref, lam_ref,
                     ya_ref, yb_ref, h_ref):
    u = u_ref[...]
    acc = cba_ref[...] + u * cwa_ref[CONV_A_WIDTH - 1:CONV_A_WIDTH, :]
    for w in range(CONV_A_WIDTH - 1):
        acc = acc + ha_ref[w] * cwa_ref[w:w + 1, :]
    ya_ref[...] = _layer_norm_silu(acc, lg_ref[...], lb_ref[...]).astype(ya_ref.dtype)

    br = br_ref[...]
    xr = cbb_ref[...] + br * cwb_ref[CONV_B_WIDTH - 1:CONV_B_WIDTH, :]
    for w in range(CONV_B_WIDTH - 1):
        xr = xr + hb_ref[w] * cwb_ref[w:w + 1, :]
    a, x_in = _lru_gates(xr, wg_ref, bg_ref, lam_ref)
    h = a * h0_ref[...] + x_in
    h_ref[...] = h
    yb_ref[...] = (h * gb_ref[...]).astype(yb_ref.dtype)


def _sample_mix(u, br, gb, hist_a_t, hist_b_t, h0, cwa, cba, lg, lb, cwb, cbb, wg, bg, lam):
    n, c = u.shape
    vec = lambda a: a.reshape(1, -1)
    args = [u, br, gb, hist_a_t, hist_b_t, h0, cwa, vec(cba), vec(lg), vec(lb), cwb, vec(cbb), wg, vec(bg), vec(lam)]
    full = lambda a: pl.BlockSpec(a.shape, lambda i, nd=a.ndim: (0,) * nd)
    return pl.pallas_call(
        _sample_mix_body,
        grid=(1,),
        in_specs=[full(a) for a in args],
        out_specs=[pl.BlockSpec((n, c), lambda i: (0, 0)) for _ in range(3)],
        out_shape=[jax.ShapeDtypeStruct((n, c), BF16), jax.ShapeDtypeStruct((n, c), BF16),
                   jax.ShapeDtypeStruct((n, c), F32)],
        compiler_params=_params(("arbitrary",)),
        name="sample_mix",
    )(*args)


def _diff_lambda(lq1_ref, lk1_ref, lq2_ref, lk2_ref, lam_init):
    e1 = jnp.exp(jnp.sum(lq1_ref[...] * lk1_ref[...], axis=-1, keepdims=True))
    e2 = jnp.exp(jnp.sum(lq2_ref[...] * lk2_ref[...], axis=-1, keepdims=True))
    return e1 - e2 + lam_init


Q_SCALE = ATT_HD ** -0.5 * math.log2(math.e)


def _qkv_body(x_ref, g_ref, wq_ref, wk_ref, wv_ref, qt_ref, k_ref, v_ref, kb_ref, vt_ref):
    xn = _rms(x_ref[...], g_ref[...]).astype(BF16)
    q = _dot(xn, wq_ref[...]) * Q_SCALE
    qt_ref[0] = q.T.astype(BF16)
    k = _dot(xn, wk_ref[...])
    _store_heads(k_ref, k, ATT_HEADS)
    kb_ref[...] = k.astype(BF16)
    v = _dot(xn, wv_ref[...])
    _store_heads(v_ref, v, ATT_HEADS)
    vt_ref[0] = v.T.astype(BF16)


def _qkv(x, g, w, bsz, seq):
    m, d = x.shape
    tm = ROW_TILE
    per_seq = seq // tm
    row = pl.BlockSpec((tm, d), lambda i: (i, 0))
    heads = pl.BlockSpec((tm, ATT_HEADS, ATT_VD), lambda i: (i, 0, 0))
    col = pl.BlockSpec((1, d, tm), lambda i: (i // per_seq, 0, i % per_seq))
    wspec = lambda b: pl.BlockSpec((d, d), lambda i: (0, b))
    return pl.pallas_call(
        _qkv_body,
        grid=(m // tm,),
        in_specs=[row, pl.BlockSpec((1, d), lambda i: (0, 0)), wspec(0), wspec(1), wspec(2)],
        out_specs=[col, heads, heads, row, col],
        out_shape=[jax.ShapeDtypeStruct((bsz, d, seq), BF16),
                   jax.ShapeDtypeStruct((m, ATT_HEADS, ATT_VD), F32),
                   jax.ShapeDtypeStruct((m, ATT_HEADS, ATT_VD), F32),
                   jax.ShapeDtypeStruct((m, d), BF16),
                   jax.ShapeDtypeStruct((bsz, d, seq), BF16)],
        compiler_params=_params(("parallel",)),
        name="qkv",
    )(x, g.reshape(1, d), w, w, w)


def _flash_body(qt_ref, k_ref, vt_ref, lq1_ref, lk1_ref, lq2_ref, lk2_ref, g_ref, o_ref,
                qs_ref, m_ref, l_ref, acc_ref, *, tq, lam_init):
    qi = pl.program_id(1)
    ki = pl.program_id(2)
    vd = ATT_VD

    @pl.when(ki == 0)
    def _():
        dim = lax.broadcasted_iota(jnp.int32, (vd, tq), 0)
        for h in range(ATT_HEADS):
            qh = qt_ref[0, h * vd:(h + 1) * vd, :]
            zero = jnp.zeros_like(qh)
            qs_ref[h, :, 0:tq] = jnp.where(dim < ATT_HD, qh, zero)
            qs_ref[h, :, tq:2 * tq] = jnp.where(dim >= ATT_HD, qh, zero)
        m_ref[...] = jnp.full(m_ref.shape, NEG_INF, F32)
        l_ref[...] = jnp.zeros(l_ref.shape, F32)
        acc_ref[...] = jnp.zeros(acc_ref.shape, F32)

    def step(diagonal):
        if diagonal:
            key = lax.broadcasted_iota(jnp.int32, (tq, 2 * tq), 0)
            qry = lax.broadcasted_iota(jnp.int32, (tq, 2 * tq), 1)
            visible = key <= jnp.where(qry >= tq, qry - tq, qry)
        for h in range(ATT_HEADS):
            kh = k_ref[0, :, h * vd:(h + 1) * vd]
            s = _dot(kh, qs_ref[h])
            if diagonal:
                s = jnp.where(visible, s, NEG_INF)
            m_prev = m_ref[h:h + 1, :]
            m_new = jnp.maximum(m_prev, jnp.max(s, axis=0, keepdims=True))
            alpha = jnp.exp2(m_prev - m_new)
            p = jnp.exp2(s - m_new)
            l_ref[h:h + 1, :] = alpha * l_ref[h:h + 1, :] + jnp.sum(p, axis=0, keepdims=True)
            vth = vt_ref[0, h * vd:(h + 1) * vd, :]
            acc_ref[h] = alpha * acc_ref[h] + _dot(vth, p.astype(BF16))
            m_ref[h:h + 1, :] = m_new

    @pl.when(ki < qi)
    def _():
        step(False)

    @pl.when(ki == qi)
    def _():
        step(True)
        lam = _diff_lambda(lq1_ref, lk1_ref, lq2_ref, lk2_ref, lam_init)
        g = g_ref[...]
        for h in range(ATT_HEADS):
            o = acc_ref[h] / l_ref[h:h + 1, :]
            att = o[:, 0:tq] - lam * o[:, tq:2 * tq]
            ms = jnp.mean(att * att, axis=0, keepdims=True)
            att = att * lax.rsqrt(ms + EPS) * g * (1.0 - lam_init)
            o_ref[0, :, h * vd:(h + 1) * vd] = att.T.astype(o_ref.dtype)


def _flash(qt, k, vt, lam_p, g, lam_init):
    b, d, t = qt.shape
    tq = ATT_TQ
    nq = t // tq
    vec = lambda a: a.reshape(1, -1)
    small = lambda n: pl.BlockSpec((1, n), lambda bi, qi, ki: (0, 0))
    return pl.pallas_call(
        functools.partial(_flash_body, tq=tq, lam_init=lam_init),
        grid=(b, nq, nq),
        in_specs=[pl.BlockSpec((1, d, tq), lambda bi, qi, ki: (bi, 0, qi)),
                  pl.BlockSpec((1, tq, d), lambda bi, qi, ki: (bi, jnp.minimum(ki, qi), 0)),
                  pl.BlockSpec((1, d, tq), lambda bi, qi, ki: (bi, 0, jnp.minimum(ki, qi))),
                  small(ATT_HD), small(ATT_HD), small(ATT_HD), small(ATT_HD),
                  pl.BlockSpec((ATT_VD, 1), lambda bi, qi, ki: (0, 0))],
        out_specs=pl.BlockSpec((1, tq, d), lambda bi, qi, ki: (bi, qi, 0)),
        out_shape=jax.ShapeDtypeStruct((b, t, d), BF16),
        scratch_shapes=[pltpu.VMEM((ATT_HEADS, ATT_VD, 2 * tq), BF16),
                        pltpu.VMEM((ATT_HEADS, 2 * tq), F32),
                        pltpu.VMEM((ATT_HEADS, 2 * tq), F32),
                        pltpu.VMEM((ATT_HEADS, ATT_VD, 2 * tq), F32)],
        compiler_params=_params(("parallel", "parallel", "arbitrary")),
        name="flash_diff_attn",
    )(qt, k, vt, *[vec(a) for a in lam_p], g.reshape(ATT_VD, 1))


def _decode_body(pt_ref, q_ref, kn_ref, vn_ref, lq1_ref, lk1_ref, lq2_ref, lk2_ref, g_ref, *rest,
                 n_pages, lam_init):
    k_refs = rest[:n_pages]
    v_refs = rest[n_pages:2 * n_pages]
    o_ref, qx_ref, m_ref, l_ref, acc_ref = rest[2 * n_pages:]
    j = pl.program_id(1)
    nh = ATT_HEADS
    rows = PAGE_SIZE * nh

    @pl.when(j == 0)
    def _():
        q8 = q_ref[0].astype(F32)
        lane = lax.broadcasted_iota(jnp.int32, (nh, ATT_VD), 1)
        qx = jnp.concatenate([jnp.where(lane < ATT_HD, q8, 0.0), jnp.where(lane >= ATT_HD, q8, 0.0)], axis=0)
        qx_ref[...] = qx.astype(BF16)
        kn = kn_ref[0].astype(BF16).astype(F32)
        vn = vn_ref[0].astype(BF16).astype(F32)
        m_ref[...] = jnp.sum(qx * jnp.concatenate([kn, kn], axis=0), axis=-1, keepdims=True)
        l_ref[...] = jnp.ones(l_ref.shape, F32)
        acc_ref[...] = jnp.concatenate([vn, vn], axis=0)

    qx = qx_ref[...]
    s = jnp.concatenate([_dot_nt(qx, k_ref[0].reshape(rows, ATT_VD).astype(BF16)) for k_ref in k_refs], axis=1)
    r = lax.broadcasted_iota(jnp.int32, s.shape, 0)
    c = lax.broadcasted_iota(jnp.int32, s.shape, 1)
    s = jnp.where(c % nh == r % nh, s, NEG_INF)
    m_prev = m_ref[...]
    m_new = jnp.maximum(m_prev, jnp.max(s, axis=-1, keepdims=True))
    alpha = jnp.exp2(m_prev - m_new)
    p = jnp.exp2(s - m_new)
    l_ref[...] = alpha * l_ref[...] + jnp.sum(p, axis=-1, keepdims=True)
    pb = p.astype(BF16)
    acc = alpha * acc_ref[...]
    for i, v_ref in enumerate(v_refs):
        acc = acc + _dot(pb[:, i * rows:(i + 1) * rows], v_ref[0].reshape(rows, ATT_VD).astype(BF16))
    acc_ref[...] = acc
    m_ref[...] = m_new

    @pl.when(j == pl.num_programs(1) - 1)
    def _():
        lam = _diff_lambda(lq1_ref, lk1_ref, lq2_ref, lk2_ref, lam_init)
        o = acc_ref[...] / l_ref[...]
        att = o[0:nh] - lam * o[nh:2 * nh]
        o_ref[0] = (_rms(att, g_ref[...]) * (1.0 - lam_init)).astype(o_ref.dtype)


def _decode(q, k_new, v_new, cache_k, cache_v, page_base, page_table, lam_p, g, lam_init):
    b = q.shape[0]
    nh, vd = ATT_HEADS, ATT_VD
    n_pages_total = page_table.shape[1]
    pp = DEC_PAGES
    vec = lambda a: a.reshape(1, -1)
    row_spec = pl.BlockSpec((1, nh, vd), lambda bi, j, pt: (bi, 0, 0))
    small = lambda n: pl.BlockSpec((1, n), lambda bi, j, pt: (0, 0))
    page_specs = [pl.BlockSpec((1, PAGE_SIZE, nh, vd),
                               lambda bi, j, pt, i=i: (page_base + pt[bi, j * pp + i], 0, 0, 0))
                  for i in range(pp)]
    grid_spec = pltpu.PrefetchScalarGridSpec(
        num_scalar_prefetch=1,
        grid=(b, n_pages_total // pp),
        in_specs=[row_spec, row_spec, row_spec,
                  small(ATT_HD), small(ATT_HD), small(ATT_HD), small(ATT_HD), small(vd)]
                 + page_specs + page_specs,
        out_specs=pl.BlockSpec((1, nh, vd), lambda bi, j, pt: (bi, 0, 0)),
        scratch_shapes=[pltpu.VMEM((2 * nh, vd), BF16),
                        pltpu.VMEM((2 * nh, 1), F32),
                        pltpu.VMEM((2 * nh, 1), F32),
                        pltpu.VMEM((2 * nh, vd), F32)],
    )
    return pl.pallas_call(
        functools.partial(_decode_body, n_pages=pp, lam_init=lam_init),
        grid_spec=grid_spec,
        out_shape=jax.ShapeDtypeStruct((b, nh, vd), BF16),
        compiler_params=_params(("parallel", "arbitrary")),
        name="paged_diff_attn",
    )(page_table, q, k_new, v_new, *[vec(a) for a in lam_p], vec(g),
      *([cache_k] * pp), *([cache_v] * pp))


def _xattn_head(q, k, v):
    s = _dot_nt(q, k)
    p = jnp.exp(s - jnp.max(s, axis=-1, keepdims=True))
    p = p / jnp.sum(p, axis=-1, keepdims=True)
    return _dot(p.astype(BF16), v)


def _xattn_body(q_ref, k_ref, v_ref, o_ref):
    hd = XATT_HD
    for h in range(XATT_HEADS):
        cols = slice(h * hd, (h + 1) * hd)
        o_ref[0, :, cols] = _xattn_head(q_ref[0, :, cols], k_ref[0, :, cols], v_ref[0, :, cols]).astype(o_ref.dtype)


def _xattn(q, mem_k, mem_v):
    b, t, d = q.shape
    tq = ROW_TILE
    kv_spec = pl.BlockSpec((1, N_MEM, d), lambda bi, i: (bi, 0, 0))
    return pl.pallas_call(
        _xattn_body,
        grid=(b, t // tq),
        in_specs=[pl.BlockSpec((1, tq, d), lambda bi, i: (bi, i, 0)), kv_spec, kv_spec],
        out_specs=pl.BlockSpec((1, tq, d), lambda bi, i: (bi, i, 0)),
        out_shape=jax.ShapeDtypeStruct((b, t, d), BF16),
        compiler_params=_params(("parallel", "parallel")),
        name="xattn",
    )(q, mem_k, mem_v)


def _xattn_cached_body(q_ref, k_ref, v_ref, o_ref):
    hd = XATT_HD
    for i in range(XS_BATCH):
        for h in range(XATT_HEADS):
            cols = slice(h * hd, (h + 1) * hd)
            k = k_ref[0, i, :, h, :].astype(BF16)
            v = v_ref[0, i, :, h, :].astype(BF16)
            o_ref[i, :, cols] = _xattn_head(q_ref[i, :, cols], k, v).astype(o_ref.dtype)


def _xattn_cached(q, cache_k, cache_v, layer):
    b, t, d = q.shape
    bb = XS_BATCH
    kv_spec = pl.BlockSpec((1, bb, N_MEM, XATT_HEADS, XATT_HD), lambda i: (layer, i, 0, 0, 0))
    return pl.pallas_call(
        _xattn_cached_body,
        grid=(b // bb,),
        in_specs=[pl.BlockSpec((bb, t, d), lambda i: (i, 0, 0)), kv_spec, kv_spec],
        out_specs=pl.BlockSpec((bb, t, d), lambda i: (i, 0, 0)),
        out_shape=jax.ShapeDtypeStruct((b, t, d), BF16),
        compiler_params=_params(("parallel",)),
        name="xattn_cached",
    )(q, cache_k, cache_v)


def _memkv_body(x_ref, wk_ref, wv_ref, kn_ref, vn_ref, kb_ref, vb_ref):
    xb = x_ref[...].astype(BF16)
    for w_ref, n_ref, b_ref in ((wk_ref, kn_ref, kb_ref), (wv_ref, vn_ref, vb_ref)):
        y = _dot(xb, w_ref[...])
        b_ref[...] = y.astype(BF16)
        _store_heads(n_ref, y, XATT_HEADS)


def _memkv(x, w):
    m, d = x.shape
    tm = _row_tile(m)
    row = pl.BlockSpec((tm, d), lambda i: (i, 0))
    heads = pl.BlockSpec((tm, XATT_HEADS, XATT_HD), lambda i: (i, 0, 0))
    return pl.pallas_call(
        _memkv_body,
        grid=(m // tm,),
        in_specs=[row, pl.BlockSpec((d, d), lambda i: (0, 0)), pl.BlockSpec((d, d), lambda i: (0, 1))],
        out_specs=[heads, heads, row, row],
        out_shape=[jax.ShapeDtypeStruct((m, XATT_HEADS, XATT_HD), F32),
                   jax.ShapeDtypeStruct((m, XATT_HEADS, XATT_HD), F32),
                   jax.ShapeDtypeStruct((m, d), BF16),
                   jax.ShapeDtypeStruct((m, d), BF16)],
        compiler_params=_params(("parallel",)),
        name="memkv",
    )(x, w, w)


def _block_diag(w):
    nb, n, _ = w.shape
    eye = jnp.eye(nb, dtype=w.dtype)
    return jnp.einsum('kij,kl->kilj', w, eye).reshape(nb * n, nb * n)


def _stack(arrays):
    return arrays[0][None] if len(arrays) == 1 else jnp.stack(arrays)


def _trunk(x3, xattn, p, even_mix, odd_mix):
    b, t, d = x3.shape
    x = x3.reshape(b * t, d)
    depth = p['ffn1_g'].shape[0]
    for l in range(depth):
        x = _ffn(x, p['ffn1_g'][l], p['ffn1_w_in'][l], p['ffn1_w_out'][l])
        if l % 2 == 0:
            x = even_mix(x, l // 2)
        else:
            x = odd_mix(x, l // 2, l)
        (q,) = _proj(x, p['xattn_g'][l], p['xattn_w_q'][l], d, [0], [(0, BF16, XATT_HD ** -0.5)])
        o = xattn(q.reshape(b, t, d), l)
        x = _outproj(x, [o.reshape(b * t, d)], [p['xattn_w_out'][l]])
        last = l == depth - 1
        x = _ffn(x, p['ffn2_g'][l], p['ffn2_w_in'][l], p['ffn2_w_out'][l],
                 final_g=p['final_g'] if last else None)
    return x.reshape(b, t, d)


def kernel(x_prompt, x_sample, state_conv_a, state_conv_b, state_lru, cache_k, cache_v, cache_mem_k, cache_mem_v, page_table, mem_prompt, ffn1_g, ffn1_w_in, ffn1_w_out, mix_g, even_w_in, conv_a_w, conv_a_b, conv_a_ln_g, conv_a_ln_b, conv_b_w, conv_b_b, lru_w_a, lru_b_a, lru_w_x, lru_b_x, lru_lambda, even_w_out, attn_w_in, lam_q1, lam_k1, lam_q2, lam_k2, attn_subln_g, attn_w_out, xattn_g, xattn_w_q, xattn_w_kv, xattn_w_out, ffn2_g, ffn2_w_in, ffn2_w_out, final_g):
    bsz, seq, d = x_prompt.shape
    dbsz, dseq, _ = x_sample.shape
    depth = ffn1_g.shape[0]
    n_even = even_w_in.shape[0]
    n_odd = attn_w_in.shape[0]
    bf = lambda w: w.astype(BF16)
    p = {
        'ffn1_g': ffn1_g, 'ffn1_w_in': bf(ffn1_w_in), 'ffn1_w_out': bf(ffn1_w_out),
        'ffn2_g': ffn2_g, 'ffn2_w_in': bf(ffn2_w_in), 'ffn2_w_out': bf(ffn2_w_out),
        'xattn_g': xattn_g, 'xattn_w_q': bf(xattn_w_q), 'xattn_w_out': bf(xattn_w_out),
        'final_g': final_g,
    }
    even_w_in_b = bf(even_w_in)
    even_w_out_b = bf(even_w_out)
    attn_w_in_b = bf(attn_w_in)
    attn_w_out_b = bf(attn_w_out)
    xattn_w_kv_b = bf(xattn_w_kv)
    lru_wg = [bf(jnp.concatenate([_block_diag(lru_w_a[e]), _block_diag(lru_w_x[e])], axis=1)) for e in range(n_even)]
    lru_bg = [jnp.concatenate([lru_b_a[e], lru_b_x[e]]) for e in range(n_even)]
    c = CONV_CH
    lam_inits = [0.8 - 0.6 * math.exp(-0.3 * (2 * o + 1)) for o in range(n_odd)]

    mem2 = mem_prompt.reshape(bsz * N_MEM, d)
    p_mem_k, p_mem_v, p_mem_kb, p_mem_vb = [], [], [], []
    for l in range(depth):
        mk, mv, mkb, mvb = _memkv(mem2, xattn_w_kv_b[l])
        p_mem_k.append(mk.reshape(bsz, N_MEM, XATT_HEADS, XATT_HD))
        p_mem_v.append(mv.reshape(bsz, N_MEM, XATT_HEADS, XATT_HD))
        p_mem_kb.append(mkb.reshape(bsz, N_MEM, d))
        p_mem_vb.append(mvb.reshape(bsz, N_MEM, d))

    p_conv_a, p_conv_b, p_lru, p_k, p_v = [], [], [], [], []

    def p_even(x, e):
        u, br, gb = _even_in(x, mix_g[2 * e], even_w_in_b[e])
        u3, br3, gb3 = (a.reshape(bsz, seq, c) for a in (u, br, gb))
        ya = _conv_a(u3, conv_a_w[e], conv_a_b[e], conv_a_ln_g[e], conv_a_ln_b[e])
        yb, h_last = _lru(br3, gb3, conv_b_w[e], conv_b_b[e], lru_wg[e], lru_bg[e], lru_lambda[e])
        p_conv_a.append(u3[:, seq - (CONV_A_WIDTH - 1):])
        p_conv_b.append(br3[:, seq - (CONV_B_WIDTH - 1):])
        p_lru.append(h_last.reshape(bsz, c))
        return _outproj(x, [ya.reshape(-1, c), yb.reshape(-1, c)], [even_w_out_b[e][:c], even_w_out_b[e][c:]])

    def p_odd(x, o, l):
        qt, k, v, kb, vt = _qkv(x, mix_g[l], attn_w_in_b[o], bsz, seq)
        p_k.append(k.reshape(bsz, seq, ATT_HEADS, ATT_VD))
        p_v.append(v.reshape(bsz, seq, ATT_HEADS, ATT_VD))
        att = _flash(qt, kb.reshape(bsz, seq, d), vt,
                     (lam_q1[o], lam_k1[o], lam_q2[o], lam_k2[o]), attn_subln_g[o], lam_inits[o])
        return _outproj(x, [att.reshape(-1, d)], [attn_w_out_b[o]])

    y_prompt = _trunk(x_prompt, lambda q, l: _xattn(q, p_mem_kb[l], p_mem_vb[l]), p, p_even, p_odd)

    s_conv_a, s_conv_b, s_lru, s_k, s_v = [], [], [], [], []
    pool = cache_k.shape[1]
    pages_k = cache_k.reshape(n_odd * pool, PAGE_SIZE, ATT_HEADS, ATT_VD)
    pages_v = cache_v.reshape(n_odd * pool, PAGE_SIZE, ATT_HEADS, ATT_VD)

    def s_even(x, e):
        u, br, gb = _even_in(x, mix_g[2 * e], even_w_in_b[e])
        ya, yb, h_new = _sample_mix(
            u, br, gb, jnp.swapaxes(state_conv_a[e], 0, 1), jnp.swapaxes(state_conv_b[e], 0, 1), state_lru[e],
            conv_a_w[e], conv_a_b[e], conv_a_ln_g[e], conv_a_ln_b[e],
            conv_b_w[e], conv_b_b[e], lru_wg[e], lru_bg[e], lru_lambda[e])
        s_conv_a.append(jnp.concatenate([state_conv_a[e][:, 1:], u[:, None, :]], axis=1))
        s_conv_b.append(jnp.concatenate([state_conv_b[e][:, 1:], br[:, None, :]], axis=1))
        s_lru.append(h_new)
        return _outproj(x, [ya, yb], [even_w_out_b[e][:c], even_w_out_b[e][c:]])

    def s_odd(x, o, l):
        q, k, v = _proj(x, mix_g[l], attn_w_in_b[o], d, [0, 1, 2],
                        [(0, BF16, Q_SCALE), (1, F32, 1.0), (2, F32, 1.0)])
        heads = (dbsz, ATT_HEADS, ATT_VD)
        s_k.append(k.reshape(dbsz, dseq, ATT_HEADS, ATT_VD))
        s_v.append(v.reshape(dbsz, dseq, ATT_HEADS, ATT_VD))
        att = _decode(q.reshape(heads), k.reshape(heads), v.reshape(heads), pages_k, pages_v, o * pool,
                      page_table, (lam_q1[o], lam_k1[o], lam_q2[o], lam_k2[o]), attn_subln_g[o], lam_inits[o])
        return _outproj(x, [att.reshape(dbsz, d)], [attn_w_out_b[o]])

    y_sample = _trunk(x_sample, lambda q, l: _xattn_cached(q, cache_mem_k, cache_mem_v, l), p, s_even, s_odd)

    return (y_prompt, y_sample,
            _stack(p_conv_a), _stack(p_conv_b), _stack(p_lru),
            _stack(p_k), _stack(p_v),
            _stack(p_mem_k), _stack(p_mem_v),
            _stack(s_conv_a), _stack(s_conv_b), _stack(s_lru),
            _stack(s_k), _stack(s_v))
```

```python
import functools
import math

import jax
import jax.numpy as jnp
from jax import lax
from jax.experimental import pallas as pl
from jax.experimental.pallas import tpu as pltpu

F32 = jnp.float32
BF16 = jnp.bfloat16

D_MODEL = 1024
D_FF = 2816
CONV_CH = 512
CONV_A_WIDTH = 31
LRU_CH = 512
LRU_BLOCKS = 8
CONV_B_WIDTH = 4
LRU_C = 8.0
ATT_HEADS = 8
ATT_HD = 64
ATT_VD = 128
N_MEM = 256
XATT_HEADS = 4
XATT_HD = 256
PAGE_SIZE = 128
EPS = 1e-6
NEG_INF = -1e30

SUBLANES = 8
VMEM_LIMIT = 56 * 1024 * 1024

MXU_COLS = 256
FFN_CHUNK = 4 * MXU_COLS
XS_BATCH = 8
ROW_TILE = 512
SEQ_TILE_CONV = 512
SEQ_TILE_LRU = 256
ATT_TQ = 512
DEC_PAGES = 8


def _params(sem):
    return pltpu.CompilerParams(dimension_semantics=sem, vmem_limit_bytes=VMEM_LIMIT)


def _rms(x, g):
    return x * lax.rsqrt(jnp.mean(x * x, axis=-1, keepdims=True) + EPS) * g


def _dot(a, b):
    return jnp.dot(a, b, preferred_element_type=F32)


def _dot_nt(a, b):
    return lax.dot_general(a, b, (((1,), (1,)), ((), ())), preferred_element_type=F32)


def _row_tile(m):
    return ROW_TILE if m % ROW_TILE == 0 else m


def _store_heads(o_ref, y, n_heads):
    hd = y.shape[1] // n_heads
    for h in range(n_heads):
        o_ref[:, h, :] = y[:, h * hd:(h + 1) * hd]


def _ffn_chunks(f):
    return [(c0, min(FFN_CHUNK, f - c0)) for c0 in range(0, f, FFN_CHUNK)]


def _ffn_body(*refs, final, pre):
    refs = list(refs)
    x_ref, g_ref, wi_ref, wo_ref = refs[:4]
    pos = 4
    if pre:
        y_ref, wp_ref = refs[pos:pos + 2]
        pos += 2
    if final:
        gf_ref = refs[pos]
        pos += 1
    o_ref = refs[pos]
    f = wo_ref.shape[0]
    x = x_ref[...]
    if pre:
        x = x + _dot(y_ref[...], wp_ref[...])
    xn = _rms(x, g_ref[...]).astype(BF16)
    acc = None
    for c0, cw in _ffn_chunks(f):
        gate = _dot(xn, wi_ref[:, c0:c0 + cw])
        up = _dot(xn, wi_ref[:, f + c0:f + c0 + cw])
        h = (gate * jax.nn.sigmoid(gate) * up).astype(BF16)
        part = _dot(h, wo_ref[c0:c0 + cw, :])
        acc = part if acc is None else acc + part
    y = x + 0.5 * acc
    if final:
        y = _rms(y, gf_ref[...])
    o_ref[...] = y


def _ffn(x, g, w_in, w_out, final_g=None, pre=None):
    m, d = x.shape
    f = w_out.shape[0]
    tm = _row_tile(m)
    final = final_g is not None
    const = lambda shape: pl.BlockSpec(shape, lambda i: (0, 0), pipeline_mode=pl.Buffered(1))
    in_specs = [pl.BlockSpec((tm, d), lambda i: (i, 0)), const((1, d)), const((d, 2 * f)), const((f, d))]
    args = [x, g.reshape(1, d), w_in, w_out]
    if pre is not None:
        y, wp = pre
        in_specs += [pl.BlockSpec((tm, y.shape[1]), lambda i: (i, 0)), const(wp.shape)]
        args += [y, wp]
    if final:
        in_specs.append(const((1, d)))
        args.append(final_g.reshape(1, d))
    return pl.pallas_call(
        functools.partial(_ffn_body, final=final, pre=pre is not None),
        grid=(m // tm,),
        in_specs=in_specs,
        out_specs=pl.BlockSpec((tm, d), lambda i: (i, 0)),
        out_shape=jax.ShapeDtypeStruct((m, d), F32),
        compiler_params=_params(("parallel",)),
        name="ffn",
    )(*args)


def _proj_body(*refs, has_g, n_w, out_meta):
    x_ref = refs[0]
    pos = 1
    if has_g:
        g_ref = refs[pos]
        pos += 1
    w_refs = refs[pos:pos + n_w]
    o_refs = refs[pos + n_w:]
    x = x_ref[...]
    if has_g:
        x = _rms(x, g_ref[...])
    xb = x.astype(BF16)
    ys = [_dot(xb, w_ref[...]) for w_ref in w_refs]
    for o_ref, (wi, scale) in zip(o_refs, out_meta):
        y = ys[wi]
        if scale != 1.0:
            y = y * scale
        o_ref[...] = y.astype(o_ref.dtype)


def _proj(x, g, w, width, blocks, outs):
    m, k = x.shape
    tm = _row_tile(m)
    has_g = g is not None
    in_specs = [pl.BlockSpec((tm, k), lambda i: (i, 0))]
    args = [x]
    if has_g:
        in_specs.append(pl.BlockSpec((1, k), lambda i: (0, 0)))
        args.append(g.reshape(1, k))
    for b in blocks:
        in_specs.append(pl.BlockSpec((k, width), lambda i, b=b: (0, b)))
        args.append(w)
    return pl.pallas_call(
        functools.partial(_proj_body, has_g=has_g, n_w=len(blocks),
                          out_meta=tuple((wi, sc) for wi, _, sc in outs)),
        grid=(m // tm,),
        in_specs=in_specs,
        out_specs=[pl.BlockSpec((tm, width), lambda i: (i, 0)) for _ in outs],
        out_shape=[jax.ShapeDtypeStruct((m, width), dt) for _, dt, _ in outs],
        compiler_params=_params(("parallel",)),
        name="proj",
    )(*args)


def _outproj_body(*refs, n, post_scale):
    x_ref = refs[0]
    y_refs = refs[1:1 + n]
    w_refs = refs[1 + n:1 + 2 * n]
    pos = 1 + 2 * n
    if post_scale is not None:
        g_ref, wq_ref = refs[pos:pos + 2]
        pos += 2
    o_ref = refs[pos]
    acc = x_ref[...]
    for y_ref, w_ref in zip(y_refs, w_refs):
        acc = acc + _dot(y_ref[...], w_ref[...])
    o_ref[...] = acc
    if post_scale is not None:
        q_ref = refs[pos + 1]
        q = _dot(_rms(acc, g_ref[...]).astype(BF16), wq_ref[...]) * post_scale
        q_ref[...] = q.astype(q_ref.dtype)


def _outproj(x, ys, ws, post=None):
    m, d = x.shape
    tm = _row_tile(m)
    n = len(ys)
    row = pl.BlockSpec((tm, d), lambda i: (i, 0))
    in_specs = [row]
    in_specs += [pl.BlockSpec((tm, y.shape[1]), lambda i: (i, 0)) for y in ys]
    in_specs += [pl.BlockSpec(w.shape, lambda i: (0, 0)) for w in ws]
    args = [x, *ys, *ws]
    out_specs = row
    out_shape = jax.ShapeDtypeStruct((m, d), F32)
    post_scale = None
    if post is not None:
        g, wq, post_scale = post
        in_specs += [pl.BlockSpec((1, d), lambda i: (0, 0)), pl.BlockSpec(wq.shape, lambda i: (0, 0))]
        args += [g.reshape(1, d), wq]
        out_specs = [row, pl.BlockSpec((tm, wq.shape[1]), lambda i: (i, 0))]
        out_shape = [out_shape, jax.ShapeDtypeStruct((m, wq.shape[1]), BF16)]
    return pl.pallas_call(
        functools.partial(_outproj_body, n=n, post_scale=post_scale),
        grid=(m // tm,),
        in_specs=in_specs,
        out_specs=out_specs,
        out_shape=out_shape,
        compiler_params=_params(("parallel",)),
        name="outproj",
    )(*args)


def _even_in_body(x_ref, g_ref, w_ref, u_ref, br_ref, gb_ref):
    xn = _rms(x_ref[...], g_ref[...]).astype(BF16)
    z = _dot(xn, w_ref[...])
    c = CONV_CH
    u_ref[...] = z[:, :c] * jax.nn.sigmoid(z[:, c:2 * c])
    br_ref[...] = z[:, 2 * c:2 * c + LRU_CH]
    gb_ref[...] = jax.nn.gelu(z[:, 2 * c + LRU_CH:])


def _even_in(x, g, w):
    m, d = x.shape
    tm = _row_tile(m)
    n = w.shape[1]
    return pl.pallas_call(
        _even_in_body,
        grid=(m // tm,),
        in_specs=[pl.BlockSpec((tm, d), lambda i: (i, 0)),
                  pl.BlockSpec((1, d), lambda i: (0, 0)),
                  pl.BlockSpec((d, n), lambda i: (0, 0))],
        out_specs=[pl.BlockSpec((tm, CONV_CH), lambda i: (i, 0)) for _ in range(3)],
        out_shape=[jax.ShapeDtypeStruct((m, CONV_CH), F32) for _ in range(3)],
        compiler_params=_params(("parallel",)),
        name="even_in",
    )(x, g.reshape(1, d), w)


def _layer_norm_silu(y, g, b):
    mu = jnp.mean(y, axis=-1, keepdims=True)
    yc = y - mu
    var = jnp.mean(yc * yc, axis=-1, keepdims=True)
    yn = yc * lax.rsqrt(var + EPS) * g + b
    return yn * jax.nn.sigmoid(yn)


CONV_HALO = 32
CONV_CHUNK = 32


def _conv_a_body(u_ref, prev_ref, cw_ref, cb_ref, lg_ref, lb_ref, o_ref, ext_ref, rot_ref, *, tt):
    i = pl.program_id(1)
    c = CONV_CH
    n_ext = tt + CONV_HALO

    @pl.when(i == 0)
    def _():
        ext_ref[0:CONV_HALO, :] = jnp.zeros((CONV_HALO, c), F32)

    @pl.when(i != 0)
    def _():
        ext_ref[0:CONV_HALO, :] = prev_ref[0]

    ext_ref[CONV_HALO:n_ext, :] = u_ref[0]
    ext_ref[n_ext:n_ext + SUBLANES, :] = jnp.zeros((SUBLANES, c), F32)
    for r in range(SUBLANES):
        rot_ref[r] = ext_ref[pl.ds(r, n_ext), :]
    first = CONV_HALO - (CONV_A_WIDTH - 1)
    cb = cb_ref[...]
    lg = lg_ref[...]
    lb = lb_ref[...]
    for r0 in range(0, tt, CONV_CHUNK):
        acc = jnp.zeros((CONV_CHUNK, c), F32)
        for w in range(CONV_A_WIDTH):
            s = first + w
            acc = acc + rot_ref[s % SUBLANES, pl.ds(r0 + (s // SUBLANES) * SUBLANES, CONV_CHUNK), :] * cw_ref[w:w + 1, :]
        o_ref[0, r0:r0 + CONV_CHUNK, :] = _layer_norm_silu(acc + cb, lg, lb).astype(o_ref.dtype)


def _conv_a(u, cw, cb, lg, lb):
    b, t, c = u.shape
    tt = SEQ_TILE_CONV
    halo_blocks = tt // CONV_HALO
    vec = lambda a: a.reshape(1, c)
    return pl.pallas_call(
        functools.partial(_conv_a_body, tt=tt),
        grid=(b, t // tt),
        in_specs=[pl.BlockSpec((1, tt, c), lambda bi, i: (bi, i, 0)),
                  pl.BlockSpec((1, CONV_HALO, c), lambda bi, i: (bi, jnp.maximum(i * halo_blocks - 1, 0), 0)),
                  pl.BlockSpec((CONV_A_WIDTH, c), lambda bi, i: (0, 0)),
                  pl.BlockSpec((1, c), lambda bi, i: (0, 0)),
                  pl.BlockSpec((1, c), lambda bi, i: (0, 0)),
                  pl.BlockSpec((1, c), lambda bi, i: (0, 0))],
        out_specs=pl.BlockSpec((1, tt, c), lambda bi, i: (bi, i, 0)),
        out_shape=jax.ShapeDtypeStruct((b, t, c), BF16),
        scratch_shapes=[pltpu.VMEM((tt + CONV_HALO + SUBLANES, c), F32),
                        pltpu.VMEM((SUBLANES, tt + CONV_HALO, c), F32)],
        compiler_params=_params(("parallel", "arbitrary")),
        name="conv_a",
    )(u, u, cw, vec(cb), vec(lg), vec(lb))


def _lru_gates(xr, wg_ref, bg_ref, lam_ref):
    gates = _dot(xr.astype(BF16), wg_ref[...]) + bg_ref[...]
    r = jax.nn.sigmoid(gates[:, :LRU_CH])
    ig = jax.nn.sigmoid(gates[:, LRU_CH:])
    nl = -lam_ref[...]
    softplus = jnp.maximum(nl, 0.0) + jnp.log1p(jnp.exp(-jnp.abs(nl)))
    log_a = -LRU_C * r * softplus
    a = jnp.exp(log_a)
    beta = jnp.sqrt(jnp.maximum(-jnp.tanh(log_a) * (a * a + 1.0), 0.0))
    return a, beta * ig * xr


def _lru_body(br_ref, prev_ref, gb_ref, cw_ref, cb_ref, wg_ref, bg_ref, lam_ref,
              yb_ref, hl_ref, ext_ref, h_ref, *, tt):
    i = pl.program_id(1)
    c = LRU_CH

    @pl.when(i == 0)
    def _():
        h_ref[...] = jnp.zeros((1, c), F32)
        ext_ref[0:SUBLANES, :] = jnp.zeros((SUBLANES, c), F32)

    @pl.when(i != 0)
    def _():
        ext_ref[0:SUBLANES, :] = prev_ref[0]

    ext_ref[SUBLANES:SUBLANES + tt, :] = br_ref[0]
    first = SUBLANES - (CONV_B_WIDTH - 1)
    xr = cb_ref[...]
    for w in range(CONV_B_WIDTH):
        xr = xr + ext_ref[pl.ds(first + w, tt), :] * cw_ref[w:w + 1, :]
    a, u = _lru_gates(xr, wg_ref, bg_ref, lam_ref)

    row = lax.broadcasted_iota(jnp.int32, (tt, c), 0)
    d = 1
    while d < tt:
        if d < SUBLANES:
            keep = row >= d
            a_sh = jnp.where(keep, pltpu.roll(a, d, 0), 1.0)
            u_sh = jnp.where(keep, pltpu.roll(u, d, 0), 0.0)
        else:
            a_sh = jnp.concatenate([jnp.ones((d, c), F32), a[:tt - d]], axis=0)
            u_sh = jnp.concatenate([jnp.zeros((d, c), F32), u[:tt - d]], axis=0)
        u = a * u_sh + u
        a = a * a_sh
        d *= 2
    h = a * h_ref[...] + u
    yb_ref[0] = (h * gb_ref[0]).astype(yb_ref.dtype)
    h_last = h[tt - 1:tt, :]
    h_ref[...] = h_last
    hl_ref[0] = h_last


def _lru(br, gb, cw, cb, wg, bg, lam):
    b, t, c = br.shape
    tt = SEQ_TILE_LRU
    halo_blocks = tt // SUBLANES
    vec = lambda a: a.reshape(1, -1)
    return pl.pallas_call(
        functools.partial(_lru_body, tt=tt),
        grid=(b, t // tt),
        in_specs=[pl.BlockSpec((1, tt, c), lambda bi, i: (bi, i, 0)),
                  pl.BlockSpec((1, SUBLANES, c), lambda bi, i: (bi, jnp.maximum(i * halo_blocks - 1, 0), 0)),
                  pl.BlockSpec((1, tt, c), lambda bi, i: (bi, i, 0)),
                  pl.BlockSpec((CONV_B_WIDTH, c), lambda bi, i: (0, 0)),
                  pl.BlockSpec((1, c), lambda bi, i: (0, 0)),
                  pl.BlockSpec((c, 2 * c), lambda bi, i: (0, 0)),
                  pl.BlockSpec((1, 2 * c), lambda bi, i: (0, 0)),
                  pl.BlockSpec((1, c), lambda bi, i: (0, 0))],
        out_specs=[pl.BlockSpec((1, tt, c), lambda bi, i: (bi, i, 0)),
                   pl.BlockSpec((1, 1, c), lambda bi, i: (bi, 0, 0))],
        out_shape=[jax.ShapeDtypeStruct((b, t, c), BF16),
                   jax.ShapeDtypeStruct((b, 1, c), F32)],
        scratch_shapes=[pltpu.VMEM((tt + SUBLANES, c), F32), pltpu.VMEM((1, c), F32)],
        compiler_params=_params(("parallel", "arbitrary")),
        name="lru",
    )(br, br, gb, cw, vec(cb), wg, vec(bg), vec(lam))


def _sample_mix_body(u_ref, br_ref, gb_ref, ha_ref, hb_ref, h0_ref,
                     cwa_ref, cba_ref, lg_ref, lb_ref, cwb_ref, cbb_ref, wg_ref, bg_ref, lam_ref,
                     ya_ref, yb_ref, h_ref):
    u = u_ref[...]
    acc = cba_ref[...] + u * cwa_ref[CONV_A_WIDTH - 1:CONV_A_WIDTH, :]
    for w in range(CONV_A_WIDTH - 1):
        acc = acc + ha_ref[w] * cwa_ref[w:w + 1, :]
    ya_ref[...] = _layer_norm_silu(acc, lg_ref[...], lb_ref[...]).astype(ya_ref.dtype)

    br = br_ref[...]
    xr = cbb_ref[...] + br * cwb_ref[CONV_B_WIDTH - 1:CONV_B_WIDTH, :]
    for w in range(CONV_B_WIDTH - 1):
        xr = xr + hb_ref[w] * cwb_ref[w:w + 1, :]
    a, x_in = _lru_gates(xr, wg_ref, bg_ref, lam_ref)
    h = a * h0_ref[...] + x_in
    h_ref[...] = h
    yb_ref[...] = (h * gb_ref[...]).astype(yb_ref.dtype)


def _sample_mix(u, br, gb, hist_a_t, hist_b_t, h0, cwa, cba, lg, lb, cwb, cbb, wg, bg, lam):
    n, c = u.shape
    vec = lambda a: a.reshape(1, -1)
    args = [u, br, gb, hist_a_t, hist_b_t, h0, cwa, vec(cba), vec(lg), vec(lb), cwb, vec(cbb), wg, vec(bg), vec(lam)]
    full = lambda a: pl.BlockSpec(a.shape, lambda i, nd=a.ndim: (0,) * nd)
    return pl.pallas_call(
        _sample_mix_body,
        grid=(1,),
        in_specs=[full(a) for a in args],
        out_specs=[pl.BlockSpec((n, c), lambda i: (0, 0)) for _ in range(3)],
        out_shape=[jax.ShapeDtypeStruct((n, c), BF16), jax.ShapeDtypeStruct((n, c), BF16),
                   jax.ShapeDtypeStruct((n, c), F32)],
        compiler_params=_params(("arbitrary",)),
        name="sample_mix",
    )(*args)


def _diff_lambda(lq1_ref, lk1_ref, lq2_ref, lk2_ref, lam_init):
    e1 = jnp.exp(jnp.sum(lq1_ref[...] * lk1_ref[...], axis=-1, keepdims=True))
    e2 = jnp.exp(jnp.sum(lq2_ref[...] * lk2_ref[...], axis=-1, keepdims=True))
    return e1 - e2 + lam_init


Q_SCALE = ATT_HD ** -0.5 * math.log2(math.e)
VT_ROWS = ATT_VD + 16


def _qkv_body(x_ref, g_ref, wq_ref, wk_ref, wv_ref, qt_ref, k_ref, v_ref, kb_ref, vt_ref):
    xn = _rms(x_ref[...], g_ref[...]).astype(BF16)
    q = _dot(xn, wq_ref[...]) * Q_SCALE
    qt_ref[0] = q.T.astype(BF16)
    k = _dot(xn, wk_ref[...])
    _store_heads(k_ref, k, ATT_HEADS)
    kb_ref[...] = k.astype(BF16)
    v = _dot(xn, wv_ref[...])
    _store_heads(v_ref, v, ATT_HEADS)
    vt = v.T.astype(BF16)
    ones = jnp.ones((VT_ROWS - ATT_VD, vt.shape[1]), BF16)
    for h in range(ATT_HEADS):
        vt_ref[0, h * VT_ROWS:h * VT_ROWS + ATT_VD, :] = vt[h * ATT_VD:(h + 1) * ATT_VD, :]
        vt_ref[0, h * VT_ROWS + ATT_VD:(h + 1) * VT_ROWS, :] = ones


def _qkv(x, g, w, bsz, seq):
    m, d = x.shape
    tm = ROW_TILE
    per_seq = seq // tm
    row = pl.BlockSpec((tm, d), lambda i: (i, 0))
    heads = pl.BlockSpec((tm, ATT_HEADS, ATT_VD), lambda i: (i, 0, 0))
    col = lambda rows: pl.BlockSpec((1, rows, tm), lambda i: (i // per_seq, 0, i % per_seq))
    wspec = lambda b: pl.BlockSpec((d, d), lambda i: (0, b))
    return pl.pallas_call(
        _qkv_body,
        grid=(m // tm,),
        in_specs=[row, pl.BlockSpec((1, d), lambda i: (0, 0)), wspec(0), wspec(1), wspec(2)],
        out_specs=[col(d), heads, heads, row, col(ATT_HEADS * VT_ROWS)],
        out_shape=[jax.ShapeDtypeStruct((bsz, d, seq), BF16),
                   jax.ShapeDtypeStruct((m, ATT_HEADS, ATT_VD), F32),
                   jax.ShapeDtypeStruct((m, ATT_HEADS, ATT_VD), F32),
                   jax.ShapeDtypeStruct((m, d), BF16),
                   jax.ShapeDtypeStruct((bsz, ATT_HEADS * VT_ROWS, seq), BF16)],
        compiler_params=_params(("parallel",)),
        name="qkv",
    )(x, g.reshape(1, d), w, w, w)


def _flash_body(qi_ref, ki_ref, qt_ref, k_ref, vt_ref, lq1_ref, lk1_ref, lq2_ref, lk2_ref, g_ref, o_ref,
                qs_ref, m_ref, acc_ref, s_ref, *, tq, lam_init):
    step_id = pl.program_id(1)
    qi = qi_ref[step_id]
    ki = ki_ref[step_id]
    vd = ATT_VD

    @pl.when(ki == 0)
    def _():
        dim = lax.broadcasted_iota(jnp.int32, (vd, tq), 0)
        for h in range(ATT_HEADS):
            qh = qt_ref[0, h * vd:(h + 1) * vd, :]
            zero = jnp.zeros_like(qh)
            qs_ref[h, :, 0:tq] = jnp.where(dim < ATT_HD, qh, zero)
            qs_ref[h, :, tq:2 * tq] = jnp.where(dim >= ATT_HD, qh, zero)
        m_ref[...] = jnp.full(m_ref.shape, NEG_INF, F32)
        acc_ref[...] = jnp.zeros(acc_ref.shape, F32)

    def scores(h):
        s_ref[h % 2] = _dot(k_ref[0, :, h * vd:(h + 1) * vd], qs_ref[h])

    def step(diagonal):
        if diagonal:
            key = lax.broadcasted_iota(jnp.int32, (tq, 2 * tq), 0)
            qry = lax.broadcasted_iota(jnp.int32, (tq, 2 * tq), 1)
            visible = key <= jnp.where(qry >= tq, qry - tq, qry)
        scores(0)
        for h in range(ATT_HEADS):
            if h + 1 < ATT_HEADS:
                scores(h + 1)
            s = s_ref[h % 2]
            if diagonal:
                s = jnp.where(visible, s, NEG_INF)
            m_prev = m_ref[h]
            m_new = jnp.maximum(m_prev, jnp.max(s, axis=0, keepdims=True))
            alpha = jnp.exp2(m_prev - m_new)
            p = jnp.exp2(s - m_new).astype(BF16)
            vth = vt_ref[0, h * VT_ROWS:(h + 1) * VT_ROWS, :]
            acc_ref[h] = alpha * acc_ref[h] + _dot(vth, p)
            m_ref[h] = m_new

    @pl.when(ki < qi)
    def _():
        step(False)

    @pl.when(ki == qi)
    def _():
        step(True)
        lam = _diff_lambda(lq1_ref, lk1_ref, lq2_ref, lk2_ref, lam_init)
        g = g_ref[...]
        for h in range(ATT_HEADS):
            o = acc_ref[h, 0:vd, :] / acc_ref[h, vd:vd + 1, :]
            att = o[:, 0:tq] - lam * o[:, tq:2 * tq]
            ms = jnp.mean(att * att, axis=0, keepdims=True)
            att = att * lax.rsqrt(ms + EPS) * g * (1.0 - lam_init)
            o_ref[0, :, h * vd:(h + 1) * vd] = att.T.astype(o_ref.dtype)


def _flash(qt, k, vt, lam_p, g, lam_init):
    b, d, t = qt.shape
    tq = ATT_TQ
    nq = t // tq
    pairs = [(qi, ki) for qi in range(nq) for ki in range(qi + 1)]
    qi_tab = jnp.asarray([p[0] for p in pairs], jnp.int32)
    ki_tab = jnp.asarray([p[1] for p in pairs], jnp.int32)
    vec = lambda a: a.reshape(1, -1)
    small = lambda n: pl.BlockSpec((1, n), lambda bi, s, qi, ki: (0, 0))
    grid_spec = pltpu.PrefetchScalarGridSpec(
        num_scalar_prefetch=2,
        grid=(b, len(pairs)),
        in_specs=[pl.BlockSpec((1, d, tq), lambda bi, s, qi, ki: (bi, 0, qi[s])),
                  pl.BlockSpec((1, tq, d), lambda bi, s, qi, ki: (bi, ki[s], 0)),
                  pl.BlockSpec((1, ATT_HEADS * VT_ROWS, tq), lambda bi, s, qi, ki: (bi, 0, ki[s])),
                  small(ATT_HD), small(ATT_HD), small(ATT_HD), small(ATT_HD),
                  pl.BlockSpec((ATT_VD, 1), lambda bi, s, qi, ki: (0, 0))],
        out_specs=pl.BlockSpec((1, tq, d), lambda bi, s, qi, ki: (bi, qi[s], 0)),
        scratch_shapes=[pltpu.VMEM((ATT_HEADS, ATT_VD, 2 * tq), BF16),
                        pltpu.VMEM((ATT_HEADS, 1, 2 * tq), F32),
                        pltpu.VMEM((ATT_HEADS, VT_ROWS, 2 * tq), F32),
                        pltpu.VMEM((2, tq, 2 * tq), F32)],
    )
    return pl.pallas_call(
        functools.partial(_flash_body, tq=tq, lam_init=lam_init),
        grid_spec=grid_spec,
        out_shape=jax.ShapeDtypeStruct((b, t, d), BF16),
        compiler_params=_params(("parallel", "arbitrary")),
        name="flash_diff_attn",
    )(qi_tab, ki_tab, qt, k, vt, *[vec(a) for a in lam_p], g.reshape(ATT_VD, 1))


def _decode_body(pt_ref, q_ref, kn_ref, vn_ref, lq1_ref, lk1_ref, lq2_ref, lk2_ref, g_ref, *rest,
                 n_pages, lam_init):
    k_refs = rest[:n_pages]
    v_refs = rest[n_pages:2 * n_pages]
    o_ref, qx_ref, m_ref, l_ref, acc_ref = rest[2 * n_pages:]
    j = pl.program_id(1)
    nh = ATT_HEADS
    rows = PAGE_SIZE * nh

    @pl.when(j == 0)
    def _():
        q8 = q_ref[0].astype(F32)
        lane = lax.broadcasted_iota(jnp.int32, (nh, ATT_VD), 1)
        qx = jnp.concatenate([jnp.where(lane < ATT_HD, q8, 0.0), jnp.where(lane >= ATT_HD, q8, 0.0)], axis=0)
        qx_ref[...] = qx.astype(BF16)
        kn = kn_ref[0].astype(BF16).astype(F32)
        vn = vn_ref[0].astype(BF16).astype(F32)
        m_ref[...] = jnp.sum(qx * jnp.concatenate([kn, kn], axis=0), axis=-1, keepdims=True)
        l_ref[...] = jnp.ones(l_ref.shape, F32)
        acc_ref[...] = jnp.concatenate([vn, vn], axis=0)

    qx = qx_ref[...]
    s = jnp.concatenate([_dot_nt(qx, k_ref[0].reshape(rows, ATT_VD).astype(BF16)) for k_ref in k_refs], axis=1)
    r = lax.broadcasted_iota(jnp.int32, s.shape, 0)
    c = lax.broadcasted_iota(jnp.int32, s.shape, 1)
    s = jnp.where(c % nh == r % nh, s, NEG_INF)
    m_prev = m_ref[...]
    m_new = jnp.maximum(m_prev, jnp.max(s, axis=-1, keepdims=True))
    alpha = jnp.exp2(m_prev - m_new)
    p = jnp.exp2(s - m_new)
    l_ref[...] = alpha * l_ref[...] + jnp.sum(p, axis=-1, keepdims=True)
    pb = p.astype(BF16)
    acc = alpha * acc_ref[...]
    for i, v_ref in enumerate(v_refs):
        acc = acc + _dot(pb[:, i * rows:(i + 1) * rows], v_ref[0].reshape(rows, ATT_VD).astype(BF16))
    acc_ref[...] = acc
    m_ref[...] = m_new

    @pl.when(j == pl.num_programs(1) - 1)
    def _():
        lam = _diff_lambda(lq1_ref, lk1_ref, lq2_ref, lk2_ref, lam_init)
        o = acc_ref[...] / l_ref[...]
        att = o[0:nh] - lam * o[nh:2 * nh]
        o_ref[0] = (_rms(att, g_ref[...]) * (1.0 - lam_init)).astype(o_ref.dtype)


def _decode(q, k_new, v_new, cache_k, cache_v, page_base, page_table, lam_p, g, lam_init):
    b = q.shape[0]
    nh, vd = ATT_HEADS, ATT_VD
    n_pages_total = page_table.shape[1]
    pp = DEC_PAGES
    vec = lambda a: a.reshape(1, -1)
    row_spec = pl.BlockSpec((1, nh, vd), lambda bi, j, pt: (bi, 0, 0))
    small = lambda n: pl.BlockSpec((1, n), lambda bi, j, pt: (0, 0))
    page_specs = [pl.BlockSpec((1, PAGE_SIZE, nh, vd),
                               lambda bi, j, pt, i=i: (page_base + pt[bi, j * pp + i], 0, 0, 0))
                  for i in range(pp)]
    grid_spec = pltpu.PrefetchScalarGridSpec(
        num_scalar_prefetch=1,
        grid=(b, n_pages_total // pp),
        in_specs=[row_spec, row_spec, row_spec,
                  small(ATT_HD), small(ATT_HD), small(ATT_HD), small(ATT_HD), small(vd)]
                 + page_specs + page_specs,
        out_specs=pl.BlockSpec((1, nh, vd), lambda bi, j, pt: (bi, 0, 0)),
        scratch_shapes=[pltpu.VMEM((2 * nh, vd), BF16),
                        pltpu.VMEM((2 * nh, 1), F32),
                        pltpu.VMEM((2 * nh, 1), F32),
                        pltpu.VMEM((2 * nh, vd), F32)],
    )
    return pl.pallas_call(
        functools.partial(_decode_body, n_pages=pp, lam_init=lam_init),
        grid_spec=grid_spec,
        out_shape=jax.ShapeDtypeStruct((b, nh, vd), BF16),
        compiler_params=_params(("parallel", "arbitrary")),
        name="paged_diff_attn",
    )(page_table, q, k_new, v_new, *[vec(a) for a in lam_p], vec(g),
      *([cache_k] * pp), *([cache_v] * pp))


def _xattn_head(q, k, v):
    s = _dot_nt(q, k)
    p = jnp.exp(s - jnp.max(s, axis=-1, keepdims=True))
    p = p / jnp.sum(p, axis=-1, keepdims=True)
    return _dot(p.astype(BF16), v)


def _xattn_body(q_ref, k_ref, v_ref, o_ref, *, bb):
    hd = XATT_HD
    for i in range(bb):
        for h in range(XATT_HEADS):
            cols = slice(h * hd, (h + 1) * hd)
            o_ref[i, :, cols] = _xattn_head(q_ref[i, :, cols], k_ref[i, :, cols], v_ref[i, :, cols]).astype(o_ref.dtype)


def _xattn(q, mem_k, mem_v, bb, tq):
    b, t, d = q.shape
    kv_spec = pl.BlockSpec((bb, N_MEM, d), lambda bi, i: (bi, 0, 0))
    return pl.pallas_call(
        functools.partial(_xattn_body, bb=bb),
        grid=(b // bb, t // tq),
        in_specs=[pl.BlockSpec((bb, tq, d), lambda bi, i: (bi, i, 0)), kv_spec, kv_spec],
        out_specs=pl.BlockSpec((bb, tq, d), lambda bi, i: (bi, i, 0)),
        out_shape=jax.ShapeDtypeStruct((b, t, d), BF16),
        compiler_params=_params(("parallel", "parallel")),
        name="xattn",
    )(q, mem_k, mem_v)


def _memkv_body(x_ref, wk_ref, wv_ref, kn_ref, vn_ref, kb_ref, vb_ref):
    xb = x_ref[...].astype(BF16)
    for w_ref, n_ref, b_ref in ((wk_ref, kn_ref, kb_ref), (wv_ref, vn_ref, vb_ref)):
        y = _dot(xb, w_ref[...])
        b_ref[...] = y.astype(BF16)
        _store_heads(n_ref, y, XATT_HEADS)


def _memkv(x, w):
    m, d = x.shape
    tm = _row_tile(m)
    row = pl.BlockSpec((tm, d), lambda i: (i, 0))
    heads = pl.BlockSpec((tm, XATT_HEADS, XATT_HD), lambda i: (i, 0, 0))
    return pl.pallas_call(
        _memkv_body,
        grid=(m // tm,),
        in_specs=[row, pl.BlockSpec((d, d), lambda i: (0, 0)), pl.BlockSpec((d, d), lambda i: (0, 1))],
        out_specs=[heads, heads, row, row],
        out_shape=[jax.ShapeDtypeStruct((m, XATT_HEADS, XATT_HD), F32),
                   jax.ShapeDtypeStruct((m, XATT_HEADS, XATT_HD), F32),
                   jax.ShapeDtypeStruct((m, d), BF16),
                   jax.ShapeDtypeStruct((m, d), BF16)],
        compiler_params=_params(("parallel",)),
        name="memkv",
    )(x, w, w)


def _block_diag(w):
    nb, n, _ = w.shape
    eye = jnp.eye(nb, dtype=w.dtype)
    return jnp.einsum('kij,kl->kilj', w, eye).reshape(nb * n, nb * n)


def _stack(arrays):
    return arrays[0][None] if len(arrays) == 1 else jnp.stack(arrays)


def _trunk(x3, xattn, p, even_mix, odd_mix):
    b, t, d = x3.shape
    x = x3.reshape(b * t, d)
    depth = p['ffn1_g'].shape[0]
    for l in range(depth):
        x = _ffn(x, p['ffn1_g'][l], p['ffn1_w_in'][l], p['ffn1_w_out'][l])
        post = (p['xattn_g'][l], p['xattn_w_q'][l], XATT_HD ** -0.5)
        x, q = even_mix(x, l // 2, post) if l % 2 == 0 else odd_mix(x, l // 2, l, post)
        o = xattn(q.reshape(b, t, d), l)
        last = l == depth - 1
        x = _ffn(x, p['ffn2_g'][l], p['ffn2_w_in'][l], p['ffn2_w_out'][l],
                 final_g=p['final_g'] if last else None, pre=(o.reshape(b * t, d), p['xattn_w_out'][l]))
    return x.reshape(b, t, d)


def kernel(x_prompt, x_sample, state_conv_a, state_conv_b, state_lru, cache_k, cache_v, cache_mem_k, cache_mem_v, page_table, mem_prompt, ffn1_g, ffn1_w_in, ffn1_w_out, mix_g, even_w_in, conv_a_w, conv_a_b, conv_a_ln_g, conv_a_ln_b, conv_b_w, conv_b_b, lru_w_a, lru_b_a, lru_w_x, lru_b_x, lru_lambda, even_w_out, attn_w_in, lam_q1, lam_k1, lam_q2, lam_k2, attn_subln_g, attn_w_out, xattn_g, xattn_w_q, xattn_w_kv, xattn_w_out, ffn2_g, ffn2_w_in, ffn2_w_out, final_g):
    bsz, seq, d = x_prompt.shape
    dbsz, dseq, _ = x_sample.shape
    depth = ffn1_g.shape[0]
    n_even = even_w_in.shape[0]
    n_odd = attn_w_in.shape[0]
    bf = lambda w: w.astype(BF16)
    p = {
        'ffn1_g': ffn1_g, 'ffn1_w_in': bf(ffn1_w_in), 'ffn1_w_out': bf(ffn1_w_out),
        'ffn2_g': ffn2_g, 'ffn2_w_in': bf(ffn2_w_in), 'ffn2_w_out': bf(ffn2_w_out),
        'xattn_g': xattn_g, 'xattn_w_q': bf(xattn_w_q), 'xattn_w_out': bf(xattn_w_out),
        'final_g': final_g,
    }
    even_w_in_b = bf(even_w_in)
    even_w_out_b = bf(even_w_out)
    attn_w_in_b = bf(attn_w_in)
    attn_w_out_b = bf(attn_w_out)
    xattn_w_kv_b = bf(xattn_w_kv)
    lru_wg = [bf(jnp.concatenate([_block_diag(lru_w_a[e]), _block_diag(lru_w_x[e])], axis=1)) for e in range(n_even)]
    lru_bg = [jnp.concatenate([lru_b_a[e], lru_b_x[e]]) for e in range(n_even)]
    c = CONV_CH
    lam_inits = [0.8 - 0.6 * math.exp(-0.3 * (2 * o + 1)) for o in range(n_odd)]

    mem2 = mem_prompt.reshape(bsz * N_MEM, d)
    p_mem_k, p_mem_v, p_mem_kb, p_mem_vb = [], [], [], []
    for l in range(depth):
        mk, mv, mkb, mvb = _memkv(mem2, xattn_w_kv_b[l])
        p_mem_k.append(mk.reshape(bsz, N_MEM, XATT_HEADS, XATT_HD))
        p_mem_v.append(mv.reshape(bsz, N_MEM, XATT_HEADS, XATT_HD))
        p_mem_kb.append(mkb.reshape(bsz, N_MEM, d))
        p_mem_vb.append(mvb.reshape(bsz, N_MEM, d))

    p_conv_a, p_conv_b, p_lru, p_k, p_v = [], [], [], [], []

    def p_even(x, e, post):
        u, br, gb = _even_in(x, mix_g[2 * e], even_w_in_b[e])
        u3, br3, gb3 = (a.reshape(bsz, seq, c) for a in (u, br, gb))
        ya = _conv_a(u3, conv_a_w[e], conv_a_b[e], conv_a_ln_g[e], conv_a_ln_b[e])
        yb, h_last = _lru(br3, gb3, conv_b_w[e], conv_b_b[e], lru_wg[e], lru_bg[e], lru_lambda[e])
        p_conv_a.append(u3[:, seq - (CONV_A_WIDTH - 1):])
        p_conv_b.append(br3[:, seq - (CONV_B_WIDTH - 1):])
        p_lru.append(h_last.reshape(bsz, c))
        return _outproj(x, [ya.reshape(-1, c), yb.reshape(-1, c)], [even_w_out_b[e][:c], even_w_out_b[e][c:]], post)

    def p_odd(x, o, l, post):
        qt, k, v, kb, vt = _qkv(x, mix_g[l], attn_w_in_b[o], bsz, seq)
        p_k.append(k.reshape(bsz, seq, ATT_HEADS, ATT_VD))
        p_v.append(v.reshape(bsz, seq, ATT_HEADS, ATT_VD))
        att = _flash(qt, kb.reshape(bsz, seq, d), vt,
                     (lam_q1[o], lam_k1[o], lam_q2[o], lam_k2[o]), attn_subln_g[o], lam_inits[o])
        return _outproj(x, [att.reshape(-1, d)], [attn_w_out_b[o]], post)

    y_prompt = _trunk(x_prompt, lambda q, l: _xattn(q, p_mem_kb[l], p_mem_vb[l], 1, ROW_TILE), p, p_even, p_odd)

    s_conv_a, s_conv_b, s_lru, s_k, s_v = [], [], [], [], []
    pool = cache_k.shape[1]
    pages_k = cache_k.reshape(n_odd * pool, PAGE_SIZE, ATT_HEADS, ATT_VD)
    pages_v = cache_v.reshape(n_odd * pool, PAGE_SIZE, ATT_HEADS, ATT_VD)

    def s_even(x, e, post):
        u, br, gb = _even_in(x, mix_g[2 * e], even_w_in_b[e])
        ya, yb, h_new = _sample_mix(
            u, br, gb, jnp.swapaxes(state_conv_a[e], 0, 1), jnp.swapaxes(state_conv_b[e], 0, 1), state_lru[e],
            conv_a_w[e], conv_a_b[e], conv_a_ln_g[e], conv_a_ln_b[e],
            conv_b_w[e], conv_b_b[e], lru_wg[e], lru_bg[e], lru_lambda[e])
        s_conv_a.append(jnp.concatenate([state_conv_a[e][:, 1:], u[:, None, :]], axis=1))
        s_conv_b.append(jnp.concatenate([state_conv_b[e][:, 1:], br[:, None, :]], axis=1))
        s_lru.append(h_new)
        return _outproj(x, [ya, yb], [even_w_out_b[e][:c], even_w_out_b[e][c:]], post)

    def s_odd(x, o, l, post):
        q, k, v = _proj(x, mix_g[l], attn_w_in_b[o], d, [0, 1, 2],
                        [(0, BF16, Q_SCALE), (1, F32, 1.0), (2, F32, 1.0)])
        heads = (dbsz, ATT_HEADS, ATT_VD)
        s_k.append(k.reshape(dbsz, dseq, ATT_HEADS, ATT_VD))
        s_v.append(v.reshape(dbsz, dseq, ATT_HEADS, ATT_VD))
        att = _decode(q.reshape(heads), k.reshape(heads), v.reshape(heads), pages_k, pages_v, o * pool,
                      page_table, (lam_q1[o], lam_k1[o], lam_q2[o], lam_k2[o]), attn_subln_g[o], lam_inits[o])
        return _outproj(x, [att.reshape(dbsz, d)], [attn_w_out_b[o]], post)

    s_mem_k = [bf(cache_mem_k[l].reshape(dbsz, N_MEM, d)) for l in range(depth)]
    s_mem_v = [bf(cache_mem_v[l].reshape(dbsz, N_MEM, d)) for l in range(depth)]
    y_sample = _trunk(x_sample, lambda q, l: _xattn(q, s_mem_k[l], s_mem_v[l], XS_BATCH, dseq), p, s_even, s_odd)

    return (y_prompt, y_sample,
            _stack(p_conv_a), _stack(p_conv_b), _stack(p_lru),
            _stack(p_k), _stack(p_v),
            _stack(p_mem_k), _stack(p_mem_v),
            _stack(s_conv_a), _stack(s_conv_b), _stack(s_lru),
            _stack(s_k), _stack(s_v))
```

```python
import functools
import math

import jax
import jax.numpy as jnp
from jax import lax
from jax.experimental import pallas as pl
from jax.experimental.pallas import tpu as pltpu

F32 = jnp.float32
BF16 = jnp.bfloat16

D_MODEL = 1024
D_FF = 2816
CONV_CH = 512
CONV_A_WIDTH = 31
LRU_CH = 512
LRU_BLOCKS = 8
CONV_B_WIDTH = 4
LRU_C = 8.0
ATT_HEADS = 8
ATT_HD = 64
ATT_VD = 128
N_MEM = 256
XATT_HEADS = 4
XATT_HD = 256
PAGE_SIZE = 128
EPS = 1e-6
NEG_INF = -1e30

SUBLANES = 8
VMEM_LIMIT = 56 * 1024 * 1024

MXU_COLS = 256
FFN_CHUNK = 4 * MXU_COLS
XS_BATCH = 4
ROW_TILE = 512
SEQ_TILE_CONV = 512
SEQ_TILE_LRU = 256
ATT_TQ = 512
DEC_PAGES = 16


def _params(sem):
    return pltpu.CompilerParams(dimension_semantics=sem, vmem_limit_bytes=VMEM_LIMIT)


def _rms(x, g):
    return x * lax.rsqrt(jnp.mean(x * x, axis=-1, keepdims=True) + EPS) * g


def _dot(a, b):
    return jnp.dot(a, b, preferred_element_type=F32)


def _dot_nt(a, b):
    return lax.dot_general(a, b, (((1,), (1,)), ((), ())), preferred_element_type=F32)


def _row_tile(m):
    return ROW_TILE if m % ROW_TILE == 0 else m


def _store_heads(o_ref, y, n_heads):
    o_ref[...] = y.reshape(y.shape[0], n_heads, y.shape[1] // n_heads)


def _ffn_chunks(f):
    return [(c0, min(FFN_CHUNK, f - c0)) for c0 in range(0, f, FFN_CHUNK)]


def _ffn_body(*refs, final, pre):
    refs = list(refs)
    x_ref, g_ref, wi_ref, wo_ref = refs[:4]
    pos = 4
    if pre:
        y_ref, wp_ref = refs[pos:pos + 2]
        pos += 2
    if final:
        gf_ref = refs[pos]
        pos += 1
    o_ref = refs[pos]
    f = wo_ref.shape[0]
    x = x_ref[...]
    if pre:
        x = x + _dot(y_ref[...], wp_ref[...])
    xn = _rms(x, g_ref[...]).astype(BF16)
    acc = None
    for c0, cw in _ffn_chunks(f):
        gate = _dot(xn, wi_ref[:, c0:c0 + cw])
        up = _dot(xn, wi_ref[:, f + c0:f + c0 + cw])
        h = (gate * jax.nn.sigmoid(gate) * up).astype(BF16)
        part = _dot(h, wo_ref[c0:c0 + cw, :])
        acc = part if acc is None else acc + part
    y = x + 0.5 * acc
    if final:
        y = _rms(y, gf_ref[...])
    o_ref[...] = y


def _ffn(x, g, w_in, w_out, layer, final_g=None, pre=None):
    m, d = x.shape
    f = w_out.shape[1]
    tm = _row_tile(m)
    final = final_g is not None
    const = lambda shape: pl.BlockSpec(shape, lambda i: (0, 0), pipeline_mode=pl.Buffered(1))
    of_layer = lambda w: pl.BlockSpec((None,) + w.shape[1:], lambda i: (layer, 0, 0), pipeline_mode=pl.Buffered(1))
    in_specs = [pl.BlockSpec((tm, d), lambda i: (i, 0)), const((1, d)), of_layer(w_in), of_layer(w_out)]
    args = [x, g.reshape(1, d), w_in, w_out]
    if pre is not None:
        y, wp = pre
        in_specs += [pl.BlockSpec((tm, y.shape[1]), lambda i: (i, 0)), of_layer(wp)]
        args += [y, wp]
    if final:
        in_specs.append(const((1, d)))
        args.append(final_g.reshape(1, d))
    return pl.pallas_call(
        functools.partial(_ffn_body, final=final, pre=pre is not None),
        grid=(m // tm,),
        in_specs=in_specs,
        out_specs=pl.BlockSpec((tm, d), lambda i: (i, 0)),
        out_shape=jax.ShapeDtypeStruct((m, d), F32),
        compiler_params=_params(("parallel",)),
        name="ffn",
    )(*args)


def _proj_body(*refs, has_g, n_w, out_meta):
    x_ref = refs[0]
    pos = 1
    if has_g:
        g_ref = refs[pos]
        pos += 1
    w_refs = refs[pos:pos + n_w]
    o_refs = refs[pos + n_w:]
    x = x_ref[...]
    if has_g:
        x = _rms(x, g_ref[...])
    xb = x.astype(BF16)
    ys = [_dot(xb, w_ref[...]) for w_ref in w_refs]
    for o_ref, (wi, scale) in zip(o_refs, out_meta):
        y = ys[wi]
        if scale != 1.0:
            y = y * scale
        o_ref[...] = y.astype(o_ref.dtype)


def _proj(x, g, w, width, blocks, outs):
    m, k = x.shape
    tm = _row_tile(m)
    has_g = g is not None
    in_specs = [pl.BlockSpec((tm, k), lambda i: (i, 0))]
    args = [x]
    if has_g:
        in_specs.append(pl.BlockSpec((1, k), lambda i: (0, 0)))
        args.append(g.reshape(1, k))
    for b in blocks:
        in_specs.append(pl.BlockSpec((k, width), lambda i, b=b: (0, b)))
        args.append(w)
    return pl.pallas_call(
        functools.partial(_proj_body, has_g=has_g, n_w=len(blocks),
                          out_meta=tuple((wi, sc) for wi, _, sc in outs)),
        grid=(m // tm,),
        in_specs=in_specs,
        out_specs=[pl.BlockSpec((tm, width), lambda i: (i, 0)) for _ in outs],
        out_shape=[jax.ShapeDtypeStruct((m, width), dt) for _, dt, _ in outs],
        compiler_params=_params(("parallel",)),
        name="proj",
    )(*args)


def _outproj_body(*refs, n, post_scale):
    x_ref = refs[0]
    y_refs = refs[1:1 + n]
    w_refs = refs[1 + n:1 + 2 * n]
    pos = 1 + 2 * n
    if post_scale is not None:
        g_ref, wq_ref = refs[pos:pos + 2]
        pos += 2
    o_ref = refs[pos]
    acc = x_ref[...]
    for y_ref, w_ref in zip(y_refs, w_refs):
        acc = acc + _dot(y_ref[...], w_ref[...])
    o_ref[...] = acc
    if post_scale is not None:
        q_ref = refs[pos + 1]
        q = _dot(_rms(acc, g_ref[...]).astype(BF16), wq_ref[...]) * post_scale
        q_ref[...] = q.astype(q_ref.dtype)


def _outproj(x, ys, ws, post=None):
    m, d = x.shape
    tm = _row_tile(m)
    n = len(ys)
    row = pl.BlockSpec((tm, d), lambda i: (i, 0))
    in_specs = [row]
    in_specs += [pl.BlockSpec((tm, y.shape[1]), lambda i: (i, 0)) for y in ys]
    in_specs += [pl.BlockSpec(w.shape, lambda i: (0, 0)) for w in ws]
    args = [x, *ys, *ws]
    out_specs = row
    out_shape = jax.ShapeDtypeStruct((m, d), F32)
    post_scale = None
    if post is not None:
        g, wq, post_scale = post
        in_specs += [pl.BlockSpec((1, d), lambda i: (0, 0)), pl.BlockSpec(wq.shape, lambda i: (0, 0))]
        args += [g.reshape(1, d), wq]
        out_specs = [row, pl.BlockSpec((tm, wq.shape[1]), lambda i: (i, 0))]
        out_shape = [out_shape, jax.ShapeDtypeStruct((m, wq.shape[1]), BF16)]
    return pl.pallas_call(
        functools.partial(_outproj_body, n=n, post_scale=post_scale),
        grid=(m // tm,),
        in_specs=in_specs,
        out_specs=out_specs,
        out_shape=out_shape,
        compiler_params=_params(("parallel",)),
        name="outproj",
    )(*args)


def _even_in_body(x_ref, g_ref, w_ref, u_ref, br_ref, gb_ref):
    xn = _rms(x_ref[...], g_ref[...]).astype(BF16)
    z = _dot(xn, w_ref[...])
    c = CONV_CH
    u_ref[...] = z[:, :c] * jax.nn.sigmoid(z[:, c:2 * c])
    br_ref[...] = z[:, 2 * c:2 * c + LRU_CH]
    gb_ref[...] = jax.nn.gelu(z[:, 2 * c + LRU_CH:])


def _even_in(x, g, w):
    m, d = x.shape
    tm = _row_tile(m)
    n = w.shape[1]
    return pl.pallas_call(
        _even_in_body,
        grid=(m // tm,),
        in_specs=[pl.BlockSpec((tm, d), lambda i: (i, 0)),
                  pl.BlockSpec((1, d), lambda i: (0, 0)),
                  pl.BlockSpec((d, n), lambda i: (0, 0))],
        out_specs=[pl.BlockSpec((tm, CONV_CH), lambda i: (i, 0)) for _ in range(3)],
        out_shape=[jax.ShapeDtypeStruct((m, CONV_CH), F32) for _ in range(3)],
        compiler_params=_params(("parallel",)),
        name="even_in",
    )(x, g.reshape(1, d), w)


def _layer_norm_silu(y, g, b):
    mu = jnp.mean(y, axis=-1, keepdims=True)
    yc = y - mu
    var = jnp.mean(yc * yc, axis=-1, keepdims=True)
    yn = yc * lax.rsqrt(var + EPS) * g + b
    return yn * jax.nn.sigmoid(yn)


CONV_HALO = 32
CONV_CHUNK = 32


def _conv_a_body(u_ref, prev_ref, cw_ref, cb_ref, lg_ref, lb_ref, o_ref, ext_ref, rot_ref, *, tt):
    i = pl.program_id(1)
    c = CONV_CH
    n_ext = tt + CONV_HALO

    @pl.when(i == 0)
    def _():
        ext_ref[0:CONV_HALO, :] = jnp.zeros((CONV_HALO, c), F32)

    @pl.when(i != 0)
    def _():
        ext_ref[0:CONV_HALO, :] = prev_ref[0]

    ext_ref[CONV_HALO:n_ext, :] = u_ref[0]
    ext_ref[n_ext:n_ext + SUBLANES, :] = jnp.zeros((SUBLANES, c), F32)
    for r in range(SUBLANES):
        rot_ref[r] = ext_ref[pl.ds(r, n_ext), :]
    first = CONV_HALO - (CONV_A_WIDTH - 1)
    cb = cb_ref[...]
    lg = lg_ref[...]
    lb = lb_ref[...]
    for r0 in range(0, tt, CONV_CHUNK):
        acc = jnp.zeros((CONV_CHUNK // SUBLANES, SUBLANES, c), F32)
        for w in range(CONV_A_WIDTH):
            s = first + w
            tap = rot_ref[s % SUBLANES, pl.ds(r0 + (s // SUBLANES) * SUBLANES, CONV_CHUNK), :]
            acc = acc + tap.reshape(acc.shape) * cw_ref[w * SUBLANES:(w + 1) * SUBLANES, :][None]
        y = acc.reshape(CONV_CHUNK, c) + cb
        o_ref[0, r0:r0 + CONV_CHUNK, :] = _layer_norm_silu(y, lg, lb).astype(o_ref.dtype)


def _conv_a(u, cw, cb, lg, lb):
    b, t, c = u.shape
    tt = SEQ_TILE_CONV
    halo_blocks = tt // CONV_HALO
    vec = lambda a: a.reshape(1, c)
    return pl.pallas_call(
        functools.partial(_conv_a_body, tt=tt),
        grid=(b, t // tt),
        in_specs=[pl.BlockSpec((1, tt, c), lambda bi, i: (bi, i, 0)),
                  pl.BlockSpec((1, CONV_HALO, c), lambda bi, i: (bi, jnp.maximum(i * halo_blocks - 1, 0), 0)),
                  pl.BlockSpec((CONV_A_WIDTH * SUBLANES, c), lambda bi, i: (0, 0)),
                  pl.BlockSpec((1, c), lambda bi, i: (0, 0)),
                  pl.BlockSpec((1, c), lambda bi, i: (0, 0)),
                  pl.BlockSpec((1, c), lambda bi, i: (0, 0))],
        out_specs=pl.BlockSpec((1, tt, c), lambda bi, i: (bi, i, 0)),
        out_shape=jax.ShapeDtypeStruct((b, t, c), BF16),
        scratch_shapes=[pltpu.VMEM((tt + CONV_HALO + SUBLANES, c), F32),
                        pltpu.VMEM((SUBLANES, tt + CONV_HALO, c), F32)],
        compiler_params=_params(("parallel", "arbitrary")),
        name="conv_a",
    )(u, u, jnp.repeat(cw, SUBLANES, axis=0), vec(cb), vec(lg), vec(lb))


def _lru_gates(xr, wg_ref, bg_ref, lam_ref):
    gates = _dot(xr.astype(BF16), wg_ref[...]) + bg_ref[...]
    r = jax.nn.sigmoid(gates[:, :LRU_CH])
    ig = jax.nn.sigmoid(gates[:, LRU_CH:])
    nl = -lam_ref[...]
    softplus = jnp.maximum(nl, 0.0) + jnp.log1p(jnp.exp(-jnp.abs(nl)))
    log_a = -LRU_C * r * softplus
    a = jnp.exp(log_a)
    beta = jnp.sqrt(jnp.maximum(-jnp.tanh(log_a) * (a * a + 1.0), 0.0))
    return a, beta * ig * xr


def _lru_body(br_ref, prev_ref, gb_ref, cw_ref, cb_ref, wg_ref, bg_ref, lam_ref,
              yb_ref, hl_ref, ext_ref, h_ref, *, tt):
    i = pl.program_id(1)
    c = LRU_CH

    @pl.when(i == 0)
    def _():
        h_ref[...] = jnp.zeros((1, c), F32)
        ext_ref[0:SUBLANES, :] = jnp.zeros((SUBLANES, c), F32)

    @pl.when(i != 0)
    def _():
        ext_ref[0:SUBLANES, :] = prev_ref[0]

    ext_ref[SUBLANES:SUBLANES + tt, :] = br_ref[0]
    first = SUBLANES - (CONV_B_WIDTH - 1)
    xr = cb_ref[...]
    for w in range(CONV_B_WIDTH):
        xr = xr + ext_ref[pl.ds(first + w, tt), :] * cw_ref[w:w + 1, :]
    a, u = _lru_gates(xr, wg_ref, bg_ref, lam_ref)

    row = lax.broadcasted_iota(jnp.int32, (tt, c), 0)
    d = 1
    while d < tt:
        if d < SUBLANES:
            keep = row >= d
            a_sh = jnp.where(keep, pltpu.roll(a, d, 0), 1.0)
            u_sh = jnp.where(keep, pltpu.roll(u, d, 0), 0.0)
        else:
            a_sh = jnp.concatenate([jnp.ones((d, c), F32), a[:tt - d]], axis=0)
            u_sh = jnp.concatenate([jnp.zeros((d, c), F32), u[:tt - d]], axis=0)
        u = a * u_sh + u
        a = a * a_sh
        d *= 2
    h = a * h_ref[...] + u
    yb_ref[0] = (h * gb_ref[0]).astype(yb_ref.dtype)
    h_last = h[tt - 1:tt, :]
    h_ref[...] = h_last
    hl_ref[0] = h_last


def _lru(br, gb, cw, cb, wg, bg, lam):
    b, t, c = br.shape
    tt = SEQ_TILE_LRU
    halo_blocks = tt // SUBLANES
    vec = lambda a: a.reshape(1, -1)
    return pl.pallas_call(
        functools.partial(_lru_body, tt=tt),
        grid=(b, t // tt),
        in_specs=[pl.BlockSpec((1, tt, c), lambda bi, i: (bi, i, 0)),
                  pl.BlockSpec((1, SUBLANES, c), lambda bi, i: (bi, jnp.maximum(i * halo_blocks - 1, 0), 0)),
                  pl.BlockSpec((1, tt, c), lambda bi, i: (bi, i, 0)),
                  pl.BlockSpec((CONV_B_WIDTH, c), lambda bi, i: (0, 0)),
                  pl.BlockSpec((1, c), lambda bi, i: (0, 0)),
                  pl.BlockSpec((c, 2 * c), lambda bi, i: (0, 0)),
                  pl.BlockSpec((1, 2 * c), lambda bi, i: (0, 0)),
                  pl.BlockSpec((1, c), lambda bi, i: (0, 0))],
        out_specs=[pl.BlockSpec((1, tt, c), lambda bi, i: (bi, i, 0)),
                   pl.BlockSpec((1, 1, c), lambda bi, i: (bi, 0, 0))],
        out_shape=[jax.ShapeDtypeStruct((b, t, c), BF16),
                   jax.ShapeDtypeStruct((b, 1, c), F32)],
        scratch_shapes=[pltpu.VMEM((tt + SUBLANES, c), F32), pltpu.VMEM((1, c), F32)],
        compiler_params=_params(("parallel", "arbitrary")),
        name="lru",
    )(br, br, gb, cw, vec(cb), wg, vec(bg), vec(lam))


def _sample_mix_body(u_ref, br_ref, gb_ref, ha_ref, hb_ref, h0_ref,
                     cwa_ref, cba_ref, lg_ref, lb_ref, cwb_ref, cbb_ref, wg_ref, bg_ref, lam_ref,
                     ya_ref, yb_ref, h_ref):
    u = u_ref[...]
    acc = cba_ref[...] + u * cwa_ref[CONV_A_WIDTH - 1:CONV_A_WIDTH, :]
    for w in range(CONV_A_WIDTH - 1):
        acc = acc + ha_ref[w] * cwa_ref[w:w + 1, :]
    ya_ref[...] = _layer_norm_silu(acc, lg_ref[...], lb_ref[...]).astype(ya_ref.dtype)

    br = br_ref[...]
    xr = cbb_ref[...] + br * cwb_ref[CONV_B_WIDTH - 1:CONV_B_WIDTH, :]
    for w in range(CONV_B_WIDTH - 1):
        xr = xr + hb_ref[w] * cwb_ref[w:w + 1, :]
    a, x_in = _lru_gates(xr, wg_ref, bg_ref, lam_ref)
    h = a * h0_ref[...] + x_in
    h_ref[...] = h
    yb_ref[...] = (h * gb_ref[...]).astype(yb_ref.dtype)


def _sample_mix(u, br, gb, hist_a_t, hist_b_t, h0, cwa, cba, lg, lb, cwb, cbb, wg, bg, lam):
    n, c = u.shape
    vec = lambda a: a.reshape(1, -1)
    args = [u, br, gb, hist_a_t, hist_b_t, h0, cwa, vec(cba), vec(lg), vec(lb), cwb, vec(cbb), wg, vec(bg), vec(lam)]
    full = lambda a: pl.BlockSpec(a.shape, lambda i, nd=a.ndim: (0,) * nd)
    return pl.pallas_call(
        _sample_mix_body,
        grid=(1,),
        in_specs=[full(a) for a in args],
        out_specs=[pl.BlockSpec((n, c), lambda i: (0, 0)) for _ in range(3)],
        out_shape=[jax.ShapeDtypeStruct((n, c), BF16), jax.ShapeDtypeStruct((n, c), BF16),
                   jax.ShapeDtypeStruct((n, c), F32)],
        compiler_params=_params(("arbitrary",)),
        name="sample_mix",
    )(*args)


def _diff_lambda(lq1_ref, lk1_ref, lq2_ref, lk2_ref, lam_init):
    e1 = jnp.exp(jnp.sum(lq1_ref[...] * lk1_ref[...], axis=-1, keepdims=True))
    e2 = jnp.exp(jnp.sum(lq2_ref[...] * lk2_ref[...], axis=-1, keepdims=True))
    return e1 - e2 + lam_init


Q_SCALE = ATT_HD ** -0.5 * math.log2(math.e)
VT_ROWS = ATT_VD + 16


def _qkv_body(x_ref, g_ref, wq_ref, wk_ref, wv_ref, qt_ref, k_ref, v_ref, kb_ref, vt_ref):
    xn = _rms(x_ref[...], g_ref[...]).astype(BF16)
    q = _dot(xn, wq_ref[...]) * Q_SCALE
    qt_ref[0] = q.T.astype(BF16)
    k = _dot(xn, wk_ref[...])
    _store_heads(k_ref, k, ATT_HEADS)
    kb_ref[...] = k.astype(BF16)
    v = _dot(xn, wv_ref[...])
    _store_heads(v_ref, v, ATT_HEADS)
    vt = v.T.astype(BF16)
    ones = jnp.ones((VT_ROWS - ATT_VD, vt.shape[1]), BF16)
    for h in range(ATT_HEADS):
        vt_ref[0, h * VT_ROWS:h * VT_ROWS + ATT_VD, :] = vt[h * ATT_VD:(h + 1) * ATT_VD, :]
        vt_ref[0, h * VT_ROWS + ATT_VD:(h + 1) * VT_ROWS, :] = ones


def _qkv(x, g, w, bsz, seq):
    m, d = x.shape
    tm = ROW_TILE
    per_seq = seq // tm
    row = pl.BlockSpec((tm, d), lambda i: (i, 0))
    heads = pl.BlockSpec((tm, ATT_HEADS, ATT_VD), lambda i: (i, 0, 0))
    col = lambda rows: pl.BlockSpec((1, rows, tm), lambda i: (i // per_seq, 0, i % per_seq))
    wspec = lambda b: pl.BlockSpec((d, d), lambda i: (0, b))
    return pl.pallas_call(
        _qkv_body,
        grid=(m // tm,),
        in_specs=[row, pl.BlockSpec((1, d), lambda i: (0, 0)), wspec(0), wspec(1), wspec(2)],
        out_specs=[col(d), heads, heads, row, col(ATT_HEADS * VT_ROWS)],
        out_shape=[jax.ShapeDtypeStruct((bsz, d, seq), BF16),
                   jax.ShapeDtypeStruct((m, ATT_HEADS, ATT_VD), F32),
                   jax.ShapeDtypeStruct((m, ATT_HEADS, ATT_VD), F32),
                   jax.ShapeDtypeStruct((m, d), BF16),
                   jax.ShapeDtypeStruct((bsz, ATT_HEADS * VT_ROWS, seq), BF16)],
        compiler_params=_params(("parallel",)),
        name="qkv",
    )(x, g.reshape(1, d), w, w, w)


def _flash_body(qi_ref, ki_ref, qt_ref, k_ref, vt_ref, lq1_ref, lk1_ref, lq2_ref, lk2_ref, g_ref, o_ref,
                qs_ref, m_ref, acc_ref, s_ref, *, tq, lam_init):
    step_id = pl.program_id(1)
    qi = qi_ref[step_id]
    ki = ki_ref[step_id]
    vd = ATT_VD

    @pl.when(ki == 0)
    def _():
        dim = lax.broadcasted_iota(jnp.int32, (vd, tq), 0)
        for h in range(ATT_HEADS):
            qh = qt_ref[0, h * vd:(h + 1) * vd, :]
            zero = jnp.zeros_like(qh)
            qs_ref[h, :, 0:tq] = jnp.where(dim < ATT_HD, qh, zero)
            qs_ref[h, :, tq:2 * tq] = jnp.where(dim >= ATT_HD, qh, zero)
        m_ref[...] = jnp.full(m_ref.shape, NEG_INF, F32)
        acc_ref[...] = jnp.zeros(acc_ref.shape, F32)

    def scores(h):
        s_ref[h % 2] = _dot(k_ref[0, :, h * vd:(h + 1) * vd], qs_ref[h])

    def step(diagonal):
        if diagonal:
            key = lax.broadcasted_iota(jnp.int32, (tq, 2 * tq), 0)
            qry = lax.broadcasted_iota(jnp.int32, (tq, 2 * tq), 1)
            visible = key <= jnp.where(qry >= tq, qry - tq, qry)
        scores(0)
        for h in range(ATT_HEADS):
            if h + 1 < ATT_HEADS:
                scores(h + 1)
            s = s_ref[h % 2]
            if diagonal:
                s = jnp.where(visible, s, NEG_INF)
            m_prev = m_ref[h]
            m_new = jnp.maximum(m_prev, jnp.max(s, axis=0, keepdims=True))
            alpha = jnp.exp2(m_prev - m_new)
            p = jnp.exp2(s - m_new).astype(BF16)
            vth = vt_ref[0, h * VT_ROWS:(h + 1) * VT_ROWS, :]
            acc_ref[h] = alpha * acc_ref[h] + _dot(vth, p)
            m_ref[h] = m_new

    @pl.when(ki < qi)
    def _():
        step(False)

    @pl.when(ki == qi)
    def _():
        step(True)
        lam = _diff_lambda(lq1_ref, lk1_ref, lq2_ref, lk2_ref, lam_init)
        g = g_ref[...]
        for h in range(ATT_HEADS):
            o = acc_ref[h, 0:vd, :] / acc_ref[h, vd:vd + 1, :]
            att = o[:, 0:tq] - lam * o[:, tq:2 * tq]
            ms = jnp.mean(att * att, axis=0, keepdims=True)
            att = att * lax.rsqrt(ms + EPS) * g * (1.0 - lam_init)
            o_ref[0, :, h * vd:(h + 1) * vd] = att.T.astype(o_ref.dtype)


def _flash(qt, k, vt, lam_p, g, lam_init):
    b, d, t = qt.shape
    tq = ATT_TQ
    nq = t // tq
    pairs = [(qi, ki) for qi in range(nq) for ki in range(qi + 1)]
    qi_tab = jnp.asarray([p[0] for p in pairs], jnp.int32)
    ki_tab = jnp.asarray([p[1] for p in pairs], jnp.int32)
    vec = lambda a: a.reshape(1, -1)
    small = lambda n: pl.BlockSpec((1, n), lambda bi, s, qi, ki: (0, 0))
    grid_spec = pltpu.PrefetchScalarGridSpec(
        num_scalar_prefetch=2,
        grid=(b, len(pairs)),
        in_specs=[pl.BlockSpec((1, d, tq), lambda bi, s, qi, ki: (bi, 0, qi[s])),
                  pl.BlockSpec((1, tq, d), lambda bi, s, qi, ki: (bi, ki[s], 0)),
                  pl.BlockSpec((1, ATT_HEADS * VT_ROWS, tq), lambda bi, s, qi, ki: (bi, 0, ki[s])),
                  small(ATT_HD), small(ATT_HD), small(ATT_HD), small(ATT_HD),
                  pl.BlockSpec((ATT_VD, 1), lambda bi, s, qi, ki: (0, 0))],
        out_specs=pl.BlockSpec((1, tq, d), lambda bi, s, qi, ki: (bi, qi[s], 0)),
        scratch_shapes=[pltpu.VMEM((ATT_HEADS, ATT_VD, 2 * tq), BF16),
                        pltpu.VMEM((ATT_HEADS, 1, 2 * tq), F32),
                        pltpu.VMEM((ATT_HEADS, VT_ROWS, 2 * tq), F32),
                        pltpu.VMEM((2, tq, 2 * tq), F32)],
    )
    return pl.pallas_call(
        functools.partial(_flash_body, tq=tq, lam_init=lam_init),
        grid_spec=grid_spec,
        out_shape=jax.ShapeDtypeStruct((b, t, d), BF16),
        compiler_params=_params(("parallel", "arbitrary")),
        name="flash_diff_attn",
    )(qi_tab, ki_tab, qt, k, vt, *[vec(a) for a in lam_p], g.reshape(ATT_VD, 1))


def _decode_body(pt_ref, q_ref, kn_ref, vn_ref, lq1_ref, lk1_ref, lq2_ref, lk2_ref, g_ref, *rest,
                 n_pages, lam_init):
    k_refs = rest[:n_pages]
    v_refs = rest[n_pages:2 * n_pages]
    o_ref, qx_ref, m_ref, l_ref, acc_ref = rest[2 * n_pages:]
    j = pl.program_id(1)
    nh = ATT_HEADS
    rows = PAGE_SIZE * nh

    @pl.when(j == 0)
    def _():
        q8 = q_ref[0].astype(F32)
        lane = lax.broadcasted_iota(jnp.int32, (nh, ATT_VD), 1)
        qx = jnp.concatenate([jnp.where(lane < ATT_HD, q8, 0.0), jnp.where(lane >= ATT_HD, q8, 0.0)], axis=0)
        qx_ref[...] = qx.astype(BF16)
        kn = kn_ref[0].astype(BF16).astype(F32)
        vn = vn_ref[0].astype(BF16).astype(F32)
        m_ref[...] = jnp.sum(qx * jnp.concatenate([kn, kn], axis=0), axis=-1, keepdims=True)
        l_ref[...] = jnp.ones(l_ref.shape, F32)
        acc_ref[...] = jnp.concatenate([vn, vn], axis=0)

    qx = qx_ref[...]
    s = jnp.concatenate([_dot_nt(qx, k_ref[0].reshape(rows, ATT_VD).astype(BF16)) for k_ref in k_refs], axis=1)
    r = lax.broadcasted_iota(jnp.int32, s.shape, 0)
    c = lax.broadcasted_iota(jnp.int32, s.shape, 1)
    s = jnp.where(c % nh == r % nh, s, NEG_INF)
    m_prev = m_ref[...]
    m_new = jnp.maximum(m_prev, jnp.max(s, axis=-1, keepdims=True))
    alpha = jnp.exp2(m_prev - m_new)
    p = jnp.exp2(s - m_new)
    l_ref[...] = alpha * l_ref[...] + jnp.sum(p, axis=-1, keepdims=True)
    pb = p.astype(BF16)
    acc = alpha * acc_ref[...]
    for i, v_ref in enumerate(v_refs):
        acc = acc + _dot(pb[:, i * rows:(i + 1) * rows], v_ref[0].reshape(rows, ATT_VD).astype(BF16))
    acc_ref[...] = acc
    m_ref[...] = m_new

    @pl.when(j == pl.num_programs(1) - 1)
    def _():
        lam = _diff_lambda(lq1_ref, lk1_ref, lq2_ref, lk2_ref, lam_init)
        o = acc_ref[...] / l_ref[...]
        att = o[0:nh] - lam * o[nh:2 * nh]
        o_ref[0] = (_rms(att, g_ref[...]) * (1.0 - lam_init)).astype(o_ref.dtype)


def _decode(q, k_new, v_new, cache_k, cache_v, page_base, page_table, lam_p, g, lam_init):
    b = q.shape[0]
    nh, vd = ATT_HEADS, ATT_VD
    n_pages_total = page_table.shape[1]
    pp = DEC_PAGES
    vec = lambda a: a.reshape(1, -1)
    row_spec = pl.BlockSpec((1, nh, vd), lambda bi, j, pt: (bi, 0, 0))
    small = lambda n: pl.BlockSpec((1, n), lambda bi, j, pt: (0, 0))
    page_specs = [pl.BlockSpec((1, PAGE_SIZE, nh, vd),
                               lambda bi, j, pt, i=i: (page_base + pt[bi, j * pp + i], 0, 0, 0))
                  for i in range(pp)]
    grid_spec = pltpu.PrefetchScalarGridSpec(
        num_scalar_prefetch=1,
        grid=(b, n_pages_total // pp),
        in_specs=[row_spec, row_spec, row_spec,
                  small(ATT_HD), small(ATT_HD), small(ATT_HD), small(ATT_HD), small(vd)]
                 + page_specs + page_specs,
        out_specs=pl.BlockSpec((1, nh, vd), lambda bi, j, pt: (bi, 0, 0)),
        scratch_shapes=[pltpu.VMEM((2 * nh, vd), BF16),
                        pltpu.VMEM((2 * nh, 1), F32),
                        pltpu.VMEM((2 * nh, 1), F32),
                        pltpu.VMEM((2 * nh, vd), F32)],
    )
    return pl.pallas_call(
        functools.partial(_decode_body, n_pages=pp, lam_init=lam_init),
        grid_spec=grid_spec,
        out_shape=jax.ShapeDtypeStruct((b, nh, vd), BF16),
        compiler_params=_params(("parallel", "arbitrary")),
        name="paged_diff_attn",
    )(page_table, q, k_new, v_new, *[vec(a) for a in lam_p], vec(g),
      *([cache_k] * pp), *([cache_v] * pp))


def _xattn_head(q, k, v):
    s = _dot_nt(q, k)
    p = jnp.exp(s - jnp.max(s, axis=-1, keepdims=True))
    p = p / jnp.sum(p, axis=-1, keepdims=True)
    return _dot(p.astype(BF16), v)


def _xattn_body(q_ref, k_ref, v_ref, o_ref):
    hd = XATT_HD
    for h in range(XATT_HEADS):
        cols = slice(h * hd, (h + 1) * hd)
        o_ref[0, :, cols] = _xattn_head(q_ref[0, :, cols], k_ref[0, :, cols], v_ref[0, :, cols]).astype(o_ref.dtype)


def _xattn(q, mem_k, mem_v):
    b, t, d = q.shape
    tq = ROW_TILE
    kv_spec = pl.BlockSpec((1, N_MEM, d), lambda bi, i: (bi, 0, 0))
    return pl.pallas_call(
        _xattn_body,
        grid=(b, t // tq),
        in_specs=[pl.BlockSpec((1, tq, d), lambda bi, i: (bi, i, 0)), kv_spec, kv_spec],
        out_specs=pl.BlockSpec((1, tq, d), lambda bi, i: (bi, i, 0)),
        out_shape=jax.ShapeDtypeStruct((b, t, d), BF16),
        compiler_params=_params(("parallel", "parallel")),
        name="xattn",
    )(q, mem_k, mem_v)


def _xattn_cached_body(q_ref, k_ref, v_ref, o_ref):
    nh, hd = XATT_HEADS, XATT_HD
    rows = N_MEM * nh
    head = lax.broadcasted_iota(jnp.int32, (nh, rows), 0)
    col = lax.broadcasted_iota(jnp.int32, (nh, rows), 1)
    own = col % nh == head
    for i in range(XS_BATCH):
        q = q_ref[i]
        qx = jnp.concatenate([q[:, h * hd:(h + 1) * hd] for h in range(nh)], axis=0)
        k = k_ref[0, i].reshape(rows, hd).astype(BF16)
        v = v_ref[0, i].reshape(rows, hd).astype(BF16)
        s = jnp.where(own, _dot_nt(qx, k), NEG_INF)
        p = jnp.exp(s - jnp.max(s, axis=-1, keepdims=True))
        p = p / jnp.sum(p, axis=-1, keepdims=True)
        o = _dot(p.astype(BF16), v)
        for h in range(nh):
            o_ref[i, :, h * hd:(h + 1) * hd] = o[h:h + 1, :].astype(o_ref.dtype)


def _xattn_cached(q, cache_k, cache_v, layer):
    b, t, d = q.shape
    bb = XS_BATCH
    kv_spec = pl.BlockSpec((1, bb, N_MEM, XATT_HEADS, XATT_HD), lambda i: (layer, i, 0, 0, 0))
    return pl.pallas_call(
        _xattn_cached_body,
        grid=(b // bb,),
        in_specs=[pl.BlockSpec((bb, t, d), lambda i: (i, 0, 0)), kv_spec, kv_spec],
        out_specs=pl.BlockSpec((bb, t, d), lambda i: (i, 0, 0)),
        out_shape=jax.ShapeDtypeStruct((b, t, d), BF16),
        compiler_params=_params(("parallel",)),
        name="xattn_cached",
    )(q, cache_k, cache_v)


def _memkv_body(x_ref, wk_ref, wv_ref, kn_ref, vn_ref, kb_ref, vb_ref):
    xb = x_ref[...].astype(BF16)
    for w_ref, n_ref, b_ref in ((wk_ref, kn_ref, kb_ref), (wv_ref, vn_ref, vb_ref)):
        y = _dot(xb, w_ref[...])
        b_ref[...] = y.astype(BF16)
        _store_heads(n_ref, y, XATT_HEADS)


def _memkv(x, w):
    m, d = x.shape
    tm = _row_tile(m)
    row = pl.BlockSpec((tm, d), lambda i: (i, 0))
    heads = pl.BlockSpec((tm, XATT_HEADS, XATT_HD), lambda i: (i, 0, 0))
    return pl.pallas_call(
        _memkv_body,
        grid=(m // tm,),
        in_specs=[row, pl.BlockSpec((d, d), lambda i: (0, 0)), pl.BlockSpec((d, d), lambda i: (0, 1))],
        out_specs=[heads, heads, row, row],
        out_shape=[jax.ShapeDtypeStruct((m, XATT_HEADS, XATT_HD), F32),
                   jax.ShapeDtypeStruct((m, XATT_HEADS, XATT_HD), F32),
                   jax.ShapeDtypeStruct((m, d), BF16),
                   jax.ShapeDtypeStruct((m, d), BF16)],
        compiler_params=_params(("parallel",)),
        name="memkv",
    )(x, w, w)


def _block_diag(w):
    nb, n, _ = w.shape
    eye = jnp.eye(nb, dtype=w.dtype)
    return jnp.einsum('kij,kl->kilj', w, eye).reshape(nb * n, nb * n)


def _stack(arrays):
    return arrays[0][None] if len(arrays) == 1 else jnp.stack(arrays)


def _trunk(x3, xattn, p, even_mix, odd_mix):
    b, t, d = x3.shape
    x = x3.reshape(b * t, d)
    depth = p['ffn1_g'].shape[0]
    for l in range(depth):
        x = _ffn(x, p['ffn1_g'][l], p['ffn1_w_in'], p['ffn1_w_out'], l)
        post = (p['xattn_g'][l], p['xattn_w_q'][l], XATT_HD ** -0.5)
        x, q = even_mix(x, l // 2, post) if l % 2 == 0 else odd_mix(x, l // 2, l, post)
        o = xattn(q.reshape(b, t, d), l)
        last = l == depth - 1
        x = _ffn(x, p['ffn2_g'][l], p['ffn2_w_in'], p['ffn2_w_out'], l,
                 final_g=p['final_g'] if last else None, pre=(o.reshape(b * t, d), p['xattn_w_out']))
    return x.reshape(b, t, d)


def kernel(x_prompt, x_sample, state_conv_a, state_conv_b, state_lru, cache_k, cache_v, cache_mem_k, cache_mem_v, page_table, mem_prompt, ffn1_g, ffn1_w_in, ffn1_w_out, mix_g, even_w_in, conv_a_w, conv_a_b, conv_a_ln_g, conv_a_ln_b, conv_b_w, conv_b_b, lru_w_a, lru_b_a, lru_w_x, lru_b_x, lru_lambda, even_w_out, attn_w_in, lam_q1, lam_k1, lam_q2, lam_k2, attn_subln_g, attn_w_out, xattn_g, xattn_w_q, xattn_w_kv, xattn_w_out, ffn2_g, ffn2_w_in, ffn2_w_out, final_g):
    bsz, seq, d = x_prompt.shape
    dbsz, dseq, _ = x_sample.shape
    depth = ffn1_g.shape[0]
    n_even = even_w_in.shape[0]
    n_odd = attn_w_in.shape[0]
    bf = lambda w: w.astype(BF16)
    p = {
        'ffn1_g': ffn1_g, 'ffn1_w_in': bf(ffn1_w_in), 'ffn1_w_out': bf(ffn1_w_out),
        'ffn2_g': ffn2_g, 'ffn2_w_in': bf(ffn2_w_in), 'ffn2_w_out': bf(ffn2_w_out),
        'xattn_g': xattn_g, 'xattn_w_q': bf(xattn_w_q), 'xattn_w_out': bf(xattn_w_out),
        'final_g': final_g,
    }
    even_w_in_b = bf(even_w_in)
    even_w_out_b = bf(even_w_out)
    attn_w_in_b = bf(attn_w_in)
    attn_w_out_b = bf(attn_w_out)
    xattn_w_kv_b = bf(xattn_w_kv)
    lru_wg = [bf(jnp.concatenate([_block_diag(lru_w_a[e]), _block_diag(lru_w_x[e])], axis=1)) for e in range(n_even)]
    lru_bg = [jnp.concatenate([lru_b_a[e], lru_b_x[e]]) for e in range(n_even)]
    c = CONV_CH
    lam_inits = [0.8 - 0.6 * math.exp(-0.3 * (2 * o + 1)) for o in range(n_odd)]

    mem2 = mem_prompt.reshape(bsz * N_MEM, d)
    p_mem_k, p_mem_v, p_mem_kb, p_mem_vb = [], [], [], []
    for l in range(depth):
        mk, mv, mkb, mvb = _memkv(mem2, xattn_w_kv_b[l])
        p_mem_k.append(mk.reshape(bsz, N_MEM, XATT_HEADS, XATT_HD))
        p_mem_v.append(mv.reshape(bsz, N_MEM, XATT_HEADS, XATT_HD))
        p_mem_kb.append(mkb.reshape(bsz, N_MEM, d))
        p_mem_vb.append(mvb.reshape(bsz, N_MEM, d))

    p_conv_a, p_conv_b, p_lru, p_k, p_v = [], [], [], [], []

    def p_even(x, e, post):
        u, br, gb = _even_in(x, mix_g[2 * e], even_w_in_b[e])
        u3, br3, gb3 = (a.reshape(bsz, seq, c) for a in (u, br, gb))
        ya = _conv_a(u3, conv_a_w[e], conv_a_b[e], conv_a_ln_g[e], conv_a_ln_b[e])
        yb, h_last = _lru(br3, gb3, conv_b_w[e], conv_b_b[e], lru_wg[e], lru_bg[e], lru_lambda[e])
        p_conv_a.append(u3[:, seq - (CONV_A_WIDTH - 1):])
        p_conv_b.append(br3[:, seq - (CONV_B_WIDTH - 1):])
        p_lru.append(h_last.reshape(bsz, c))
        return _outproj(x, [ya.reshape(-1, c), yb.reshape(-1, c)], [even_w_out_b[e][:c], even_w_out_b[e][c:]], post)

    def p_odd(x, o, l, post):
        qt, k, v, kb, vt = _qkv(x, mix_g[l], attn_w_in_b[o], bsz, seq)
        p_k.append(k.reshape(bsz, seq, ATT_HEADS, ATT_VD))
        p_v.append(v.reshape(bsz, seq, ATT_HEADS, ATT_VD))
        att = _flash(qt, kb.reshape(bsz, seq, d), vt,
                     (lam_q1[o], lam_k1[o], lam_q2[o], lam_k2[o]), attn_subln_g[o], lam_inits[o])
        return _outproj(x, [att.reshape(-1, d)], [attn_w_out_b[o]], post)

    y_prompt = _trunk(x_prompt, lambda q, l: _xattn(q, p_mem_kb[l], p_mem_vb[l]), p, p_even, p_odd)

    s_conv_a, s_conv_b, s_lru, s_k, s_v = [], [], [], [], []
    pool = cache_k.shape[1]
    pages_k = cache_k.reshape(n_odd * pool, PAGE_SIZE, ATT_HEADS, ATT_VD)
    pages_v = cache_v.reshape(n_odd * pool, PAGE_SIZE, ATT_HEADS, ATT_VD)

    def s_even(x, e, post):
        u, br, gb = _even_in(x, mix_g[2 * e], even_w_in_b[e])
        ya, yb, h_new = _sample_mix(
            u, br, gb, jnp.swapaxes(state_conv_a[e], 0, 1), jnp.swapaxes(state_conv_b[e], 0, 1), state_lru[e],
            conv_a_w[e], conv_a_b[e], conv_a_ln_g[e], conv_a_ln_b[e],
            conv_b_w[e], conv_b_b[e], lru_wg[e], lru_bg[e], lru_lambda[e])
        s_conv_a.append(jnp.concatenate([state_conv_a[e][:, 1:], u[:, None, :]], axis=1))
        s_conv_b.append(jnp.concatenate([state_conv_b[e][:, 1:], br[:, None, :]], axis=1))
        s_lru.append(h_new)
        return _outproj(x, [ya, yb], [even_w_out_b[e][:c], even_w_out_b[e][c:]], post)

    def s_odd(x, o, l, post):
        q, k, v = _proj(x, mix_g[l], attn_w_in_b[o], d, [0, 1, 2],
                        [(0, BF16, Q_SCALE), (1, F32, 1.0), (2, F32, 1.0)])
        heads = (dbsz, ATT_HEADS, ATT_VD)
        s_k.append(k.reshape(dbsz, dseq, ATT_HEADS, ATT_VD))
        s_v.append(v.reshape(dbsz, dseq, ATT_HEADS, ATT_VD))
        att = _decode(q.reshape(heads), k.reshape(heads), v.reshape(heads), pages_k, pages_v, o * pool,
                      page_table, (lam_q1[o], lam_k1[o], lam_q2[o], lam_k2[o]), attn_subln_g[o], lam_inits[o])
        return _outproj(x, [att.reshape(dbsz, d)], [attn_w_out_b[o]], post)

    y_sample = _trunk(x_sample, lambda q, l: _xattn_cached(q, cache_mem_k, cache_mem_v, l), p, s_even, s_odd)

    return (y_prompt, y_sample,
            _stack(p_conv_a), _stack(p_conv_b), _stack(p_lru),
            _stack(p_k), _stack(p_v),
            _stack(p_mem_k), _stack(p_mem_v),
            _stack(s_conv_a), _stack(s_conv_b), _stack(s_lru),
            _stack(s_k), _stack(s_v))
```

```python
import functools
import math

import jax
import jax.numpy as jnp
from jax import lax
from jax.experimental import pallas as pl
from jax.experimental.pallas import tpu as pltpu

F32 = jnp.float32
BF16 = jnp.bfloat16

D_MODEL = 1024
D_FF = 2816
CONV_CH = 512
CONV_A_WIDTH = 31
LRU_CH = 512
LRU_BLOCKS = 8
CONV_B_WIDTH = 4
LRU_C = 8.0
ATT_HEADS = 8
ATT_HD = 64
ATT_VD = 128
N_MEM = 256
XATT_HEADS = 4
XATT_HD = 256
PAGE_SIZE = 128
EPS = 1e-6
NEG_INF = -1e30

SUBLANES = 8
VMEM_LIMIT = 56 * 1024 * 1024

MXU_COLS = 256
FFN_CHUNK = 4 * MXU_COLS
XS_BATCH = 4
ROW_TILE = 512
SEQ_TILE_CONV = 512
SEQ_TILE_LRU = 256
ATT_TQ = 512
DEC_PAGES = 16
RIDER_HOSTS = 4


def _params(sem):
    return pltpu.CompilerParams(dimension_semantics=sem, vmem_limit_bytes=VMEM_LIMIT)


def _rms(x, g):
    return x * lax.rsqrt(jnp.mean(x * x, axis=-1, keepdims=True) + EPS) * g


def _dot(a, b):
    return jnp.dot(a, b, preferred_element_type=F32)


def _dot_nt(a, b):
    return lax.dot_general(a, b, (((1,), (1,)), ((), ())), preferred_element_type=F32)


def _row_tile(m):
    return ROW_TILE if m % ROW_TILE == 0 else m


def _store_heads(o_ref, y, n_heads):
    o_ref[...] = y.reshape(y.shape[0], n_heads, y.shape[1] // n_heads)


def _ffn_chunks(f):
    return [(c0, min(FFN_CHUNK, f - c0)) for c0 in range(0, f, FFN_CHUNK)]


def _ffn_body(*refs, final, pre):
    refs = list(refs)
    x_ref, g_ref, wi_ref, wo_ref = refs[:4]
    pos = 4
    if pre:
        y_ref, wp_ref = refs[pos:pos + 2]
        pos += 2
    if final:
        gf_ref = refs[pos]
        pos += 1
    o_ref = refs[pos]
    f = wo_ref.shape[0]
    x = x_ref[...]
    if pre:
        x = x + _dot(y_ref[...], wp_ref[...])
    xn = _rms(x, g_ref[...]).astype(BF16)
    acc = None
    for c0, cw in _ffn_chunks(f):
        gate = _dot(xn, wi_ref[:, c0:c0 + cw])
        up = _dot(xn, wi_ref[:, f + c0:f + c0 + cw])
        h = (gate * jax.nn.sigmoid(gate) * up).astype(BF16)
        part = _dot(h, wo_ref[c0:c0 + cw, :])
        acc = part if acc is None else acc + part
    y = x + 0.5 * acc
    if final:
        y = _rms(y, gf_ref[...])
    o_ref[...] = y


def _ffn(x, g, w_in, w_out, layer, final_g=None, pre=None):
    m, d = x.shape
    f = w_out.shape[1]
    tm = _row_tile(m)
    final = final_g is not None
    const = lambda shape: pl.BlockSpec(shape, lambda i: (0, 0), pipeline_mode=pl.Buffered(1))
    of_layer = lambda w: pl.BlockSpec((None,) + w.shape[1:], lambda i: (layer, 0, 0), pipeline_mode=pl.Buffered(1))
    in_specs = [pl.BlockSpec((tm, d), lambda i: (i, 0)), const((1, d)), of_layer(w_in), of_layer(w_out)]
    args = [x, g.reshape(1, d), w_in, w_out]
    if pre is not None:
        y, wp = pre
        in_specs += [pl.BlockSpec((tm, y.shape[1]), lambda i: (i, 0)), of_layer(wp)]
        args += [y, wp]
    if final:
        in_specs.append(const((1, d)))
        args.append(final_g.reshape(1, d))
    return pl.pallas_call(
        functools.partial(_ffn_body, final=final, pre=pre is not None),
        grid=(m // tm,),
        in_specs=in_specs,
        out_specs=pl.BlockSpec((tm, d), lambda i: (i, 0)),
        out_shape=jax.ShapeDtypeStruct((m, d), F32),
        compiler_params=_params(("parallel",)),
        name="ffn",
    )(*args)


def _proj_body(*refs, has_g, n_w, out_meta):
    x_ref = refs[0]
    pos = 1
    if has_g:
        g_ref = refs[pos]
        pos += 1
    w_refs = refs[pos:pos + n_w]
    o_refs = refs[pos + n_w:]
    x = x_ref[...]
    if has_g:
        x = _rms(x, g_ref[...])
    xb = x.astype(BF16)
    ys = [_dot(xb, w_ref[...]) for w_ref in w_refs]
    for o_ref, (wi, scale) in zip(o_refs, out_meta):
        y = ys[wi]
        if scale != 1.0:
            y = y * scale
        o_ref[...] = y.astype(o_ref.dtype)


def _proj(x, g, w, width, blocks, outs):
    m, k = x.shape
    tm = _row_tile(m)
    has_g = g is not None
    in_specs = [pl.BlockSpec((tm, k), lambda i: (i, 0))]
    args = [x]
    if has_g:
        in_specs.append(pl.BlockSpec((1, k), lambda i: (0, 0)))
        args.append(g.reshape(1, k))
    for b in blocks:
        in_specs.append(pl.BlockSpec((k, width), lambda i, b=b: (0, b)))
        args.append(w)
    return pl.pallas_call(
        functools.partial(_proj_body, has_g=has_g, n_w=len(blocks),
                          out_meta=tuple((wi, sc) for wi, _, sc in outs)),
        grid=(m // tm,),
        in_specs=in_specs,
        out_specs=[pl.BlockSpec((tm, width), lambda i: (i, 0)) for _ in outs],
        out_shape=[jax.ShapeDtypeStruct((m, width), dt) for _, dt, _ in outs],
        compiler_params=_params(("parallel",)),
        name="proj",
    )(*args)


def _outproj_body(*refs, n, post_scale):
    x_ref = refs[0]
    y_refs = refs[1:1 + n]
    w_refs = refs[1 + n:1 + 2 * n]
    pos = 1 + 2 * n
    if post_scale is not None:
        g_ref, wq_ref = refs[pos:pos + 2]
        pos += 2
    o_ref = refs[pos]
    acc = x_ref[...]
    for y_ref, w_ref in zip(y_refs, w_refs):
        acc = acc + _dot(y_ref[...], w_ref[...])
    o_ref[...] = acc
    if post_scale is not None:
        q_ref = refs[pos + 1]
        q = _dot(_rms(acc, g_ref[...]).astype(BF16), wq_ref[...]) * post_scale
        q_ref[...] = q.astype(q_ref.dtype)


def _outproj(x, ys, ws, post=None, rider=None):
    m, d = x.shape
    tm = _row_tile(m)
    n = len(ys)
    row = pl.BlockSpec((tm, d), lambda i: (i, 0))
    in_specs = [row]
    in_specs += [pl.BlockSpec((tm, y.shape[1]), lambda i: (i, 0)) for y in ys]
    in_specs += [pl.BlockSpec(w.shape, lambda i: (0, 0)) for w in ws]
    args = [x, *ys, *ws]
    out_specs = [row]
    out_shape = [jax.ShapeDtypeStruct((m, d), F32)]
    post_scale = None
    if post is not None:
        g, wq, post_scale = post
        in_specs += [pl.BlockSpec((1, d), lambda i: (0, 0)), pl.BlockSpec(wq.shape, lambda i: (0, 0))]
        args += [g.reshape(1, d), wq]
        out_specs.append(pl.BlockSpec((tm, wq.shape[1]), lambda i: (i, 0)))
        out_shape.append(jax.ShapeDtypeStruct((m, wq.shape[1]), BF16))
    return _call_with_rider(functools.partial(_outproj_body, n=n, post_scale=post_scale), (m // tm,),
                            in_specs, args, out_specs, out_shape, "outproj", rider)


def _even_in_body(x_ref, g_ref, w_ref, u_ref, br_ref, gb_ref):
    xn = _rms(x_ref[...], g_ref[...]).astype(BF16)
    z = _dot(xn, w_ref[...])
    c = CONV_CH
    u_ref[...] = z[:, :c] * jax.nn.sigmoid(z[:, c:2 * c])
    br_ref[...] = z[:, 2 * c:2 * c + LRU_CH]
    gb_ref[...] = jax.nn.gelu(z[:, 2 * c + LRU_CH:])


def _even_in(x, g, w):
    m, d = x.shape
    tm = _row_tile(m)
    n = w.shape[1]
    return pl.pallas_call(
        _even_in_body,
        grid=(m // tm,),
        in_specs=[pl.BlockSpec((tm, d), lambda i: (i, 0)),
                  pl.BlockSpec((1, d), lambda i: (0, 0)),
                  pl.BlockSpec((d, n), lambda i: (0, 0))],
        out_specs=[pl.BlockSpec((tm, CONV_CH), lambda i: (i, 0)) for _ in range(3)],
        out_shape=[jax.ShapeDtypeStruct((m, CONV_CH), F32) for _ in range(3)],
        compiler_params=_params(("parallel",)),
        name="even_in",
    )(x, g.reshape(1, d), w)


def _layer_norm_silu(y, g, b):
    mu = jnp.mean(y, axis=-1, keepdims=True)
    yc = y - mu
    var = jnp.mean(yc * yc, axis=-1, keepdims=True)
    yn = yc * lax.rsqrt(var + EPS) * g + b
    return yn * jax.nn.sigmoid(yn)


CONV_HALO = 32
CONV_CHUNK = 32


def _conv_a_body(u_ref, prev_ref, cw_ref, cb_ref, lg_ref, lb_ref, o_ref, ext_ref, rot_ref, *, tt):
    i = pl.program_id(1)
    c = CONV_CH
    n_ext = tt + CONV_HALO

    @pl.when(i == 0)
    def _():
        ext_ref[0:CONV_HALO, :] = jnp.zeros((CONV_HALO, c), F32)

    @pl.when(i != 0)
    def _():
        ext_ref[0:CONV_HALO, :] = prev_ref[0]

    ext_ref[CONV_HALO:n_ext, :] = u_ref[0]
    ext_ref[n_ext:n_ext + SUBLANES, :] = jnp.zeros((SUBLANES, c), F32)
    for r in range(SUBLANES):
        rot_ref[r] = ext_ref[pl.ds(r, n_ext), :]
    first = CONV_HALO - (CONV_A_WIDTH - 1)
    cb = cb_ref[...]
    lg = lg_ref[...]
    lb = lb_ref[...]
    for r0 in range(0, tt, CONV_CHUNK):
        acc = jnp.zeros((CONV_CHUNK // SUBLANES, SUBLANES, c), F32)
        for w in range(CONV_A_WIDTH):
            s = first + w
            tap = rot_ref[s % SUBLANES, pl.ds(r0 + (s // SUBLANES) * SUBLANES, CONV_CHUNK), :]
            acc = acc + tap.reshape(acc.shape) * cw_ref[w * SUBLANES:(w + 1) * SUBLANES, :][None]
        y = acc.reshape(CONV_CHUNK, c) + cb
        o_ref[0, r0:r0 + CONV_CHUNK, :] = _layer_norm_silu(y, lg, lb).astype(o_ref.dtype)


def _conv_a(u, cw, cb, lg, lb):
    b, t, c = u.shape
    tt = SEQ_TILE_CONV
    halo_blocks = tt // CONV_HALO
    vec = lambda a: a.reshape(1, c)
    return pl.pallas_call(
        functools.partial(_conv_a_body, tt=tt),
        grid=(b, t // tt),
        in_specs=[pl.BlockSpec((1, tt, c), lambda bi, i: (bi, i, 0)),
                  pl.BlockSpec((1, CONV_HALO, c), lambda bi, i: (bi, jnp.maximum(i * halo_blocks - 1, 0), 0)),
                  pl.BlockSpec((CONV_A_WIDTH * SUBLANES, c), lambda bi, i: (0, 0)),
                  pl.BlockSpec((1, c), lambda bi, i: (0, 0)),
                  pl.BlockSpec((1, c), lambda bi, i: (0, 0)),
                  pl.BlockSpec((1, c), lambda bi, i: (0, 0))],
        out_specs=pl.BlockSpec((1, tt, c), lambda bi, i: (bi, i, 0)),
        out_shape=jax.ShapeDtypeStruct((b, t, c), BF16),
        scratch_shapes=[pltpu.VMEM((tt + CONV_HALO + SUBLANES, c), F32),
                        pltpu.VMEM((SUBLANES, tt + CONV_HALO, c), F32)],
        compiler_params=_params(("parallel", "arbitrary")),
        name="conv_a",
    )(u, u, jnp.repeat(cw, SUBLANES, axis=0), vec(cb), vec(lg), vec(lb))


def _lru_gates(xr, wg_ref, bg_ref, lam_ref):
    gates = _dot(xr.astype(BF16), wg_ref[...]) + bg_ref[...]
    r = jax.nn.sigmoid(gates[:, :LRU_CH])
    ig = jax.nn.sigmoid(gates[:, LRU_CH:])
    nl = -lam_ref[...]
    softplus = jnp.maximum(nl, 0.0) + jnp.log1p(jnp.exp(-jnp.abs(nl)))
    log_a = -LRU_C * r * softplus
    a = jnp.exp(log_a)
    beta = jnp.sqrt(jnp.maximum(-jnp.tanh(log_a) * (a * a + 1.0), 0.0))
    return a, beta * ig * xr


def _lru_body(br_ref, prev_ref, gb_ref, cw_ref, cb_ref, wg_ref, bg_ref, lam_ref,
              yb_ref, hl_ref, ext_ref, h_ref, *, tt):
    i = pl.program_id(1)
    c = LRU_CH

    @pl.when(i == 0)
    def _():
        h_ref[...] = jnp.zeros((1, c), F32)
        ext_ref[0:SUBLANES, :] = jnp.zeros((SUBLANES, c), F32)

    @pl.when(i != 0)
    def _():
        ext_ref[0:SUBLANES, :] = prev_ref[0]

    ext_ref[SUBLANES:SUBLANES + tt, :] = br_ref[0]
    first = SUBLANES - (CONV_B_WIDTH - 1)
    xr = cb_ref[...]
    for w in range(CONV_B_WIDTH):
        xr = xr + ext_ref[pl.ds(first + w, tt), :] * cw_ref[w:w + 1, :]
    a, u = _lru_gates(xr, wg_ref, bg_ref, lam_ref)

    row = lax.broadcasted_iota(jnp.int32, (tt, c), 0)
    d = 1
    while d < tt:
        if d < SUBLANES:
            keep = row >= d
            a_sh = jnp.where(keep, pltpu.roll(a, d, 0), 1.0)
            u_sh = jnp.where(keep, pltpu.roll(u, d, 0), 0.0)
        else:
            a_sh = jnp.concatenate([jnp.ones((d, c), F32), a[:tt - d]], axis=0)
            u_sh = jnp.concatenate([jnp.zeros((d, c), F32), u[:tt - d]], axis=0)
        u = a * u_sh + u
        a = a * a_sh
        d *= 2
    h = a * h_ref[...] + u
    yb_ref[0] = (h * gb_ref[0]).astype(yb_ref.dtype)
    h_last = h[tt - 1:tt, :]
    h_ref[...] = h_last
    hl_ref[0] = h_last


def _lru(br, gb, cw, cb, wg, bg, lam):
    b, t, c = br.shape
    tt = SEQ_TILE_LRU
    halo_blocks = tt // SUBLANES
    vec = lambda a: a.reshape(1, -1)
    return pl.pallas_call(
        functools.partial(_lru_body, tt=tt),
        grid=(b, t // tt),
        in_specs=[pl.BlockSpec((1, tt, c), lambda bi, i: (bi, i, 0)),
                  pl.BlockSpec((1, SUBLANES, c), lambda bi, i: (bi, jnp.maximum(i * halo_blocks - 1, 0), 0)),
                  pl.BlockSpec((1, tt, c), lambda bi, i: (bi, i, 0)),
                  pl.BlockSpec((CONV_B_WIDTH, c), lambda bi, i: (0, 0)),
                  pl.BlockSpec((1, c), lambda bi, i: (0, 0)),
                  pl.BlockSpec((c, 2 * c), lambda bi, i: (0, 0)),
                  pl.BlockSpec((1, 2 * c), lambda bi, i: (0, 0)),
                  pl.BlockSpec((1, c), lambda bi, i: (0, 0))],
        out_specs=[pl.BlockSpec((1, tt, c), lambda bi, i: (bi, i, 0)),
                   pl.BlockSpec((1, 1, c), lambda bi, i: (bi, 0, 0))],
        out_shape=[jax.ShapeDtypeStruct((b, t, c), BF16),
                   jax.ShapeDtypeStruct((b, 1, c), F32)],
        scratch_shapes=[pltpu.VMEM((tt + SUBLANES, c), F32), pltpu.VMEM((1, c), F32)],
        compiler_params=_params(("parallel", "arbitrary")),
        name="lru",
    )(br, br, gb, cw, vec(cb), wg, vec(bg), vec(lam))


def _sample_mix_body(u_ref, br_ref, gb_ref, ha_ref, hb_ref, h0_ref,
                     cwa_ref, cba_ref, lg_ref, lb_ref, cwb_ref, cbb_ref, wg_ref, bg_ref, lam_ref,
                     ya_ref, yb_ref, h_ref):
    u = u_ref[...]
    acc = cba_ref[...] + u * cwa_ref[CONV_A_WIDTH - 1:CONV_A_WIDTH, :]
    for w in range(CONV_A_WIDTH - 1):
        acc = acc + ha_ref[w] * cwa_ref[w:w + 1, :]
    ya_ref[...] = _layer_norm_silu(acc, lg_ref[...], lb_ref[...]).astype(ya_ref.dtype)

    br = br_ref[...]
    xr = cbb_ref[...] + br * cwb_ref[CONV_B_WIDTH - 1:CONV_B_WIDTH, :]
    for w in range(CONV_B_WIDTH - 1):
        xr = xr + hb_ref[w] * cwb_ref[w:w + 1, :]
    a, x_in = _lru_gates(xr, wg_ref, bg_ref, lam_ref)
    h = a * h0_ref[...] + x_in
    h_ref[...] = h
    yb_ref[...] = (h * gb_ref[...]).astype(yb_ref.dtype)


def _sample_mix(u, br, gb, hist_a_t, hist_b_t, h0, cwa, cba, lg, lb, cwb, cbb, wg, bg, lam):
    n, c = u.shape
    vec = lambda a: a.reshape(1, -1)
    args = [u, br, gb, hist_a_t, hist_b_t, h0, cwa, vec(cba), vec(lg), vec(lb), cwb, vec(cbb), wg, vec(bg), vec(lam)]
    full = lambda a: pl.BlockSpec(a.shape, lambda i, nd=a.ndim: (0,) * nd)
    return pl.pallas_call(
        _sample_mix_body,
        grid=(1,),
        in_specs=[full(a) for a in args],
        out_specs=[pl.BlockSpec((n, c), lambda i: (0, 0)) for _ in range(3)],
        out_shape=[jax.ShapeDtypeStruct((n, c), BF16), jax.ShapeDtypeStruct((n, c), BF16),
                   jax.ShapeDtypeStruct((n, c), F32)],
        compiler_params=_params(("arbitrary",)),
        name="sample_mix",
    )(*args)


def _diff_lambda(lq1_ref, lk1_ref, lq2_ref, lk2_ref, lam_init):
    e1 = jnp.exp(jnp.sum(lq1_ref[...] * lk1_ref[...], axis=-1, keepdims=True))
    e2 = jnp.exp(jnp.sum(lq2_ref[...] * lk2_ref[...], axis=-1, keepdims=True))
    return e1 - e2 + lam_init


Q_SCALE = ATT_HD ** -0.5 * math.log2(math.e)
VT_ROWS = ATT_VD + 16


def _qkv_body(x_ref, g_ref, wq_ref, wk_ref, wv_ref, qt_ref, k_ref, v_ref, kb_ref, vt_ref):
    xn = _rms(x_ref[...], g_ref[...]).astype(BF16)
    q = _dot(xn, wq_ref[...])
    k = _dot(xn, wk_ref[...])
    v = _dot(xn, wv_ref[...])
    qt_ref[0] = (q * Q_SCALE).T.astype(BF16)
    _store_heads(k_ref, k, ATT_HEADS)
    kb_ref[...] = k.astype(BF16)
    _store_heads(v_ref, v, ATT_HEADS)
    vt = v.T.astype(BF16)
    ones = jnp.ones((VT_ROWS - ATT_VD, vt.shape[1]), BF16)
    for h in range(ATT_HEADS):
        vt_ref[0, h * VT_ROWS:h * VT_ROWS + ATT_VD, :] = vt[h * ATT_VD:(h + 1) * ATT_VD, :]
        vt_ref[0, h * VT_ROWS + ATT_VD:(h + 1) * VT_ROWS, :] = ones


def _qkv(x, g, w, bsz, seq):
    m, d = x.shape
    tm = ROW_TILE
    per_seq = seq // tm
    row = pl.BlockSpec((tm, d), lambda i: (i, 0))
    heads = pl.BlockSpec((tm, ATT_HEADS, ATT_VD), lambda i: (i, 0, 0))
    col = lambda rows: pl.BlockSpec((1, rows, tm), lambda i: (i // per_seq, 0, i % per_seq))
    wspec = lambda b: pl.BlockSpec((d, d), lambda i: (0, b))
    return pl.pallas_call(
        _qkv_body,
        grid=(m // tm,),
        in_specs=[row, pl.BlockSpec((1, d), lambda i: (0, 0)), wspec(0), wspec(1), wspec(2)],
        out_specs=[col(d), heads, heads, row, col(ATT_HEADS * VT_ROWS)],
        out_shape=[jax.ShapeDtypeStruct((bsz, d, seq), BF16),
                   jax.ShapeDtypeStruct((m, ATT_HEADS, ATT_VD), F32),
                   jax.ShapeDtypeStruct((m, ATT_HEADS, ATT_VD), F32),
                   jax.ShapeDtypeStruct((m, d), BF16),
                   jax.ShapeDtypeStruct((bsz, ATT_HEADS * VT_ROWS, seq), BF16)],
        compiler_params=_params(("parallel",)),
        name="qkv",
    )(x, g.reshape(1, d), w, w, w)


def _flash_body(qi_ref, ki_ref, qt_ref, k_ref, vt_ref, lq1_ref, lk1_ref, lq2_ref, lk2_ref, g_ref, o_ref,
                qs_ref, m_ref, acc_ref, s_ref, *, tq, lam_init):
    step_id = pl.program_id(1)
    qi = qi_ref[step_id]
    ki = ki_ref[step_id]
    vd = ATT_VD

    @pl.when(ki == 0)
    def _():
        dim = lax.broadcasted_iota(jnp.int32, (vd, tq), 0)
        for h in range(ATT_HEADS):
            qh = qt_ref[0, h * vd:(h + 1) * vd, :]
            zero = jnp.zeros_like(qh)
            qs_ref[h, :, 0:tq] = jnp.where(dim < ATT_HD, qh, zero)
            qs_ref[h, :, tq:2 * tq] = jnp.where(dim >= ATT_HD, qh, zero)
        m_ref[...] = jnp.full(m_ref.shape, NEG_INF, F32)
        acc_ref[...] = jnp.zeros(acc_ref.shape, F32)

    def scores(h):
        s_ref[h % 2] = _dot(k_ref[0, :, h * vd:(h + 1) * vd], qs_ref[h])

    def step(diagonal):
        if diagonal:
            key = lax.broadcasted_iota(jnp.int32, (tq, 2 * tq), 0)
            qry = lax.broadcasted_iota(jnp.int32, (tq, 2 * tq), 1)
            visible = key <= jnp.where(qry >= tq, qry - tq, qry)
        scores(0)
        for h in range(ATT_HEADS):
            if h + 1 < ATT_HEADS:
                scores(h + 1)
            s = s_ref[h % 2]
            if diagonal:
                s = jnp.where(visible, s, NEG_INF)
            m_prev = m_ref[h]
            m_new = jnp.maximum(m_prev, jnp.max(s, axis=0, keepdims=True))
            alpha = jnp.exp2(m_prev - m_new)
            p = jnp.exp2(s - m_new).astype(BF16)
            vth = vt_ref[0, h * VT_ROWS:(h + 1) * VT_ROWS, :]
            acc_ref[h] = alpha * acc_ref[h] + _dot(vth, p)
            m_ref[h] = m_new

    @pl.when(ki < qi)
    def _():
        step(False)

    @pl.when(ki == qi)
    def _():
        step(True)
        lam = _diff_lambda(lq1_ref, lk1_ref, lq2_ref, lk2_ref, lam_init)
        g = g_ref[...]
        for h in range(ATT_HEADS):
            o = acc_ref[h, 0:vd, :] / acc_ref[h, vd:vd + 1, :]
            att = o[:, 0:tq] - lam * o[:, tq:2 * tq]
            ms = jnp.mean(att * att, axis=0, keepdims=True)
            att = att * lax.rsqrt(ms + EPS) * g * (1.0 - lam_init)
            o_ref[0, :, h * vd:(h + 1) * vd] = att.T.astype(o_ref.dtype)


def _flash(qt, k, vt, lam_p, g, lam_init):
    b, d, t = qt.shape
    tq = ATT_TQ
    nq = t // tq
    pairs = [(qi, ki) for qi in range(nq) for ki in range(qi + 1)]
    qi_tab = jnp.asarray([p[0] for p in pairs], jnp.int32)
    ki_tab = jnp.asarray([p[1] for p in pairs], jnp.int32)
    vec = lambda a: a.reshape(1, -1)
    small = lambda n: pl.BlockSpec((1, n), lambda bi, s, qi, ki: (0, 0))
    grid_spec = pltpu.PrefetchScalarGridSpec(
        num_scalar_prefetch=2,
        grid=(b, len(pairs)),
        in_specs=[pl.BlockSpec((1, d, tq), lambda bi, s, qi, ki: (bi, 0, qi[s])),
                  pl.BlockSpec((1, tq, d), lambda bi, s, qi, ki: (bi, ki[s], 0)),
                  pl.BlockSpec((1, ATT_HEADS * VT_ROWS, tq), lambda bi, s, qi, ki: (bi, 0, ki[s])),
                  small(ATT_HD), small(ATT_HD), small(ATT_HD), small(ATT_HD),
                  pl.BlockSpec((ATT_VD, 1), lambda bi, s, qi, ki: (0, 0))],
        out_specs=pl.BlockSpec((1, tq, d), lambda bi, s, qi, ki: (bi, qi[s], 0)),
        scratch_shapes=[pltpu.VMEM((ATT_HEADS, ATT_VD, 2 * tq), BF16),
                        pltpu.VMEM((ATT_HEADS, 1, 2 * tq), F32),
                        pltpu.VMEM((ATT_HEADS, VT_ROWS, 2 * tq), F32),
                        pltpu.VMEM((2, tq, 2 * tq), F32)],
    )
    return pl.pallas_call(
        functools.partial(_flash_body, tq=tq, lam_init=lam_init),
        grid_spec=grid_spec,
        out_shape=jax.ShapeDtypeStruct((b, t, d), BF16),
        compiler_params=_params(("parallel", "arbitrary")),
        name="flash_diff_attn",
    )(qi_tab, ki_tab, qt, k, vt, *[vec(a) for a in lam_p], g.reshape(ATT_VD, 1))


def _decode_step(j, last, refs_in, o_ref, scratch, lam_init):
    q_ref, kn_ref, vn_ref, lq1_ref, lk1_ref, lq2_ref, lk2_ref, g_ref = refs_in[:8]
    k_refs = refs_in[8:8 + DEC_PAGES]
    v_refs = refs_in[8 + DEC_PAGES:8 + 2 * DEC_PAGES]
    qx_ref, m_ref, l_ref, acc_ref = scratch
    nh = ATT_HEADS
    rows = PAGE_SIZE * nh

    @pl.when(j == 0)
    def _():
        q8 = q_ref[0].astype(F32)
        lane = lax.broadcasted_iota(jnp.int32, (nh, ATT_VD), 1)
        qx = jnp.concatenate([jnp.where(lane < ATT_HD, q8, 0.0), jnp.where(lane >= ATT_HD, q8, 0.0)], axis=0)
        qx_ref[...] = qx.astype(BF16)
        kn = kn_ref[0].astype(BF16).astype(F32)
        vn = vn_ref[0].astype(BF16).astype(F32)
        m_ref[...] = jnp.sum(qx * jnp.concatenate([kn, kn], axis=0), axis=-1, keepdims=True)
        l_ref[...] = jnp.ones(l_ref.shape, F32)
        acc_ref[...] = jnp.concatenate([vn, vn], axis=0)

    qx = qx_ref[...]
    s = jnp.concatenate([_dot_nt(qx, k_ref[0].reshape(rows, ATT_VD).astype(BF16)) for k_ref in k_refs], axis=1)
    r = lax.broadcasted_iota(jnp.int32, s.shape, 0)
    c = lax.broadcasted_iota(jnp.int32, s.shape, 1)
    s = jnp.where(c % nh == r % nh, s, NEG_INF)
    m_prev = m_ref[...]
    m_new = jnp.maximum(m_prev, jnp.max(s, axis=-1, keepdims=True))
    alpha = jnp.exp2(m_prev - m_new)
    p = jnp.exp2(s - m_new)
    l_ref[...] = alpha * l_ref[...] + jnp.sum(p, axis=-1, keepdims=True)
    pb = p.astype(BF16)
    acc = alpha * acc_ref[...]
    for i, v_ref in enumerate(v_refs):
        acc = acc + _dot(pb[:, i * rows:(i + 1) * rows], v_ref[0].reshape(rows, ATT_VD).astype(BF16))
    acc_ref[...] = acc
    m_ref[...] = m_new

    @pl.when(j == last)
    def _():
        lam = _diff_lambda(lq1_ref, lk1_ref, lq2_ref, lk2_ref, lam_init)
        o = acc_ref[...] / l_ref[...]
        att = o[0:nh] - lam * o[nh:2 * nh]
        o_ref[0] = (_rms(att, g_ref[...]) * (1.0 - lam_init)).astype(o_ref.dtype)


def _linear_step(idx, grid):
    step = idx[0]
    for i, n in zip(idx[1:], grid[1:]):
        step = step * n + i
    return step


def _call_with_rider(host_body, grid, in_specs, args, out_specs, out_shape, name, rider):
    if rider is None:
        return pl.pallas_call(host_body, grid=grid, in_specs=in_specs, out_specs=out_specs, out_shape=out_shape,
                              compiler_params=_params(("parallel",) * len(grid)), name=name)(*args)
    nh, vd, pp = ATT_HEADS, ATT_VD, DEC_PAGES
    page_table = rider['page_table']
    per_seq = page_table.shape[1] // pp
    n_seq, seq0, base = rider['n_seq'], rider['seq0'], rider['page_base']
    assert math.prod(grid) == n_seq * per_seq
    n_in, n_out = len(in_specs), len(out_specs)
    host_in = [pl.BlockSpec(sp.block_shape, lambda *a, f=sp.index_map: f(*a[:-1]), pipeline_mode=sp.pipeline_mode)
               for sp in in_specs]
    host_out = [pl.BlockSpec(sp.block_shape, lambda *a, f=sp.index_map: f(*a[:-1])) for sp in out_specs]
    seq = lambda a: _linear_step(a[:-1], grid) // per_seq
    sub = lambda a: _linear_step(a[:-1], grid) % per_seq
    row_spec = pl.BlockSpec((1, nh, vd), lambda *a: (seq0 + seq(a), 0, 0))
    small = lambda n: pl.BlockSpec((1, n), lambda *a: (0, 0))
    page_specs = [pl.BlockSpec((1, PAGE_SIZE, nh, vd),
                               lambda *a, i=i: (base + a[-1][seq0 + seq(a), sub(a) * pp + i], 0, 0, 0))
                  for i in range(pp)]
    rider_in = [row_spec, row_spec, row_spec, small(ATT_HD), small(ATT_HD), small(ATT_HD), small(ATT_HD), small(vd)]
    rider_in += page_specs + page_specs
    vec = lambda a: a.reshape(1, -1)
    rider_args = [rider['q'], rider['k_new'], rider['v_new'], *[vec(a) for a in rider['lam_p']], vec(rider['g']),
                  *([rider['cache_k']] * pp), *([rider['cache_v']] * pp)]

    def body(pt_ref, *refs):
        host_body(*refs[:n_in], *refs[n_in + len(rider_in):n_in + len(rider_in) + n_out])
        step = _linear_step([pl.program_id(i) for i in range(len(grid))], grid)
        _decode_step(step % per_seq, per_seq - 1, refs[n_in:n_in + len(rider_in)],
                     refs[n_in + len(rider_in) + n_out], refs[n_in + len(rider_in) + n_out + 1:], rider['lam_init'])

    grid_spec = pltpu.PrefetchScalarGridSpec(
        num_scalar_prefetch=1,
        grid=grid,
        in_specs=host_in + rider_in,
        out_specs=host_out + [pl.BlockSpec((1, nh, vd), lambda *a: (seq(a), 0, 0))],
        scratch_shapes=[pltpu.VMEM((2 * nh, vd), BF16),
                        pltpu.VMEM((2 * nh, 1), F32),
                        pltpu.VMEM((2 * nh, 1), F32),
                        pltpu.VMEM((2 * nh, vd), F32)],
    )
    return pl.pallas_call(
        body,
        grid_spec=grid_spec,
        out_shape=list(out_shape) + [jax.ShapeDtypeStruct((n_seq, nh, vd), BF16)],
        compiler_params=_params(("arbitrary",) * len(grid)),
        name=name + "_paged",
    )(page_table, *args, *rider_args)


def _xattn_head(q, k, v):
    s = _dot_nt(q, k)
    p = jnp.exp(s - jnp.max(s, axis=-1, keepdims=True))
    p = p / jnp.sum(p, axis=-1, keepdims=True)
    return _dot(p.astype(BF16), v)


def _xattn_body(q_ref, k_ref, v_ref, o_ref):
    hd = XATT_HD
    heads = [slice(h * hd, (h + 1) * hd) for h in range(XATT_HEADS)]
    scores = [_dot_nt(q_ref[0, :, c], k_ref[0, :, c]) for c in heads]
    for c, s in zip(heads, scores):
        p = jnp.exp(s - jnp.max(s, axis=-1, keepdims=True))
        l = jnp.sum(p, axis=-1, keepdims=True)
        o_ref[0, :, c] = (_dot(p.astype(BF16), v_ref[0, :, c]) / l).astype(o_ref.dtype)


def _xattn(q, mem_k, mem_v, rider=None):
    b, t, d = q.shape
    tq = ROW_TILE
    kv_spec = pl.BlockSpec((1, N_MEM, d), lambda bi, i: (bi, 0, 0))
    q_spec = pl.BlockSpec((1, tq, d), lambda bi, i: (bi, i, 0))
    return _call_with_rider(_xattn_body, (b, t // tq), [q_spec, kv_spec, kv_spec], [q, mem_k, mem_v],
                            [q_spec], [jax.ShapeDtypeStruct((b, t, d), BF16)], "xattn", rider)


def _xattn_cached_body(q_ref, k_ref, v_ref, o_ref):
    nh, hd = XATT_HEADS, XATT_HD
    rows = N_MEM * nh
    head = lax.broadcasted_iota(jnp.int32, (nh, rows), 0)
    col = lax.broadcasted_iota(jnp.int32, (nh, rows), 1)
    own = col % nh == head
    for i in range(XS_BATCH):
        q = q_ref[i]
        qx = jnp.concatenate([q[:, h * hd:(h + 1) * hd] for h in range(nh)], axis=0)
        k = k_ref[0, i].reshape(rows, hd).astype(BF16)
        v = v_ref[0, i].reshape(rows, hd).astype(BF16)
        s = jnp.where(own, _dot_nt(qx, k), NEG_INF)
        p = jnp.exp(s - jnp.max(s, axis=-1, keepdims=True))
        p = p / jnp.sum(p, axis=-1, keepdims=True)
        o = _dot(p.astype(BF16), v)
        for h in range(nh):
            o_ref[i, :, h * hd:(h + 1) * hd] = o[h:h + 1, :].astype(o_ref.dtype)


def _xattn_cached(q, cache_k, cache_v, layer):
    b, t, d = q.shape
    bb = XS_BATCH
    kv_spec = pl.BlockSpec((1, bb, N_MEM, XATT_HEADS, XATT_HD), lambda i: (layer, i, 0, 0, 0))
    return pl.pallas_call(
        _xattn_cached_body,
        grid=(b // bb,),
        in_specs=[pl.BlockSpec((bb, t, d), lambda i: (i, 0, 0)), kv_spec, kv_spec],
        out_specs=pl.BlockSpec((bb, t, d), lambda i: (i, 0, 0)),
        out_shape=jax.ShapeDtypeStruct((b, t, d), BF16),
        compiler_params=_params(("parallel",)),
        name="xattn_cached",
    )(q, cache_k, cache_v)


def _memkv_body(x_ref, wk_ref, wv_ref, kn_ref, vn_ref, kb_ref, vb_ref):
    xb = x_ref[...].astype(BF16)
    for w_ref, n_ref, b_ref in ((wk_ref, kn_ref, kb_ref), (wv_ref, vn_ref, vb_ref)):
        y = _dot(xb, w_ref[...])
        b_ref[...] = y.astype(BF16)
        _store_heads(n_ref, y, XATT_HEADS)


def _memkv(x, w):
    m, d = x.shape
    tm = _row_tile(m)
    row = pl.BlockSpec((tm, d), lambda i: (i, 0))
    heads = pl.BlockSpec((tm, XATT_HEADS, XATT_HD), lambda i: (i, 0, 0))
    return pl.pallas_call(
        _memkv_body,
        grid=(m // tm,),
        in_specs=[row, pl.BlockSpec((d, d), lambda i: (0, 0)), pl.BlockSpec((d, d), lambda i: (0, 1))],
        out_specs=[heads, heads, row, row],
        out_shape=[jax.ShapeDtypeStruct((m, XATT_HEADS, XATT_HD), F32),
                   jax.ShapeDtypeStruct((m, XATT_HEADS, XATT_HD), F32),
                   jax.ShapeDtypeStruct((m, d), BF16),
                   jax.ShapeDtypeStruct((m, d), BF16)],
        compiler_params=_params(("parallel",)),
        name="memkv",
    )(x, w, w)


def _block_diag(w):
    nb, n, _ = w.shape
    eye = jnp.eye(nb, dtype=w.dtype)
    return jnp.einsum('kij,kl->kilj', w, eye).reshape(nb * n, nb * n)


def _stack(arrays):
    return arrays[0][None] if len(arrays) == 1 else jnp.stack(arrays)


def _trunk(x, bt, xattn, p, even_mix, odd_mix, layers, mixed=None):
    d = x.shape[1]
    depth = p['ffn1_g'].shape[0]
    for l in layers:
        if mixed is None:
            x = _ffn(x, p['ffn1_g'][l], p['ffn1_w_in'], p['ffn1_w_out'], l)
            post = (p['xattn_g'][l], p['xattn_w_q'][l], XATT_HD ** -0.5)
            x, q = even_mix(x, l // 2, post) if l % 2 == 0 else odd_mix(x, l // 2, l, post)
        else:
            x, q = mixed
            mixed = None
        o = xattn(q.reshape(*bt, d), l)
        last = l == depth - 1
        x = _ffn(x, p['ffn2_g'][l], p['ffn2_w_in'], p['ffn2_w_out'], l,
                 final_g=p['final_g'] if last else None, pre=(o.reshape(-1, d), p['xattn_w_out']))
    return x


def kernel(x_prompt, x_sample, state_conv_a, state_conv_b, state_lru, cache_k, cache_v, cache_mem_k, cache_mem_v, page_table, mem_prompt, ffn1_g, ffn1_w_in, ffn1_w_out, mix_g, even_w_in, conv_a_w, conv_a_b, conv_a_ln_g, conv_a_ln_b, conv_b_w, conv_b_b, lru_w_a, lru_b_a, lru_w_x, lru_b_x, lru_lambda, even_w_out, attn_w_in, lam_q1, lam_k1, lam_q2, lam_k2, attn_subln_g, attn_w_out, xattn_g, xattn_w_q, xattn_w_kv, xattn_w_out, ffn2_g, ffn2_w_in, ffn2_w_out, final_g):
    bsz, seq, d = x_prompt.shape
    dbsz, dseq, _ = x_sample.shape
    depth = ffn1_g.shape[0]
    n_even = even_w_in.shape[0]
    n_odd = attn_w_in.shape[0]
    bf = lambda w: w.astype(BF16)
    p = {
        'ffn1_g': ffn1_g, 'ffn1_w_in': bf(ffn1_w_in), 'ffn1_w_out': bf(ffn1_w_out),
        'ffn2_g': ffn2_g, 'ffn2_w_in': bf(ffn2_w_in), 'ffn2_w_out': bf(ffn2_w_out),
        'xattn_g': xattn_g, 'xattn_w_q': bf(xattn_w_q), 'xattn_w_out': bf(xattn_w_out),
        'final_g': final_g,
    }
    even_w_in_b = bf(even_w_in)
    even_w_out_b = bf(even_w_out)
    attn_w_in_b = bf(attn_w_in)
    attn_w_out_b = bf(attn_w_out)
    xattn_w_kv_b = bf(xattn_w_kv)
    lru_wg = [bf(jnp.concatenate([_block_diag(lru_w_a[e]), _block_diag(lru_w_x[e])], axis=1)) for e in range(n_even)]
    lru_bg = [jnp.concatenate([lru_b_a[e], lru_b_x[e]]) for e in range(n_even)]
    c = CONV_CH
    lam_inits = [0.8 - 0.6 * math.exp(-0.3 * (2 * o + 1)) for o in range(n_odd)]

    mem2 = mem_prompt.reshape(bsz * N_MEM, d)
    p_mem_k, p_mem_v, p_mem_kb, p_mem_vb = [], [], [], []
    for l in range(depth):
        mk, mv, mkb, mvb = _memkv(mem2, xattn_w_kv_b[l])
        p_mem_k.append(mk.reshape(bsz, N_MEM, XATT_HEADS, XATT_HD))
        p_mem_v.append(mv.reshape(bsz, N_MEM, XATT_HEADS, XATT_HD))
        p_mem_kb.append(mkb.reshape(bsz, N_MEM, d))
        p_mem_vb.append(mvb.reshape(bsz, N_MEM, d))

    assert n_odd == 1, "the paged attention is carried by the prompt kernels of a single odd layer"
    l_odd = 1
    s_conv_a, s_conv_b, s_lru, s_k, s_v = [], [], [], [], []
    pool = cache_k.shape[1]
    pages_k = cache_k.reshape(n_odd * pool, PAGE_SIZE, ATT_HEADS, ATT_VD)
    pages_v = cache_v.reshape(n_odd * pool, PAGE_SIZE, ATT_HEADS, ATT_VD)

    def s_even(x, e, post):
        u, br, gb = _even_in(x, mix_g[2 * e], even_w_in_b[e])
        ya, yb, h_new = _sample_mix(
            u, br, gb, jnp.swapaxes(state_conv_a[e], 0, 1), jnp.swapaxes(state_conv_b[e], 0, 1), state_lru[e],
            conv_a_w[e], conv_a_b[e], conv_a_ln_g[e], conv_a_ln_b[e],
            conv_b_w[e], conv_b_b[e], lru_wg[e], lru_bg[e], lru_lambda[e])
        s_conv_a.append(jnp.concatenate([state_conv_a[e][:, 1:], u[:, None, :]], axis=1))
        s_conv_b.append(jnp.concatenate([state_conv_b[e][:, 1:], br[:, None, :]], axis=1))
        s_lru.append(h_new)
        return _outproj(x, [ya, yb], [even_w_out_b[e][:c], even_w_out_b[e][c:]], post)

    s_xattn = lambda q, l: _xattn_cached(q, cache_mem_k, cache_mem_v, l)
    xs = _trunk(x_sample.reshape(dbsz * dseq, d), (dbsz, dseq), s_xattn, p, s_even, None, range(l_odd))
    xs = _ffn(xs, ffn1_g[l_odd], p['ffn1_w_in'], p['ffn1_w_out'], l_odd)
    sq, sk, sv = _proj(xs, mix_g[l_odd], attn_w_in_b[0], d, [0, 1, 2],
                       [(0, BF16, Q_SCALE), (1, F32, 1.0), (2, F32, 1.0)])
    s_k.append(sk.reshape(dbsz, dseq, ATT_HEADS, ATT_VD))
    s_v.append(sv.reshape(dbsz, dseq, ATT_HEADS, ATT_VD))
    heads = (dbsz, ATT_HEADS, ATT_VD)
    n_seq = dbsz // RIDER_HOSTS
    riders = [dict(seq0=i * n_seq, n_seq=n_seq, q=sq.reshape(heads), k_new=sk.reshape(heads), v_new=sv.reshape(heads),
                   cache_k=pages_k, cache_v=pages_v, page_base=0, page_table=page_table,
                   lam_p=(lam_q1[0], lam_k1[0], lam_q2[0], lam_k2[0]), g=attn_subln_g[0], lam_init=lam_inits[0])
              for i in range(RIDER_HOSTS)]
    att_parts = []

    def carried(outs, rider):
        if rider is None:
            return outs
        att_parts.append(outs[-1])
        return outs[:-1]

    p_conv_a, p_conv_b, p_lru, p_k, p_v = [], [], [], [], []

    def p_even(x, e, post):
        u, br, gb = _even_in(x, mix_g[2 * e], even_w_in_b[e])
        u3, br3, gb3 = (a.reshape(bsz, seq, c) for a in (u, br, gb))
        ya = _conv_a(u3, conv_a_w[e], conv_a_b[e], conv_a_ln_g[e], conv_a_ln_b[e])
        yb, h_last = _lru(br3, gb3, conv_b_w[e], conv_b_b[e], lru_wg[e], lru_bg[e], lru_lambda[e])
        p_conv_a.append(u3[:, seq - (CONV_A_WIDTH - 1):])
        p_conv_b.append(br3[:, seq - (CONV_B_WIDTH - 1):])
        p_lru.append(h_last.reshape(bsz, c))
        rider = riders.pop(0) if riders else None
        return carried(_outproj(x, [ya.reshape(-1, c), yb.reshape(-1, c)],
                                [even_w_out_b[e][:c], even_w_out_b[e][c:]], post, rider), rider)

    def p_odd(x, o, l, post):
        qt, k, v, kb, vt = _qkv(x, mix_g[l], attn_w_in_b[o], bsz, seq)
        p_k.append(k.reshape(bsz, seq, ATT_HEADS, ATT_VD))
        p_v.append(v.reshape(bsz, seq, ATT_HEADS, ATT_VD))
        att = _flash(qt, kb.reshape(bsz, seq, d), vt,
                     (lam_q1[o], lam_k1[o], lam_q2[o], lam_k2[o]), attn_subln_g[o], lam_inits[o])
        rider = riders.pop(0) if riders else None
        return carried(_outproj(x, [att.reshape(-1, d)], [attn_w_out_b[o]], post, rider), rider)

    def p_xattn(q, l):
        rider = riders.pop(0) if riders else None
        return carried(_xattn(q, p_mem_kb[l], p_mem_vb[l], rider), rider)[0]

    y_prompt = _trunk(x_prompt.reshape(bsz * seq, d), (bsz, seq), p_xattn, p, p_even, p_odd, range(depth))
    y_prompt = y_prompt.reshape(bsz, seq, d)
    assert not riders and len(att_parts) == RIDER_HOSTS

    att = jnp.concatenate(att_parts, axis=0).reshape(dbsz * dseq, d)
    post = (xattn_g[l_odd], p['xattn_w_q'][l_odd], XATT_HD ** -0.5)
    mixed = _outproj(xs, [att], [attn_w_out_b[0]], post)
    y_sample = _trunk(xs, (dbsz, dseq), s_xattn, p, s_even, None, range(l_odd, depth), mixed=tuple(mixed))
    y_sample = y_sample.reshape(dbsz, dseq, d)

    return (y_prompt, y_sample,
            _stack(p_conv_a), _stack(p_conv_b), _stack(p_lru),
            _stack(p_k), _stack(p_v),
            _stack(p_mem_k), _stack(p_mem_v),
            _stack(s_conv_a), _stack(s_conv_b), _stack(s_lru),
            _stack(s_k), _stack(s_v))
```

```python
import functools
import math

import jax
import jax.numpy as jnp
from jax import lax
from jax.experimental import pallas as pl
from jax.experimental.pallas import tpu as pltpu

F32 = jnp.float32
BF16 = jnp.bfloat16

D_MODEL = 1024
D_FF = 2816
CONV_CH = 512
CONV_A_WIDTH = 31
LRU_CH = 512
LRU_BLOCKS = 8
CONV_B_WIDTH = 4
LRU_C = 8.0
ATT_HEADS = 8
ATT_HD = 64
ATT_VD = 128
N_MEM = 256
XATT_HEADS = 4
XATT_HD = 256
PAGE_SIZE = 128
EPS = 1e-6
NEG_INF = -1e30

SUBLANES = 8
VMEM_LIMIT = 56 * 1024 * 1024

MXU_COLS = 256
FFN_CHUNK = 4 * MXU_COLS
XS_BATCH = 4
ROW_TILE = 512
SEQ_TILE_CONV = 512
SEQ_TILE_LRU = 256
ATT_TQ = 512
DEC_PAGES = 16
RIDER_HOSTS = 4


def _params(sem):
    return pltpu.CompilerParams(dimension_semantics=sem, vmem_limit_bytes=VMEM_LIMIT)


def _rms(x, g):
    return x * lax.rsqrt(jnp.mean(x * x, axis=-1, keepdims=True) + EPS) * g


def _dot(a, b):
    return jnp.dot(a, b, preferred_element_type=F32)


def _dot_nt(a, b):
    return lax.dot_general(a, b, (((1,), (1,)), ((), ())), preferred_element_type=F32)


def _row_tile(m):
    return ROW_TILE if m % ROW_TILE == 0 else m


def _store_heads(o_ref, y, n_heads):
    o_ref[...] = y.reshape(y.shape[0], n_heads, y.shape[1] // n_heads)


def _ffn_chunks(f):
    return [(c0, min(FFN_CHUNK, f - c0)) for c0 in range(0, f, FFN_CHUNK)]


def _ffn_body(*refs, final, pre):
    refs = list(refs)
    x_ref, g_ref, wi_ref, wo_ref = refs[:4]
    pos = 4
    if pre:
        y_ref, wp_ref = refs[pos:pos + 2]
        pos += 2
    if final:
        gf_ref = refs[pos]
        pos += 1
    o_ref = refs[pos]
    f = wo_ref.shape[0]
    x = x_ref[...]
    if pre:
        x = x + _dot(y_ref[...], wp_ref[...])
    xn = _rms(x, g_ref[...]).astype(BF16)
    acc = None
    for c0, cw in _ffn_chunks(f):
        gate = _dot(xn, wi_ref[:, c0:c0 + cw])
        up = _dot(xn, wi_ref[:, f + c0:f + c0 + cw])
        h = (gate * jax.nn.sigmoid(gate) * up).astype(BF16)
        part = _dot(h, wo_ref[c0:c0 + cw, :])
        acc = part if acc is None else acc + part
    y = x + 0.5 * acc
    if final:
        y = _rms(y, gf_ref[...])
    o_ref[...] = y


def _ffn(x, g, w_in, w_out, layer, final_g=None, pre=None):
    m, d = x.shape
    f = w_out.shape[1]
    tm = _row_tile(m)
    final = final_g is not None
    const = lambda shape: pl.BlockSpec(shape, lambda i: (0, 0), pipeline_mode=pl.Buffered(1))
    of_layer = lambda w: pl.BlockSpec((None,) + w.shape[1:], lambda i: (layer, 0, 0), pipeline_mode=pl.Buffered(1))
    in_specs = [pl.BlockSpec((tm, d), lambda i: (i, 0)), const((1, d)), of_layer(w_in), of_layer(w_out)]
    args = [x, g.reshape(1, d), w_in, w_out]
    if pre is not None:
        y, wp = pre
        in_specs += [pl.BlockSpec((tm, y.shape[1]), lambda i: (i, 0)), of_layer(wp)]
        args += [y, wp]
    if final:
        in_specs.append(const((1, d)))
        args.append(final_g.reshape(1, d))
    return pl.pallas_call(
        functools.partial(_ffn_body, final=final, pre=pre is not None),
        grid=(m // tm,),
        in_specs=in_specs,
        out_specs=pl.BlockSpec((tm, d), lambda i: (i, 0)),
        out_shape=jax.ShapeDtypeStruct((m, d), F32),
        compiler_params=_params(("parallel",)),
        name="ffn",
    )(*args)


def _proj_body(*refs, has_g, n_w, out_meta):
    x_ref = refs[0]
    pos = 1
    if has_g:
        g_ref = refs[pos]
        pos += 1
    w_refs = refs[pos:pos + n_w]
    o_refs = refs[pos + n_w:]
    x = x_ref[...]
    if has_g:
        x = _rms(x, g_ref[...])
    xb = x.astype(BF16)
    ys = [_dot(xb, w_ref[...]) for w_ref in w_refs]
    for o_ref, (wi, scale) in zip(o_refs, out_meta):
        y = ys[wi]
        if scale != 1.0:
            y = y * scale
        o_ref[...] = y.astype(o_ref.dtype)


def _proj(x, g, w, width, blocks, outs):
    m, k = x.shape
    tm = _row_tile(m)
    has_g = g is not None
    in_specs = [pl.BlockSpec((tm, k), lambda i: (i, 0))]
    args = [x]
    if has_g:
        in_specs.append(pl.BlockSpec((1, k), lambda i: (0, 0)))
        args.append(g.reshape(1, k))
    for b in blocks:
        in_specs.append(pl.BlockSpec((k, width), lambda i, b=b: (0, b)))
        args.append(w)
    return pl.pallas_call(
        functools.partial(_proj_body, has_g=has_g, n_w=len(blocks),
                          out_meta=tuple((wi, sc) for wi, _, sc in outs)),
        grid=(m // tm,),
        in_specs=in_specs,
        out_specs=[pl.BlockSpec((tm, width), lambda i: (i, 0)) for _ in outs],
        out_shape=[jax.ShapeDtypeStruct((m, width), dt) for _, dt, _ in outs],
        compiler_params=_params(("parallel",)),
        name="proj",
    )(*args)


def _outproj_body(*refs, n, post_scale):
    x_ref = refs[0]
    y_refs = refs[1:1 + n]
    w_refs = refs[1 + n:1 + 2 * n]
    pos = 1 + 2 * n
    if post_scale is not None:
        g_ref, wq_ref = refs[pos:pos + 2]
        pos += 2
    o_ref = refs[pos]
    acc = x_ref[...]
    for y_ref, w_ref in zip(y_refs, w_refs):
        acc = acc + _dot(y_ref[...], w_ref[...])
    o_ref[...] = acc
    if post_scale is not None:
        q_ref = refs[pos + 1]
        q = _dot(_rms(acc, g_ref[...]).astype(BF16), wq_ref[...]) * post_scale
        q_ref[...] = q.astype(q_ref.dtype)


def _outproj(x, ys, ws, post=None, rider=None):
    m, d = x.shape
    tm = _row_tile(m)
    n = len(ys)
    row = pl.BlockSpec((tm, d), lambda i: (i, 0))
    in_specs = [row]
    in_specs += [pl.BlockSpec((tm, y.shape[1]), lambda i: (i, 0)) for y in ys]
    in_specs += [pl.BlockSpec(w.shape, lambda i: (0, 0)) for w in ws]
    args = [x, *ys, *ws]
    out_specs = [row]
    out_shape = [jax.ShapeDtypeStruct((m, d), F32)]
    post_scale = None
    if post is not None:
        g, wq, post_scale = post
        in_specs += [pl.BlockSpec((1, d), lambda i: (0, 0)), pl.BlockSpec(wq.shape, lambda i: (0, 0))]
        args += [g.reshape(1, d), wq]
        out_specs.append(pl.BlockSpec((tm, wq.shape[1]), lambda i: (i, 0)))
        out_shape.append(jax.ShapeDtypeStruct((m, wq.shape[1]), BF16))
    return _call_with_rider(functools.partial(_outproj_body, n=n, post_scale=post_scale), (m // tm,),
                            in_specs, args, out_specs, out_shape, "outproj", rider)


def _even_in_body(x_ref, g_ref, w_ref, u_ref, br_ref, gb_ref):
    xn = _rms(x_ref[...], g_ref[...]).astype(BF16)
    z = _dot(xn, w_ref[...])
    c = CONV_CH
    u_ref[...] = z[:, :c] * jax.nn.sigmoid(z[:, c:2 * c])
    br_ref[...] = z[:, 2 * c:2 * c + LRU_CH]
    gb_ref[...] = jax.nn.gelu(z[:, 2 * c + LRU_CH:])


def _even_in(x, g, w, rider=None):
    m, d = x.shape
    tm = _row_tile(m)
    n = w.shape[1]
    in_specs = [pl.BlockSpec((tm, d), lambda i: (i, 0)),
                pl.BlockSpec((1, d), lambda i: (0, 0)),
                pl.BlockSpec((d, n), lambda i: (0, 0))]
    return _call_with_rider(_even_in_body, (m // tm,), in_specs, [x, g.reshape(1, d), w],
                            [pl.BlockSpec((tm, CONV_CH), lambda i: (i, 0)) for _ in range(3)],
                            [jax.ShapeDtypeStruct((m, CONV_CH), F32) for _ in range(3)], "even_in", rider)


def _layer_norm_silu(y, g, b):
    mu = jnp.mean(y, axis=-1, keepdims=True)
    yc = y - mu
    var = jnp.mean(yc * yc, axis=-1, keepdims=True)
    yn = yc * lax.rsqrt(var + EPS) * g + b
    return yn * jax.nn.sigmoid(yn)


CONV_HALO = 32
CONV_CHUNK = 32


def _conv_a_body(u_ref, prev_ref, cw_ref, cb_ref, lg_ref, lb_ref, o_ref, ext_ref, rot_ref, *, tt):
    i = pl.program_id(1)
    c = CONV_CH
    n_ext = tt + CONV_HALO

    @pl.when(i == 0)
    def _():
        ext_ref[0:CONV_HALO, :] = jnp.zeros((CONV_HALO, c), F32)

    @pl.when(i != 0)
    def _():
        ext_ref[0:CONV_HALO, :] = prev_ref[0]

    ext_ref[CONV_HALO:n_ext, :] = u_ref[0]
    ext_ref[n_ext:n_ext + SUBLANES, :] = jnp.zeros((SUBLANES, c), F32)
    for r in range(SUBLANES):
        rot_ref[r] = ext_ref[pl.ds(r, n_ext), :]
    first = CONV_HALO - (CONV_A_WIDTH - 1)
    cb = cb_ref[...]
    lg = lg_ref[...]
    lb = lb_ref[...]
    for r0 in range(0, tt, CONV_CHUNK):
        acc = jnp.zeros((CONV_CHUNK // SUBLANES, SUBLANES, c), F32)
        for w in range(CONV_A_WIDTH):
            s = first + w
            tap = rot_ref[s % SUBLANES, pl.ds(r0 + (s // SUBLANES) * SUBLANES, CONV_CHUNK), :]
            acc = acc + tap.reshape(acc.shape) * cw_ref[w * SUBLANES:(w + 1) * SUBLANES, :][None]
        y = acc.reshape(CONV_CHUNK, c) + cb
        o_ref[0, r0:r0 + CONV_CHUNK, :] = _layer_norm_silu(y, lg, lb).astype(o_ref.dtype)


def _conv_a(u, cw, cb, lg, lb):
    b, t, c = u.shape
    tt = SEQ_TILE_CONV
    halo_blocks = tt // CONV_HALO
    vec = lambda a: a.reshape(1, c)
    return pl.pallas_call(
        functools.partial(_conv_a_body, tt=tt),
        grid=(b, t // tt),
        in_specs=[pl.BlockSpec((1, tt, c), lambda bi, i: (bi, i, 0)),
                  pl.BlockSpec((1, CONV_HALO, c), lambda bi, i: (bi, jnp.maximum(i * halo_blocks - 1, 0), 0)),
                  pl.BlockSpec((CONV_A_WIDTH * SUBLANES, c), lambda bi, i: (0, 0)),
                  pl.BlockSpec((1, c), lambda bi, i: (0, 0)),
                  pl.BlockSpec((1, c), lambda bi, i: (0, 0)),
                  pl.BlockSpec((1, c), lambda bi, i: (0, 0))],
        out_specs=pl.BlockSpec((1, tt, c), lambda bi, i: (bi, i, 0)),
        out_shape=jax.ShapeDtypeStruct((b, t, c), BF16),
        scratch_shapes=[pltpu.VMEM((tt + CONV_HALO + SUBLANES, c), F32),
                        pltpu.VMEM((SUBLANES, tt + CONV_HALO, c), F32)],
        compiler_params=_params(("parallel", "arbitrary")),
        name="conv_a",
    )(u, u, jnp.repeat(cw, SUBLANES, axis=0), vec(cb), vec(lg), vec(lb))


def _lru_gates(xr, wg_ref, bg_ref, lam_ref):
    gates = _dot(xr.astype(BF16), wg_ref[...]) + bg_ref[...]
    r = jax.nn.sigmoid(gates[:, :LRU_CH])
    ig = jax.nn.sigmoid(gates[:, LRU_CH:])
    nl = -lam_ref[...]
    softplus = jnp.maximum(nl, 0.0) + jnp.log1p(jnp.exp(-jnp.abs(nl)))
    log_a = -LRU_C * r * softplus
    a = jnp.exp(log_a)
    beta = jnp.sqrt(jnp.maximum(-jnp.tanh(log_a) * (a * a + 1.0), 0.0))
    return a, beta * ig * xr


def _lru_body(br_ref, prev_ref, gb_ref, cw_ref, cb_ref, wg_ref, bg_ref, lam_ref,
              yb_ref, hl_ref, ext_ref, h_ref, *, tt):
    i = pl.program_id(1)
    c = LRU_CH

    @pl.when(i == 0)
    def _():
        h_ref[...] = jnp.zeros((1, c), F32)
        ext_ref[0:SUBLANES, :] = jnp.zeros((SUBLANES, c), F32)

    @pl.when(i != 0)
    def _():
        ext_ref[0:SUBLANES, :] = prev_ref[0]

    ext_ref[SUBLANES:SUBLANES + tt, :] = br_ref[0]
    first = SUBLANES - (CONV_B_WIDTH - 1)
    xr = cb_ref[...]
    for w in range(CONV_B_WIDTH):
        xr = xr + ext_ref[pl.ds(first + w, tt), :] * cw_ref[w:w + 1, :]
    a, u = _lru_gates(xr, wg_ref, bg_ref, lam_ref)

    row = lax.broadcasted_iota(jnp.int32, (tt, c), 0)
    d = 1
    while d < tt:
        if d < SUBLANES:
            keep = row >= d
            a_sh = jnp.where(keep, pltpu.roll(a, d, 0), 1.0)
            u_sh = jnp.where(keep, pltpu.roll(u, d, 0), 0.0)
        else:
            a_sh = jnp.concatenate([jnp.ones((d, c), F32), a[:tt - d]], axis=0)
            u_sh = jnp.concatenate([jnp.zeros((d, c), F32), u[:tt - d]], axis=0)
        u = a * u_sh + u
        a = a * a_sh
        d *= 2
    h = a * h_ref[...] + u
    yb_ref[0] = (h * gb_ref[0]).astype(yb_ref.dtype)
    h_last = h[tt - 1:tt, :]
    h_ref[...] = h_last
    hl_ref[0] = h_last


def _lru(br, gb, cw, cb, wg, bg, lam):
    b, t, c = br.shape
    tt = SEQ_TILE_LRU
    halo_blocks = tt // SUBLANES
    vec = lambda a: a.reshape(1, -1)
    return pl.pallas_call(
        functools.partial(_lru_body, tt=tt),
        grid=(b, t // tt),
        in_specs=[pl.BlockSpec((1, tt, c), lambda bi, i: (bi, i, 0)),
                  pl.BlockSpec((1, SUBLANES, c), lambda bi, i: (bi, jnp.maximum(i * halo_blocks - 1, 0), 0)),
                  pl.BlockSpec((1, tt, c), lambda bi, i: (bi, i, 0)),
                  pl.BlockSpec((CONV_B_WIDTH, c), lambda bi, i: (0, 0)),
                  pl.BlockSpec((1, c), lambda bi, i: (0, 0)),
                  pl.BlockSpec((c, 2 * c), lambda bi, i: (0, 0)),
                  pl.BlockSpec((1, 2 * c), lambda bi, i: (0, 0)),
                  pl.BlockSpec((1, c), lambda bi, i: (0, 0))],
        out_specs=[pl.BlockSpec((1, tt, c), lambda bi, i: (bi, i, 0)),
                   pl.BlockSpec((1, 1, c), lambda bi, i: (bi, 0, 0))],
        out_shape=[jax.ShapeDtypeStruct((b, t, c), BF16),
                   jax.ShapeDtypeStruct((b, 1, c), F32)],
        scratch_shapes=[pltpu.VMEM((tt + SUBLANES, c), F32), pltpu.VMEM((1, c), F32)],
        compiler_params=_params(("parallel", "arbitrary")),
        name="lru",
    )(br, br, gb, cw, vec(cb), wg, vec(bg), vec(lam))


def _sample_mix_body(u_ref, br_ref, gb_ref, ha_ref, hb_ref, h0_ref,
                     cwa_ref, cba_ref, lg_ref, lb_ref, cwb_ref, cbb_ref, wg_ref, bg_ref, lam_ref,
                     ya_ref, yb_ref, h_ref):
    u = u_ref[...]
    acc = cba_ref[...] + u * cwa_ref[CONV_A_WIDTH - 1:CONV_A_WIDTH, :]
    for w in range(CONV_A_WIDTH - 1):
        acc = acc + ha_ref[w] * cwa_ref[w:w + 1, :]
    ya_ref[...] = _layer_norm_silu(acc, lg_ref[...], lb_ref[...]).astype(ya_ref.dtype)

    br = br_ref[...]
    xr = cbb_ref[...] + br * cwb_ref[CONV_B_WIDTH - 1:CONV_B_WIDTH, :]
    for w in range(CONV_B_WIDTH - 1):
        xr = xr + hb_ref[w] * cwb_ref[w:w + 1, :]
    a, x_in = _lru_gates(xr, wg_ref, bg_ref, lam_ref)
    h = a * h0_ref[...] + x_in
    h_ref[...] = h
    yb_ref[...] = (h * gb_ref[...]).astype(yb_ref.dtype)


def _sample_mix(u, br, gb, hist_a_t, hist_b_t, h0, cwa, cba, lg, lb, cwb, cbb, wg, bg, lam):
    n, c = u.shape
    vec = lambda a: a.reshape(1, -1)
    args = [u, br, gb, hist_a_t, hist_b_t, h0, cwa, vec(cba), vec(lg), vec(lb), cwb, vec(cbb), wg, vec(bg), vec(lam)]
    full = lambda a: pl.BlockSpec(a.shape, lambda i, nd=a.ndim: (0,) * nd)
    return pl.pallas_call(
        _sample_mix_body,
        grid=(1,),
        in_specs=[full(a) for a in args],
        out_specs=[pl.BlockSpec((n, c), lambda i: (0, 0)) for _ in range(3)],
        out_shape=[jax.ShapeDtypeStruct((n, c), BF16), jax.ShapeDtypeStruct((n, c), BF16),
                   jax.ShapeDtypeStruct((n, c), F32)],
        compiler_params=_params(("arbitrary",)),
        name="sample_mix",
    )(*args)


def _diff_lambda(lq1_ref, lk1_ref, lq2_ref, lk2_ref, lam_init):
    e1 = jnp.exp(jnp.sum(lq1_ref[...] * lk1_ref[...], axis=-1, keepdims=True))
    e2 = jnp.exp(jnp.sum(lq2_ref[...] * lk2_ref[...], axis=-1, keepdims=True))
    return e1 - e2 + lam_init


Q_SCALE = ATT_HD ** -0.5 * math.log2(math.e)
VT_ROWS = ATT_VD + 16


def _qkv_body(x_ref, g_ref, wq_ref, wk_ref, wv_ref, qt_ref, k_ref, v_ref, kb_ref, vt_ref):
    xn = _rms(x_ref[...], g_ref[...]).astype(BF16)
    q = _dot(xn, wq_ref[...])
    k = _dot(xn, wk_ref[...])
    v = _dot(xn, wv_ref[...])
    qt_ref[0] = (q * Q_SCALE).T.astype(BF16)
    _store_heads(k_ref, k, ATT_HEADS)
    kb_ref[...] = k.astype(BF16)
    _store_heads(v_ref, v, ATT_HEADS)
    vt = v.T.astype(BF16)
    ones = jnp.ones((VT_ROWS - ATT_VD, vt.shape[1]), BF16)
    for h in range(ATT_HEADS):
        vt_ref[0, h * VT_ROWS:h * VT_ROWS + ATT_VD, :] = vt[h * ATT_VD:(h + 1) * ATT_VD, :]
        vt_ref[0, h * VT_ROWS + ATT_VD:(h + 1) * VT_ROWS, :] = ones


def _qkv(x, g, w, bsz, seq):
    m, d = x.shape
    tm = ROW_TILE
    per_seq = seq // tm
    row = pl.BlockSpec((tm, d), lambda i: (i, 0))
    heads = pl.BlockSpec((tm, ATT_HEADS, ATT_VD), lambda i: (i, 0, 0))
    col = lambda rows: pl.BlockSpec((1, rows, tm), lambda i: (i // per_seq, 0, i % per_seq))
    wspec = lambda b: pl.BlockSpec((d, d), lambda i: (0, b))
    return pl.pallas_call(
        _qkv_body,
        grid=(m // tm,),
        in_specs=[row, pl.BlockSpec((1, d), lambda i: (0, 0)), wspec(0), wspec(1), wspec(2)],
        out_specs=[col(d), heads, heads, row, col(ATT_HEADS * VT_ROWS)],
        out_shape=[jax.ShapeDtypeStruct((bsz, d, seq), BF16),
                   jax.ShapeDtypeStruct((m, ATT_HEADS, ATT_VD), F32),
                   jax.ShapeDtypeStruct((m, ATT_HEADS, ATT_VD), F32),
                   jax.ShapeDtypeStruct((m, d), BF16),
                   jax.ShapeDtypeStruct((bsz, ATT_HEADS * VT_ROWS, seq), BF16)],
        compiler_params=_params(("parallel",)),
        name="qkv",
    )(x, g.reshape(1, d), w, w, w)


def _flash_body(qi_ref, ki_ref, qt_ref, k_ref, vt_ref, lq1_ref, lk1_ref, lq2_ref, lk2_ref, g_ref, o_ref,
                qs_ref, m_ref, acc_ref, s_ref, *, tq, lam_init):
    step_id = pl.program_id(1)
    qi = qi_ref[step_id]
    ki = ki_ref[step_id]
    vd = ATT_VD

    @pl.when(ki == 0)
    def _():
        dim = lax.broadcasted_iota(jnp.int32, (vd, tq), 0)
        for h in range(ATT_HEADS):
            qh = qt_ref[0, h * vd:(h + 1) * vd, :]
            zero = jnp.zeros_like(qh)
            qs_ref[h, :, 0:tq] = jnp.where(dim < ATT_HD, qh, zero)
            qs_ref[h, :, tq:2 * tq] = jnp.where(dim >= ATT_HD, qh, zero)
        m_ref[...] = jnp.full(m_ref.shape, NEG_INF, F32)
        acc_ref[...] = jnp.zeros(acc_ref.shape, F32)

    def scores(h):
        s_ref[h % 2] = _dot(k_ref[0, :, h * vd:(h + 1) * vd], qs_ref[h])

    def step(diagonal):
        if diagonal:
            key = lax.broadcasted_iota(jnp.int32, (tq, 2 * tq), 0)
            qry = lax.broadcasted_iota(jnp.int32, (tq, 2 * tq), 1)
            visible = key <= jnp.where(qry >= tq, qry - tq, qry)
        scores(0)
        for h in range(ATT_HEADS):
            if h + 1 < ATT_HEADS:
                scores(h + 1)
            s = s_ref[h % 2]
            if diagonal:
                s = jnp.where(visible, s, NEG_INF)
            m_prev = m_ref[h]
            m_new = jnp.maximum(m_prev, jnp.max(s, axis=0, keepdims=True))
            alpha = jnp.exp2(m_prev - m_new)
            p = jnp.exp2(s - m_new).astype(BF16)
            vth = vt_ref[0, h * VT_ROWS:(h + 1) * VT_ROWS, :]
            acc_ref[h] = alpha * acc_ref[h] + _dot(vth, p)
            m_ref[h] = m_new

    @pl.when(ki < qi)
    def _():
        step(False)

    @pl.when(ki == qi)
    def _():
        step(True)
        lam = _diff_lambda(lq1_ref, lk1_ref, lq2_ref, lk2_ref, lam_init)
        g = g_ref[...]
        for h in range(ATT_HEADS):
            o = acc_ref[h, 0:vd, :] / acc_ref[h, vd:vd + 1, :]
            att = o[:, 0:tq] - lam * o[:, tq:2 * tq]
            ms = jnp.mean(att * att, axis=0, keepdims=True)
            att = att * lax.rsqrt(ms + EPS) * g * (1.0 - lam_init)
            o_ref[0, :, h * vd:(h + 1) * vd] = att.T.astype(o_ref.dtype)


def _flash(qt, k, vt, lam_p, g, lam_init):
    b, d, t = qt.shape
    tq = ATT_TQ
    nq = t // tq
    pairs = [(qi, ki) for qi in range(nq) for ki in range(qi + 1)]
    qi_tab = jnp.asarray([p[0] for p in pairs], jnp.int32)
    ki_tab = jnp.asarray([p[1] for p in pairs], jnp.int32)
    vec = lambda a: a.reshape(1, -1)
    small = lambda n: pl.BlockSpec((1, n), lambda bi, s, qi, ki: (0, 0))
    grid_spec = pltpu.PrefetchScalarGridSpec(
        num_scalar_prefetch=2,
        grid=(b, len(pairs)),
        in_specs=[pl.BlockSpec((1, d, tq), lambda bi, s, qi, ki: (bi, 0, qi[s])),
                  pl.BlockSpec((1, tq, d), lambda bi, s, qi, ki: (bi, ki[s], 0)),
                  pl.BlockSpec((1, ATT_HEADS * VT_ROWS, tq), lambda bi, s, qi, ki: (bi, 0, ki[s])),
                  small(ATT_HD), small(ATT_HD), small(ATT_HD), small(ATT_HD),
                  pl.BlockSpec((ATT_VD, 1), lambda bi, s, qi, ki: (0, 0))],
        out_specs=pl.BlockSpec((1, tq, d), lambda bi, s, qi, ki: (bi, qi[s], 0)),
        scratch_shapes=[pltpu.VMEM((ATT_HEADS, ATT_VD, 2 * tq), BF16),
                        pltpu.VMEM((ATT_HEADS, 1, 2 * tq), F32),
                        pltpu.VMEM((ATT_HEADS, VT_ROWS, 2 * tq), F32),
                        pltpu.VMEM((2, tq, 2 * tq), F32)],
    )
    return pl.pallas_call(
        functools.partial(_flash_body, tq=tq, lam_init=lam_init),
        grid_spec=grid_spec,
        out_shape=jax.ShapeDtypeStruct((b, t, d), BF16),
        compiler_params=_params(("parallel", "arbitrary")),
        name="flash_diff_attn",
    )(qi_tab, ki_tab, qt, k, vt, *[vec(a) for a in lam_p], g.reshape(ATT_VD, 1))


def _decode_step(j, last, refs_in, o_ref, scratch, lam_init):
    q_ref, kn_ref, vn_ref, lq1_ref, lk1_ref, lq2_ref, lk2_ref, g_ref = refs_in[:8]
    k_refs = refs_in[8:8 + DEC_PAGES]
    v_refs = refs_in[8 + DEC_PAGES:8 + 2 * DEC_PAGES]
    qx_ref, m_ref, l_ref, acc_ref = scratch
    nh = ATT_HEADS
    rows = PAGE_SIZE * nh

    @pl.when(j == 0)
    def _():
        q8 = q_ref[0].astype(F32)
        lane = lax.broadcasted_iota(jnp.int32, (nh, ATT_VD), 1)
        qx = jnp.concatenate([jnp.where(lane < ATT_HD, q8, 0.0), jnp.where(lane >= ATT_HD, q8, 0.0)], axis=0)
        qx_ref[...] = qx.astype(BF16)
        kn = kn_ref[0].astype(BF16).astype(F32)
        vn = vn_ref[0].astype(BF16).astype(F32)
        m_ref[...] = jnp.sum(qx * jnp.concatenate([kn, kn], axis=0), axis=-1, keepdims=True)
        l_ref[...] = jnp.ones(l_ref.shape, F32)
        acc_ref[...] = jnp.concatenate([vn, vn], axis=0)

    qx = qx_ref[...]
    s = jnp.concatenate([_dot_nt(qx, k_ref[0].reshape(rows, ATT_VD).astype(BF16)) for k_ref in k_refs], axis=1)
    r = lax.broadcasted_iota(jnp.int32, s.shape, 0)
    c = lax.broadcasted_iota(jnp.int32, s.shape, 1)
    s = jnp.where(c % nh == r % nh, s, NEG_INF)
    m_prev = m_ref[...]
    m_new = jnp.maximum(m_prev, jnp.max(s, axis=-1, keepdims=True))
    alpha = jnp.exp2(m_prev - m_new)
    p = jnp.exp2(s - m_new)
    l_ref[...] = alpha * l_ref[...] + jnp.sum(p, axis=-1, keepdims=True)
    pb = p.astype(BF16)
    acc = alpha * acc_ref[...]
    for i, v_ref in enumerate(v_refs):
        acc = acc + _dot(pb[:, i * rows:(i + 1) * rows], v_ref[0].reshape(rows, ATT_VD).astype(BF16))
    acc_ref[...] = acc
    m_ref[...] = m_new

    @pl.when(j == last)
    def _():
        lam = _diff_lambda(lq1_ref, lk1_ref, lq2_ref, lk2_ref, lam_init)
        o = acc_ref[...] / l_ref[...]
        att = o[0:nh] - lam * o[nh:2 * nh]
        o_ref[0] = (_rms(att, g_ref[...]) * (1.0 - lam_init)).astype(o_ref.dtype)


def _linear_step(idx, grid):
    step = idx[0]
    for i, n in zip(idx[1:], grid[1:]):
        step = step * n + i
    return step


def _call_with_rider(host_body, grid, in_specs, args, out_specs, out_shape, name, rider):
    if rider is None:
        return pl.pallas_call(host_body, grid=grid, in_specs=in_specs, out_specs=out_specs, out_shape=out_shape,
                              compiler_params=_params(("parallel",) * len(grid)), name=name)(*args)
    nh, vd, pp = ATT_HEADS, ATT_VD, DEC_PAGES
    page_table = rider['page_table']
    per_seq = page_table.shape[1] // pp
    n_seq, seq0, base = rider['n_seq'], rider['seq0'], rider['page_base']
    assert math.prod(grid) == n_seq * per_seq
    n_in, n_out = len(in_specs), len(out_specs)
    host_in = [pl.BlockSpec(sp.block_shape, lambda *a, f=sp.index_map: f(*a[:-1]), pipeline_mode=sp.pipeline_mode)
               for sp in in_specs]
    host_out = [pl.BlockSpec(sp.block_shape, lambda *a, f=sp.index_map: f(*a[:-1])) for sp in out_specs]
    seq = lambda a: _linear_step(a[:-1], grid) // per_seq
    sub = lambda a: _linear_step(a[:-1], grid) % per_seq
    row_spec = pl.BlockSpec((1, nh, vd), lambda *a: (seq0 + seq(a), 0, 0))
    small = lambda n: pl.BlockSpec((1, n), lambda *a: (0, 0))
    page_specs = [pl.BlockSpec((1, PAGE_SIZE, nh, vd),
                               lambda *a, i=i: (base + a[-1][seq0 + seq(a), sub(a) * pp + i], 0, 0, 0))
                  for i in range(pp)]
    rider_in = [row_spec, row_spec, row_spec, small(ATT_HD), small(ATT_HD), small(ATT_HD), small(ATT_HD), small(vd)]
    rider_in += page_specs + page_specs
    vec = lambda a: a.reshape(1, -1)
    rider_args = [rider['q'], rider['k_new'], rider['v_new'], *[vec(a) for a in rider['lam_p']], vec(rider['g']),
                  *([rider['cache_k']] * pp), *([rider['cache_v']] * pp)]

    def body(pt_ref, *refs):
        host_body(*refs[:n_in], *refs[n_in + len(rider_in):n_in + len(rider_in) + n_out])
        step = _linear_step([pl.program_id(i) for i in range(len(grid))], grid)
        _decode_step(step % per_seq, per_seq - 1, refs[n_in:n_in + len(rider_in)],
                     refs[n_in + len(rider_in) + n_out], refs[n_in + len(rider_in) + n_out + 1:], rider['lam_init'])

    grid_spec = pltpu.PrefetchScalarGridSpec(
        num_scalar_prefetch=1,
        grid=grid,
        in_specs=host_in + rider_in,
        out_specs=host_out + [pl.BlockSpec((1, nh, vd), lambda *a: (seq(a), 0, 0))],
        scratch_shapes=[pltpu.VMEM((2 * nh, vd), BF16),
                        pltpu.VMEM((2 * nh, 1), F32),
                        pltpu.VMEM((2 * nh, 1), F32),
                        pltpu.VMEM((2 * nh, vd), F32)],
    )
    return pl.pallas_call(
        body,
        grid_spec=grid_spec,
        out_shape=list(out_shape) + [jax.ShapeDtypeStruct((n_seq, nh, vd), BF16)],
        compiler_params=_params(("arbitrary",) * len(grid)),
        name=name + "_paged",
    )(page_table, *args, *rider_args)


def _xattn_head(q, k, v):
    s = _dot_nt(q, k)
    p = jnp.exp(s - jnp.max(s, axis=-1, keepdims=True))
    p = p / jnp.sum(p, axis=-1, keepdims=True)
    return _dot(p.astype(BF16), v)


def _xattn_body(q_ref, k_ref, v_ref, o_ref):
    hd = XATT_HD
    heads = [slice(h * hd, (h + 1) * hd) for h in range(XATT_HEADS)]
    scores = [_dot_nt(q_ref[0, :, c], k_ref[0, :, c]) for c in heads]
    for c, s in zip(heads, scores):
        p = jnp.exp(s - jnp.max(s, axis=-1, keepdims=True))
        l = jnp.sum(p, axis=-1, keepdims=True)
        o_ref[0, :, c] = (_dot(p.astype(BF16), v_ref[0, :, c]) / l).astype(o_ref.dtype)


def _xattn(q, mem_k, mem_v, rider=None):
    b, t, d = q.shape
    tq = ROW_TILE
    kv_spec = pl.BlockSpec((1, N_MEM, d), lambda bi, i: (bi, 0, 0))
    q_spec = pl.BlockSpec((1, tq, d), lambda bi, i: (bi, i, 0))
    return _call_with_rider(_xattn_body, (b, t // tq), [q_spec, kv_spec, kv_spec], [q, mem_k, mem_v],
                            [q_spec], [jax.ShapeDtypeStruct((b, t, d), BF16)], "xattn", rider)


def _xattn_cached_body(q_ref, k_ref, v_ref, o_ref):
    nh, hd = XATT_HEADS, XATT_HD
    rows = N_MEM * nh
    head = lax.broadcasted_iota(jnp.int32, (nh, rows), 0)
    col = lax.broadcasted_iota(jnp.int32, (nh, rows), 1)
    own = col % nh == head
    for i in range(XS_BATCH):
        q = q_ref[i]
        qx = jnp.concatenate([q[:, h * hd:(h + 1) * hd] for h in range(nh)], axis=0)
        k = k_ref[0, i].reshape(rows, hd).astype(BF16)
        v = v_ref[0, i].reshape(rows, hd).astype(BF16)
        s = jnp.where(own, _dot_nt(qx, k), NEG_INF)
        p = jnp.exp(s - jnp.max(s, axis=-1, keepdims=True))
        p = p / jnp.sum(p, axis=-1, keepdims=True)
        o = _dot(p.astype(BF16), v)
        for h in range(nh):
            o_ref[i, :, h * hd:(h + 1) * hd] = o[h:h + 1, :].astype(o_ref.dtype)


def _xattn_cached(q, cache_k, cache_v, layer):
    b, t, d = q.shape
    bb = XS_BATCH
    kv_spec = pl.BlockSpec((1, bb, N_MEM, XATT_HEADS, XATT_HD), lambda i: (layer, i, 0, 0, 0))
    return pl.pallas_call(
        _xattn_cached_body,
        grid=(b // bb,),
        in_specs=[pl.BlockSpec((bb, t, d), lambda i: (i, 0, 0)), kv_spec, kv_spec],
        out_specs=pl.BlockSpec((bb, t, d), lambda i: (i, 0, 0)),
        out_shape=jax.ShapeDtypeStruct((b, t, d), BF16),
        compiler_params=_params(("parallel",)),
        name="xattn_cached",
    )(q, cache_k, cache_v)


def _memkv_body(x_ref, wk_ref, wv_ref, kn_ref, vn_ref, kb_ref, vb_ref):
    xb = x_ref[...].astype(BF16)
    for w_ref, n_ref, b_ref in ((wk_ref, kn_ref, kb_ref), (wv_ref, vn_ref, vb_ref)):
        y = _dot(xb, w_ref[...])
        b_ref[...] = y.astype(BF16)
        _store_heads(n_ref, y, XATT_HEADS)


def _memkv(x, w):
    m, d = x.shape
    tm = _row_tile(m)
    row = pl.BlockSpec((tm, d), lambda i: (i, 0))
    heads = pl.BlockSpec((tm, XATT_HEADS, XATT_HD), lambda i: (i, 0, 0))
    return pl.pallas_call(
        _memkv_body,
        grid=(m // tm,),
        in_specs=[row, pl.BlockSpec((d, d), lambda i: (0, 0)), pl.BlockSpec((d, d), lambda i: (0, 1))],
        out_specs=[heads, heads, row, row],
        out_shape=[jax.ShapeDtypeStruct((m, XATT_HEADS, XATT_HD), F32),
                   jax.ShapeDtypeStruct((m, XATT_HEADS, XATT_HD), F32),
                   jax.ShapeDtypeStruct((m, d), BF16),
                   jax.ShapeDtypeStruct((m, d), BF16)],
        compiler_params=_params(("parallel",)),
        name="memkv",
    )(x, w, w)


def _block_diag(w):
    nb, n, _ = w.shape
    eye = jnp.eye(nb, dtype=w.dtype)
    return jnp.einsum('kij,kl->kilj', w, eye).reshape(nb * n, nb * n)


def _stack(arrays):
    return arrays[0][None] if len(arrays) == 1 else jnp.stack(arrays)


def _trunk(x, bt, xattn, p, even_mix, odd_mix, layers, mixed=None):
    d = x.shape[1]
    depth = p['ffn1_g'].shape[0]
    for l in layers:
        if mixed is None:
            x = _ffn(x, p['ffn1_g'][l], p['ffn1_w_in'], p['ffn1_w_out'], l)
            post = (p['xattn_g'][l], p['xattn_w_q'][l], XATT_HD ** -0.5)
            x, q = even_mix(x, l // 2, post) if l % 2 == 0 else odd_mix(x, l // 2, l, post)
        else:
            x, q = mixed
            mixed = None
        o = xattn(q.reshape(*bt, d), l)
        last = l == depth - 1
        x = _ffn(x, p['ffn2_g'][l], p['ffn2_w_in'], p['ffn2_w_out'], l,
                 final_g=p['final_g'] if last else None, pre=(o.reshape(-1, d), p['xattn_w_out']))
    return x


def kernel(x_prompt, x_sample, state_conv_a, state_conv_b, state_lru, cache_k, cache_v, cache_mem_k, cache_mem_v, page_table, mem_prompt, ffn1_g, ffn1_w_in, ffn1_w_out, mix_g, even_w_in, conv_a_w, conv_a_b, conv_a_ln_g, conv_a_ln_b, conv_b_w, conv_b_b, lru_w_a, lru_b_a, lru_w_x, lru_b_x, lru_lambda, even_w_out, attn_w_in, lam_q1, lam_k1, lam_q2, lam_k2, attn_subln_g, attn_w_out, xattn_g, xattn_w_q, xattn_w_kv, xattn_w_out, ffn2_g, ffn2_w_in, ffn2_w_out, final_g):
    bsz, seq, d = x_prompt.shape
    dbsz, dseq, _ = x_sample.shape
    depth = ffn1_g.shape[0]
    n_even = even_w_in.shape[0]
    n_odd = attn_w_in.shape[0]
    bf = lambda w: w.astype(BF16)
    p = {
        'ffn1_g': ffn1_g, 'ffn1_w_in': bf(ffn1_w_in), 'ffn1_w_out': bf(ffn1_w_out),
        'ffn2_g': ffn2_g, 'ffn2_w_in': bf(ffn2_w_in), 'ffn2_w_out': bf(ffn2_w_out),
        'xattn_g': xattn_g, 'xattn_w_q': bf(xattn_w_q), 'xattn_w_out': bf(xattn_w_out),
        'final_g': final_g,
    }
    even_w_in_b = bf(even_w_in)
    even_w_out_b = bf(even_w_out)
    attn_w_in_b = bf(attn_w_in)
    attn_w_out_b = bf(attn_w_out)
    xattn_w_kv_b = bf(xattn_w_kv)
    lru_wg = [bf(jnp.concatenate([_block_diag(lru_w_a[e]), _block_diag(lru_w_x[e])], axis=1)) for e in range(n_even)]
    lru_bg = [jnp.concatenate([lru_b_a[e], lru_b_x[e]]) for e in range(n_even)]
    c = CONV_CH
    lam_inits = [0.8 - 0.6 * math.exp(-0.3 * (2 * o + 1)) for o in range(n_odd)]

    mem2 = mem_prompt.reshape(bsz * N_MEM, d)
    p_mem_k, p_mem_v, p_mem_kb, p_mem_vb = [], [], [], []
    for l in range(depth):
        mk, mv, mkb, mvb = _memkv(mem2, xattn_w_kv_b[l])
        p_mem_k.append(mk.reshape(bsz, N_MEM, XATT_HEADS, XATT_HD))
        p_mem_v.append(mv.reshape(bsz, N_MEM, XATT_HEADS, XATT_HD))
        p_mem_kb.append(mkb.reshape(bsz, N_MEM, d))
        p_mem_vb.append(mvb.reshape(bsz, N_MEM, d))

    assert n_odd == 1, "the paged attention is carried by the prompt kernels of a single odd layer"
    l_odd = 1
    s_conv_a, s_conv_b, s_lru, s_k, s_v = [], [], [], [], []
    pool = cache_k.shape[1]
    pages_k = cache_k.reshape(n_odd * pool, PAGE_SIZE, ATT_HEADS, ATT_VD)
    pages_v = cache_v.reshape(n_odd * pool, PAGE_SIZE, ATT_HEADS, ATT_VD)

    def s_even(x, e, post):
        u, br, gb = _even_in(x, mix_g[2 * e], even_w_in_b[e])
        ya, yb, h_new = _sample_mix(
            u, br, gb, jnp.swapaxes(state_conv_a[e], 0, 1), jnp.swapaxes(state_conv_b[e], 0, 1), state_lru[e],
            conv_a_w[e], conv_a_b[e], conv_a_ln_g[e], conv_a_ln_b[e],
            conv_b_w[e], conv_b_b[e], lru_wg[e], lru_bg[e], lru_lambda[e])
        s_conv_a.append(jnp.concatenate([state_conv_a[e][:, 1:], u[:, None, :]], axis=1))
        s_conv_b.append(jnp.concatenate([state_conv_b[e][:, 1:], br[:, None, :]], axis=1))
        s_lru.append(h_new)
        return _outproj(x, [ya, yb], [even_w_out_b[e][:c], even_w_out_b[e][c:]], post)

    s_xattn = lambda q, l: _xattn_cached(q, cache_mem_k, cache_mem_v, l)
    xs = _trunk(x_sample.reshape(dbsz * dseq, d), (dbsz, dseq), s_xattn, p, s_even, None, range(l_odd))
    xs = _ffn(xs, ffn1_g[l_odd], p['ffn1_w_in'], p['ffn1_w_out'], l_odd)
    sq, sk, sv = _proj(xs, mix_g[l_odd], attn_w_in_b[0], d, [0, 1, 2],
                       [(0, BF16, Q_SCALE), (1, F32, 1.0), (2, F32, 1.0)])
    s_k.append(sk.reshape(dbsz, dseq, ATT_HEADS, ATT_VD))
    s_v.append(sv.reshape(dbsz, dseq, ATT_HEADS, ATT_VD))
    heads = (dbsz, ATT_HEADS, ATT_VD)
    n_seq = dbsz // RIDER_HOSTS
    riders = [dict(seq0=i * n_seq, n_seq=n_seq, q=sq.reshape(heads), k_new=sk.reshape(heads), v_new=sv.reshape(heads),
                   cache_k=pages_k, cache_v=pages_v, page_base=0, page_table=page_table,
                   lam_p=(lam_q1[0], lam_k1[0], lam_q2[0], lam_k2[0]), g=attn_subln_g[0], lam_init=lam_inits[0])
              for i in range(RIDER_HOSTS)]
    att_parts = []

    def carried(outs, rider):
        if rider is None:
            return outs
        att_parts.append(outs[-1])
        return outs[:-1]

    p_conv_a, p_conv_b, p_lru, p_k, p_v = [], [], [], [], []

    def p_even(x, e, post):
        rider = riders.pop(0) if riders else None
        u, br, gb = carried(_even_in(x, mix_g[2 * e], even_w_in_b[e], rider), rider)
        u3, br3, gb3 = (a.reshape(bsz, seq, c) for a in (u, br, gb))
        ya = _conv_a(u3, conv_a_w[e], conv_a_b[e], conv_a_ln_g[e], conv_a_ln_b[e])
        yb, h_last = _lru(br3, gb3, conv_b_w[e], conv_b_b[e], lru_wg[e], lru_bg[e], lru_lambda[e])
        p_conv_a.append(u3[:, seq - (CONV_A_WIDTH - 1):])
        p_conv_b.append(br3[:, seq - (CONV_B_WIDTH - 1):])
        p_lru.append(h_last.reshape(bsz, c))
        rider = riders.pop(0) if riders else None
        return carried(_outproj(x, [ya.reshape(-1, c), yb.reshape(-1, c)],
                                [even_w_out_b[e][:c], even_w_out_b[e][c:]], post, rider), rider)

    def p_odd(x, o, l, post):
        qt, k, v, kb, vt = _qkv(x, mix_g[l], attn_w_in_b[o], bsz, seq)
        p_k.append(k.reshape(bsz, seq, ATT_HEADS, ATT_VD))
        p_v.append(v.reshape(bsz, seq, ATT_HEADS, ATT_VD))
        att = _flash(qt, kb.reshape(bsz, seq, d), vt,
                     (lam_q1[o], lam_k1[o], lam_q2[o], lam_k2[o]), attn_subln_g[o], lam_inits[o])
        rider = riders.pop(0) if riders else None
        return carried(_outproj(x, [att.reshape(-1, d)], [attn_w_out_b[o]], post, rider), rider)

    def p_xattn(q, l):
        rider = riders.pop(0) if riders else None
        return carried(_xattn(q, p_mem_kb[l], p_mem_vb[l], rider), rider)[0]

    y_prompt = _trunk(x_prompt.reshape(bsz * seq, d), (bsz, seq), p_xattn, p, p_even, p_odd, range(depth))
    y_prompt = y_prompt.reshape(bsz, seq, d)
    assert not riders and len(att_parts) == RIDER_HOSTS

    att = jnp.concatenate(att_parts, axis=0).reshape(dbsz * dseq, d)
    post = (xattn_g[l_odd], p['xattn_w_q'][l_odd], XATT_HD ** -0.5)
    mixed = _outproj(xs, [att], [attn_w_out_b[0]], post)
    y_sample = _trunk(xs, (dbsz, dseq), s_xattn, p, s_even, None, range(l_odd, depth), mixed=tuple(mixed))
    y_sample = y_sample.reshape(dbsz, dseq, d)

    return (y_prompt, y_sample,
            _stack(p_conv_a), _stack(p_conv_b), _stack(p_lru),
            _stack(p_k), _stack(p_v),
            _stack(p_mem_k), _stack(p_mem_v),
            _stack(s_conv_a), _stack(s_conv_b), _stack(s_lru),
            _stack(s_k), _stack(s_v))
```

```python
import functools
import math

import jax
import jax.numpy as jnp
from jax import lax
from jax.experimental import pallas as pl
from jax.experimental.pallas import tpu as pltpu

F32 = jnp.float32
BF16 = jnp.bfloat16

D_MODEL = 1024
D_FF = 2816
CONV_CH = 512
CONV_A_WIDTH = 31
LRU_CH = 512
LRU_BLOCKS = 8
CONV_B_WIDTH = 4
LRU_C = 8.0
ATT_HEADS = 8
ATT_HD = 64
ATT_VD = 128
N_MEM = 256
XATT_HEADS = 4
XATT_HD = 256
PAGE_SIZE = 128
EPS = 1e-6
NEG_INF = -1e30

SUBLANES = 8
VMEM_LIMIT = 56 * 1024 * 1024

MXU_COLS = 256
FFN_CHUNK = 4 * MXU_COLS
XS_BATCH = 4
ROW_TILE = 512
SEQ_TILE_CONV = 512
SEQ_TILE_LRU = 256
ATT_TQ = 512
FFN_RIDER_SEQS = 4
OUT_RIDER_SEQS = 8


def _params(sem):
    return pltpu.CompilerParams(dimension_semantics=sem, vmem_limit_bytes=VMEM_LIMIT)


def _rms(x, g):
    return x * lax.rsqrt(jnp.mean(x * x, axis=-1, keepdims=True) + EPS) * g


def _dot(a, b):
    return jnp.dot(a, b, preferred_element_type=F32)


def _dot_nt(a, b):
    return lax.dot_general(a, b, (((1,), (1,)), ((), ())), preferred_element_type=F32)


def _row_tile(m):
    return ROW_TILE if m % ROW_TILE == 0 else m


def _store_heads(o_ref, y, n_heads):
    o_ref[...] = y.reshape(y.shape[0], n_heads, y.shape[1] // n_heads)


def _ffn_chunks(f):
    return [(c0, min(FFN_CHUNK, f - c0)) for c0 in range(0, f, FFN_CHUNK)]


def _ffn_body(*refs, final, pre):
    refs = list(refs)
    x_ref, g_ref, wi_ref, wo_ref = refs[:4]
    pos = 4
    if pre:
        y_ref, wp_ref = refs[pos:pos + 2]
        pos += 2
    if final:
        gf_ref = refs[pos]
        pos += 1
    o_ref = refs[pos]
    f = wo_ref.shape[0]
    x = x_ref[...]
    if pre:
        x = x + _dot(y_ref[...], wp_ref[...])
    xn = _rms(x, g_ref[...]).astype(BF16)
    acc = None
    for c0, cw in _ffn_chunks(f):
        gate = _dot(xn, wi_ref[:, c0:c0 + cw])
        up = _dot(xn, wi_ref[:, f + c0:f + c0 + cw])
        h = (gate * jax.nn.sigmoid(gate) * up).astype(BF16)
        part = _dot(h, wo_ref[c0:c0 + cw, :])
        acc = part if acc is None else acc + part
    y = x + 0.5 * acc
    if final:
        y = _rms(y, gf_ref[...])
    o_ref[...] = y


def _ffn(x, g, w_in, w_out, layer, final_g=None, pre=None, rider=None):
    m, d = x.shape
    f = w_out.shape[1]
    tm = _row_tile(m)
    final = final_g is not None
    const = lambda shape: pl.BlockSpec(shape, lambda i: (0, 0), pipeline_mode=pl.Buffered(1))
    of_layer = lambda w: pl.BlockSpec((None,) + w.shape[1:], lambda i: (layer, 0, 0), pipeline_mode=pl.Buffered(1))
    in_specs = [pl.BlockSpec((tm, d), lambda i: (i, 0)), const((1, d)), of_layer(w_in), of_layer(w_out)]
    args = [x, g.reshape(1, d), w_in, w_out]
    if pre is not None:
        y, wp = pre
        in_specs += [pl.BlockSpec((tm, y.shape[1]), lambda i: (i, 0)), of_layer(wp)]
        args += [y, wp]
    if final:
        in_specs.append(const((1, d)))
        args.append(final_g.reshape(1, d))
    return _call_with_rider(functools.partial(_ffn_body, final=final, pre=pre is not None), (m // tm,),
                            in_specs, args, [pl.BlockSpec((tm, d), lambda i: (i, 0))],
                            [jax.ShapeDtypeStruct((m, d), F32)], "ffn", rider)


def _proj_body(*refs, has_g, n_w, out_meta):
    x_ref = refs[0]
    pos = 1
    if has_g:
        g_ref = refs[pos]
        pos += 1
    w_refs = refs[pos:pos + n_w]
    o_refs = refs[pos + n_w:]
    x = x_ref[...]
    if has_g:
        x = _rms(x, g_ref[...])
    xb = x.astype(BF16)
    ys = [_dot(xb, w_ref[...]) for w_ref in w_refs]
    for o_ref, (wi, scale) in zip(o_refs, out_meta):
        y = ys[wi]
        if scale != 1.0:
            y = y * scale
        o_ref[...] = y.astype(o_ref.dtype)


def _proj(x, g, w, width, blocks, outs):
    m, k = x.shape
    tm = _row_tile(m)
    has_g = g is not None
    in_specs = [pl.BlockSpec((tm, k), lambda i: (i, 0))]
    args = [x]
    if has_g:
        in_specs.append(pl.BlockSpec((1, k), lambda i: (0, 0)))
        args.append(g.reshape(1, k))
    for b in blocks:
        in_specs.append(pl.BlockSpec((k, width), lambda i, b=b: (0, b)))
        args.append(w)
    return pl.pallas_call(
        functools.partial(_proj_body, has_g=has_g, n_w=len(blocks),
                          out_meta=tuple((wi, sc) for wi, _, sc in outs)),
        grid=(m // tm,),
        in_specs=in_specs,
        out_specs=[pl.BlockSpec((tm, width), lambda i: (i, 0)) for _ in outs],
        out_shape=[jax.ShapeDtypeStruct((m, width), dt) for _, dt, _ in outs],
        compiler_params=_params(("parallel",)),
        name="proj",
    )(*args)


def _outproj_body(*refs, n, post_scale):
    x_ref = refs[0]
    y_refs = refs[1:1 + n]
    w_refs = refs[1 + n:1 + 2 * n]
    pos = 1 + 2 * n
    if post_scale is not None:
        g_ref, wq_ref = refs[pos:pos + 2]
        pos += 2
    o_ref = refs[pos]
    acc = x_ref[...]
    for y_ref, w_ref in zip(y_refs, w_refs):
        acc = acc + _dot(y_ref[...], w_ref[...])
    o_ref[...] = acc
    if post_scale is not None:
        q_ref = refs[pos + 1]
        q = _dot(_rms(acc, g_ref[...]).astype(BF16), wq_ref[...]) * post_scale
        q_ref[...] = q.astype(q_ref.dtype)


def _outproj(x, ys, ws, post=None, rider=None):
    m, d = x.shape
    tm = _row_tile(m)
    n = len(ys)
    row = pl.BlockSpec((tm, d), lambda i: (i, 0))
    in_specs = [row]
    in_specs += [pl.BlockSpec((tm, y.shape[1]), lambda i: (i, 0)) for y in ys]
    in_specs += [pl.BlockSpec(w.shape, lambda i: (0, 0)) for w in ws]
    args = [x, *ys, *ws]
    out_specs = [row]
    out_shape = [jax.ShapeDtypeStruct((m, d), F32)]
    post_scale = None
    if post is not None:
        g, wq, post_scale = post
        in_specs += [pl.BlockSpec((1, d), lambda i: (0, 0)), pl.BlockSpec(wq.shape, lambda i: (0, 0))]
        args += [g.reshape(1, d), wq]
        out_specs.append(pl.BlockSpec((tm, wq.shape[1]), lambda i: (i, 0)))
        out_shape.append(jax.ShapeDtypeStruct((m, wq.shape[1]), BF16))
    return _call_with_rider(functools.partial(_outproj_body, n=n, post_scale=post_scale), (m // tm,),
                            in_specs, args, out_specs, out_shape, "outproj", rider)


def _even_in_body(x_ref, g_ref, w_ref, u_ref, br_ref, gb_ref):
    xn = _rms(x_ref[...], g_ref[...]).astype(BF16)
    z = _dot(xn, w_ref[...])
    c = CONV_CH
    u_ref[...] = z[:, :c] * jax.nn.sigmoid(z[:, c:2 * c])
    br_ref[...] = z[:, 2 * c:2 * c + LRU_CH]
    gb_ref[...] = jax.nn.gelu(z[:, 2 * c + LRU_CH:])


def _even_in(x, g, w, rider=None):
    m, d = x.shape
    tm = _row_tile(m)
    n = w.shape[1]
    in_specs = [pl.BlockSpec((tm, d), lambda i: (i, 0)),
                pl.BlockSpec((1, d), lambda i: (0, 0)),
                pl.BlockSpec((d, n), lambda i: (0, 0))]
    return _call_with_rider(_even_in_body, (m // tm,), in_specs, [x, g.reshape(1, d), w],
                            [pl.BlockSpec((tm, CONV_CH), lambda i: (i, 0)) for _ in range(3)],
                            [jax.ShapeDtypeStruct((m, CONV_CH), F32) for _ in range(3)], "even_in", rider)


def _layer_norm_silu(y, g, b):
    mu = jnp.mean(y, axis=-1, keepdims=True)
    yc = y - mu
    var = jnp.mean(yc * yc, axis=-1, keepdims=True)
    yn = yc * lax.rsqrt(var + EPS) * g + b
    return yn * jax.nn.sigmoid(yn)


CONV_HALO = 32
CONV_CHUNK = 32


def _conv_a_body(u_ref, prev_ref, cw_ref, cb_ref, lg_ref, lb_ref, o_ref, ext_ref, rot_ref, *, tt):
    i = pl.program_id(1)
    c = CONV_CH
    n_ext = tt + CONV_HALO

    @pl.when(i == 0)
    def _():
        ext_ref[0:CONV_HALO, :] = jnp.zeros((CONV_HALO, c), F32)

    @pl.when(i != 0)
    def _():
        ext_ref[0:CONV_HALO, :] = prev_ref[0]

    ext_ref[CONV_HALO:n_ext, :] = u_ref[0]
    ext_ref[n_ext:n_ext + SUBLANES, :] = jnp.zeros((SUBLANES, c), F32)
    for r in range(SUBLANES):
        rot_ref[r] = ext_ref[pl.ds(r, n_ext), :]
    first = CONV_HALO - (CONV_A_WIDTH - 1)
    cb = cb_ref[...]
    lg = lg_ref[...]
    lb = lb_ref[...]
    for r0 in range(0, tt, CONV_CHUNK):
        acc = jnp.zeros((CONV_CHUNK // SUBLANES, SUBLANES, c), F32)
        for w in range(CONV_A_WIDTH):
            s = first + w
            tap = rot_ref[s % SUBLANES, pl.ds(r0 + (s // SUBLANES) * SUBLANES, CONV_CHUNK), :]
            acc = acc + tap.reshape(acc.shape) * cw_ref[w * SUBLANES:(w + 1) * SUBLANES, :][None]
        y = acc.reshape(CONV_CHUNK, c) + cb
        o_ref[0, r0:r0 + CONV_CHUNK, :] = _layer_norm_silu(y, lg, lb).astype(o_ref.dtype)


def _conv_a(u, cw, cb, lg, lb):
    b, t, c = u.shape
    tt = SEQ_TILE_CONV
    halo_blocks = tt // CONV_HALO
    vec = lambda a: a.reshape(1, c)
    return pl.pallas_call(
        functools.partial(_conv_a_body, tt=tt),
        grid=(b, t // tt),
        in_specs=[pl.BlockSpec((1, tt, c), lambda bi, i: (bi, i, 0)),
                  pl.BlockSpec((1, CONV_HALO, c), lambda bi, i: (bi, jnp.maximum(i * halo_blocks - 1, 0), 0)),
                  pl.BlockSpec((CONV_A_WIDTH * SUBLANES, c), lambda bi, i: (0, 0)),
                  pl.BlockSpec((1, c), lambda bi, i: (0, 0)),
                  pl.BlockSpec((1, c), lambda bi, i: (0, 0)),
                  pl.BlockSpec((1, c), lambda bi, i: (0, 0))],
        out_specs=pl.BlockSpec((1, tt, c), lambda bi, i: (bi, i, 0)),
        out_shape=jax.ShapeDtypeStruct((b, t, c), BF16),
        scratch_shapes=[pltpu.VMEM((tt + CONV_HALO + SUBLANES, c), F32),
                        pltpu.VMEM((SUBLANES, tt + CONV_HALO, c), F32)],
        compiler_params=_params(("parallel", "arbitrary")),
        name="conv_a",
    )(u, u, jnp.repeat(cw, SUBLANES, axis=0), vec(cb), vec(lg), vec(lb))


def _lru_gates(xr, wg_ref, bg_ref, lam_ref):
    gates = _dot(xr.astype(BF16), wg_ref[...]) + bg_ref[...]
    r = jax.nn.sigmoid(gates[:, :LRU_CH])
    ig = jax.nn.sigmoid(gates[:, LRU_CH:])
    nl = -lam_ref[...]
    softplus = jnp.maximum(nl, 0.0) + jnp.log1p(jnp.exp(-jnp.abs(nl)))
    log_a = -LRU_C * r * softplus
    a = jnp.exp(log_a)
    beta = jnp.sqrt(jnp.maximum(-jnp.tanh(log_a) * (a * a + 1.0), 0.0))
    return a, beta * ig * xr


def _lru_body(br_ref, prev_ref, gb_ref, cw_ref, cb_ref, wg_ref, bg_ref, lam_ref,
              yb_ref, hl_ref, ext_ref, h_ref, *, tt):
    i = pl.program_id(1)
    c = LRU_CH

    @pl.when(i == 0)
    def _():
        h_ref[...] = jnp.zeros((1, c), F32)
        ext_ref[0:SUBLANES, :] = jnp.zeros((SUBLANES, c), F32)

    @pl.when(i != 0)
    def _():
        ext_ref[0:SUBLANES, :] = prev_ref[0]

    ext_ref[SUBLANES:SUBLANES + tt, :] = br_ref[0]
    first = SUBLANES - (CONV_B_WIDTH - 1)
    xr = cb_ref[...]
    for w in range(CONV_B_WIDTH):
        xr = xr + ext_ref[pl.ds(first + w, tt), :] * cw_ref[w:w + 1, :]
    a, u = _lru_gates(xr, wg_ref, bg_ref, lam_ref)

    row = lax.broadcasted_iota(jnp.int32, (tt, c), 0)
    d = 1
    while d < tt:
        if d < SUBLANES:
            keep = row >= d
            a_sh = jnp.where(keep, pltpu.roll(a, d, 0), 1.0)
            u_sh = jnp.where(keep, pltpu.roll(u, d, 0), 0.0)
        else:
            a_sh = jnp.concatenate([jnp.ones((d, c), F32), a[:tt - d]], axis=0)
            u_sh = jnp.concatenate([jnp.zeros((d, c), F32), u[:tt - d]], axis=0)
        u = a * u_sh + u
        a = a * a_sh
        d *= 2
    h = a * h_ref[...] + u
    yb_ref[0] = (h * gb_ref[0]).astype(yb_ref.dtype)
    h_last = h[tt - 1:tt, :]
    h_ref[...] = h_last
    hl_ref[0] = h_last


def _lru(br, gb, cw, cb, wg, bg, lam):
    b, t, c = br.shape
    tt = SEQ_TILE_LRU
    halo_blocks = tt // SUBLANES
    vec = lambda a: a.reshape(1, -1)
    return pl.pallas_call(
        functools.partial(_lru_body, tt=tt),
        grid=(b, t // tt),
        in_specs=[pl.BlockSpec((1, tt, c), lambda bi, i: (bi, i, 0)),
                  pl.BlockSpec((1, SUBLANES, c), lambda bi, i: (bi, jnp.maximum(i * halo_blocks - 1, 0), 0)),
                  pl.BlockSpec((1, tt, c), lambda bi, i: (bi, i, 0)),
                  pl.BlockSpec((CONV_B_WIDTH, c), lambda bi, i: (0, 0)),
                  pl.BlockSpec((1, c), lambda bi, i: (0, 0)),
                  pl.BlockSpec((c, 2 * c), lambda bi, i: (0, 0)),
                  pl.BlockSpec((1, 2 * c), lambda bi, i: (0, 0)),
                  pl.BlockSpec((1, c), lambda bi, i: (0, 0))],
        out_specs=[pl.BlockSpec((1, tt, c), lambda bi, i: (bi, i, 0)),
                   pl.BlockSpec((1, 1, c), lambda bi, i: (bi, 0, 0))],
        out_shape=[jax.ShapeDtypeStruct((b, t, c), BF16),
                   jax.ShapeDtypeStruct((b, 1, c), F32)],
        scratch_shapes=[pltpu.VMEM((tt + SUBLANES, c), F32), pltpu.VMEM((1, c), F32)],
        compiler_params=_params(("parallel", "arbitrary")),
        name="lru",
    )(br, br, gb, cw, vec(cb), wg, vec(bg), vec(lam))


def _sample_mix_body(u_ref, br_ref, gb_ref, ha_ref, hb_ref, h0_ref,
                     cwa_ref, cba_ref, lg_ref, lb_ref, cwb_ref, cbb_ref, wg_ref, bg_ref, lam_ref,
                     ya_ref, yb_ref, h_ref):
    u = u_ref[...]
    acc = cba_ref[...] + u * cwa_ref[CONV_A_WIDTH - 1:CONV_A_WIDTH, :]
    for w in range(CONV_A_WIDTH - 1):
        acc = acc + ha_ref[w] * cwa_ref[w:w + 1, :]
    ya_ref[...] = _layer_norm_silu(acc, lg_ref[...], lb_ref[...]).astype(ya_ref.dtype)

    br = br_ref[...]
    xr = cbb_ref[...] + br * cwb_ref[CONV_B_WIDTH - 1:CONV_B_WIDTH, :]
    for w in range(CONV_B_WIDTH - 1):
        xr = xr + hb_ref[w] * cwb_ref[w:w + 1, :]
    a, x_in = _lru_gates(xr, wg_ref, bg_ref, lam_ref)
    h = a * h0_ref[...] + x_in
    h_ref[...] = h
    yb_ref[...] = (h * gb_ref[...]).astype(yb_ref.dtype)


def _sample_mix(u, br, gb, hist_a_t, hist_b_t, h0, cwa, cba, lg, lb, cwb, cbb, wg, bg, lam):
    n, c = u.shape
    vec = lambda a: a.reshape(1, -1)
    args = [u, br, gb, hist_a_t, hist_b_t, h0, cwa, vec(cba), vec(lg), vec(lb), cwb, vec(cbb), wg, vec(bg), vec(lam)]
    full = lambda a: pl.BlockSpec(a.shape, lambda i, nd=a.ndim: (0,) * nd)
    return pl.pallas_call(
        _sample_mix_body,
        grid=(1,),
        in_specs=[full(a) for a in args],
        out_specs=[pl.BlockSpec((n, c), lambda i: (0, 0)) for _ in range(3)],
        out_shape=[jax.ShapeDtypeStruct((n, c), BF16), jax.ShapeDtypeStruct((n, c), BF16),
                   jax.ShapeDtypeStruct((n, c), F32)],
        compiler_params=_params(("arbitrary",)),
        name="sample_mix",
    )(*args)


def _diff_lambda(lq1_ref, lk1_ref, lq2_ref, lk2_ref, lam_init):
    e1 = jnp.exp(jnp.sum(lq1_ref[...] * lk1_ref[...], axis=-1, keepdims=True))
    e2 = jnp.exp(jnp.sum(lq2_ref[...] * lk2_ref[...], axis=-1, keepdims=True))
    return e1 - e2 + lam_init


Q_SCALE = ATT_HD ** -0.5 * math.log2(math.e)
VT_ROWS = ATT_VD + 16


def _qkv_body(x_ref, g_ref, wq_ref, wk_ref, wv_ref, qt_ref, k_ref, v_ref, kb_ref, vt_ref):
    xn = _rms(x_ref[...], g_ref[...]).astype(BF16)
    q = _dot(xn, wq_ref[...])
    k = _dot(xn, wk_ref[...])
    v = _dot(xn, wv_ref[...])
    qt_ref[0] = (q * Q_SCALE).T.astype(BF16)
    _store_heads(k_ref, k, ATT_HEADS)
    kb_ref[...] = k.astype(BF16)
    _store_heads(v_ref, v, ATT_HEADS)
    vt = v.T.astype(BF16)
    ones = jnp.ones((VT_ROWS - ATT_VD, vt.shape[1]), BF16)
    for h in range(ATT_HEADS):
        vt_ref[0, h * VT_ROWS:h * VT_ROWS + ATT_VD, :] = vt[h * ATT_VD:(h + 1) * ATT_VD, :]
        vt_ref[0, h * VT_ROWS + ATT_VD:(h + 1) * VT_ROWS, :] = ones


def _qkv(x, g, w, bsz, seq):
    m, d = x.shape
    tm = ROW_TILE
    per_seq = seq // tm
    row = pl.BlockSpec((tm, d), lambda i: (i, 0))
    heads = pl.BlockSpec((tm, ATT_HEADS, ATT_VD), lambda i: (i, 0, 0))
    col = lambda rows: pl.BlockSpec((1, rows, tm), lambda i: (i // per_seq, 0, i % per_seq))
    wspec = lambda b: pl.BlockSpec((d, d), lambda i: (0, b))
    return pl.pallas_call(
        _qkv_body,
        grid=(m // tm,),
        in_specs=[row, pl.BlockSpec((1, d), lambda i: (0, 0)), wspec(0), wspec(1), wspec(2)],
        out_specs=[col(d), heads, heads, row, col(ATT_HEADS * VT_ROWS)],
        out_shape=[jax.ShapeDtypeStruct((bsz, d, seq), BF16),
                   jax.ShapeDtypeStruct((m, ATT_HEADS, ATT_VD), F32),
                   jax.ShapeDtypeStruct((m, ATT_HEADS, ATT_VD), F32),
                   jax.ShapeDtypeStruct((m, d), BF16),
                   jax.ShapeDtypeStruct((bsz, ATT_HEADS * VT_ROWS, seq), BF16)],
        compiler_params=_params(("parallel",)),
        name="qkv",
    )(x, g.reshape(1, d), w, w, w)


def _flash_body(qi_ref, ki_ref, qt_ref, k_ref, vt_ref, lq1_ref, lk1_ref, lq2_ref, lk2_ref, g_ref, o_ref,
                qs_ref, m_ref, acc_ref, s_ref, *, tq, lam_init):
    step_id = pl.program_id(1)
    qi = qi_ref[step_id]
    ki = ki_ref[step_id]
    vd = ATT_VD

    @pl.when(ki == 0)
    def _():
        dim = lax.broadcasted_iota(jnp.int32, (vd, tq), 0)
        for h in range(ATT_HEADS):
            qh = qt_ref[0, h * vd:(h + 1) * vd, :]
            zero = jnp.zeros_like(qh)
            qs_ref[h, :, 0:tq] = jnp.where(dim < ATT_HD, qh, zero)
            qs_ref[h, :, tq:2 * tq] = jnp.where(dim >= ATT_HD, qh, zero)
        m_ref[...] = jnp.full(m_ref.shape, NEG_INF, F32)
        acc_ref[...] = jnp.zeros(acc_ref.shape, F32)

    def scores(h):
        s_ref[h % 2] = _dot(k_ref[0, :, h * vd:(h + 1) * vd], qs_ref[h])

    def step(diagonal):
        if diagonal:
            key = lax.broadcasted_iota(jnp.int32, (tq, 2 * tq), 0)
            qry = lax.broadcasted_iota(jnp.int32, (tq, 2 * tq), 1)
            visible = key <= jnp.where(qry >= tq, qry - tq, qry)
        scores(0)
        for h in range(ATT_HEADS):
            if h + 1 < ATT_HEADS:
                scores(h + 1)
            s = s_ref[h % 2]
            if diagonal:
                s = jnp.where(visible, s, NEG_INF)
            m_prev = m_ref[h]
            m_new = jnp.maximum(m_prev, jnp.max(s, axis=0, keepdims=True))
            alpha = jnp.exp2(m_prev - m_new)
            p = jnp.exp2(s - m_new).astype(BF16)
            vth = vt_ref[0, h * VT_ROWS:(h + 1) * VT_ROWS, :]
            acc_ref[h] = alpha * acc_ref[h] + _dot(vth, p)
            m_ref[h] = m_new

    @pl.when(ki < qi)
    def _():
        step(False)

    @pl.when(ki == qi)
    def _():
        step(True)
        lam = _diff_lambda(lq1_ref, lk1_ref, lq2_ref, lk2_ref, lam_init)
        g = g_ref[...]
        for h in range(ATT_HEADS):
            o = acc_ref[h, 0:vd, :] / acc_ref[h, vd:vd + 1, :]
            att = o[:, 0:tq] - lam * o[:, tq:2 * tq]
            ms = jnp.mean(att * att, axis=0, keepdims=True)
            att = att * lax.rsqrt(ms + EPS) * g * (1.0 - lam_init)
            o_ref[0, :, h * vd:(h + 1) * vd] = att.T.astype(o_ref.dtype)


def _flash(qt, k, vt, lam_p, g, lam_init):
    b, d, t = qt.shape
    tq = ATT_TQ
    nq = t // tq
    pairs = [(qi, ki) for qi in range(nq) for ki in range(qi + 1)]
    qi_tab = jnp.asarray([p[0] for p in pairs], jnp.int32)
    ki_tab = jnp.asarray([p[1] for p in pairs], jnp.int32)
    vec = lambda a: a.reshape(1, -1)
    small = lambda n: pl.BlockSpec((1, n), lambda bi, s, qi, ki: (0, 0))
    grid_spec = pltpu.PrefetchScalarGridSpec(
        num_scalar_prefetch=2,
        grid=(b, len(pairs)),
        in_specs=[pl.BlockSpec((1, d, tq), lambda bi, s, qi, ki: (bi, 0, qi[s])),
                  pl.BlockSpec((1, tq, d), lambda bi, s, qi, ki: (bi, ki[s], 0)),
                  pl.BlockSpec((1, ATT_HEADS * VT_ROWS, tq), lambda bi, s, qi, ki: (bi, 0, ki[s])),
                  small(ATT_HD), small(ATT_HD), small(ATT_HD), small(ATT_HD),
                  pl.BlockSpec((ATT_VD, 1), lambda bi, s, qi, ki: (0, 0))],
        out_specs=pl.BlockSpec((1, tq, d), lambda bi, s, qi, ki: (bi, qi[s], 0)),
        scratch_shapes=[pltpu.VMEM((ATT_HEADS, ATT_VD, 2 * tq), BF16),
                        pltpu.VMEM((ATT_HEADS, 1, 2 * tq), F32),
                        pltpu.VMEM((ATT_HEADS, VT_ROWS, 2 * tq), F32),
                        pltpu.VMEM((2, tq, 2 * tq), F32)],
    )
    return pl.pallas_call(
        functools.partial(_flash_body, tq=tq, lam_init=lam_init),
        grid_spec=grid_spec,
        out_shape=jax.ShapeDtypeStruct((b, t, d), BF16),
        compiler_params=_params(("parallel", "arbitrary")),
        name="flash_diff_attn",
    )(qi_tab, ki_tab, qt, k, vt, *[vec(a) for a in lam_p], g.reshape(ATT_VD, 1))


def _decode_step(j, last, refs_in, o_ref, scratch, lam_init, pp):
    q_ref, kn_ref, vn_ref, lq1_ref, lk1_ref, lq2_ref, lk2_ref, g_ref = refs_in[:8]
    k_refs = refs_in[8:8 + pp]
    v_refs = refs_in[8 + pp:8 + 2 * pp]
    qx_ref, m_ref, l_ref, acc_ref = scratch
    nh = ATT_HEADS
    rows = PAGE_SIZE * nh

    @pl.when(j == 0)
    def _():
        q8 = q_ref[0].astype(F32)
        lane = lax.broadcasted_iota(jnp.int32, (nh, ATT_VD), 1)
        qx = jnp.concatenate([jnp.where(lane < ATT_HD, q8, 0.0), jnp.where(lane >= ATT_HD, q8, 0.0)], axis=0)
        qx_ref[...] = qx.astype(BF16)
        kn = kn_ref[0].astype(BF16).astype(F32)
        vn = vn_ref[0].astype(BF16).astype(F32)
        m_ref[...] = jnp.sum(qx * jnp.concatenate([kn, kn], axis=0), axis=-1, keepdims=True)
        l_ref[...] = jnp.ones(l_ref.shape, F32)
        acc_ref[...] = jnp.concatenate([vn, vn], axis=0)

    qx = qx_ref[...]
    s = jnp.concatenate([_dot_nt(qx, k_ref[0].reshape(rows, ATT_VD).astype(BF16)) for k_ref in k_refs], axis=1)
    r = lax.broadcasted_iota(jnp.int32, s.shape, 0)
    c = lax.broadcasted_iota(jnp.int32, s.shape, 1)
    s = jnp.where(c % nh == r % nh, s, NEG_INF)
    m_prev = m_ref[...]
    m_new = jnp.maximum(m_prev, jnp.max(s, axis=-1, keepdims=True))
    alpha = jnp.exp2(m_prev - m_new)
    p = jnp.exp2(s - m_new)
    l_ref[...] = alpha * l_ref[...] + jnp.sum(p, axis=-1, keepdims=True)
    pb = p.astype(BF16)
    acc = alpha * acc_ref[...]
    for i, v_ref in enumerate(v_refs):
        acc = acc + _dot(pb[:, i * rows:(i + 1) * rows], v_ref[0].reshape(rows, ATT_VD).astype(BF16))
    acc_ref[...] = acc
    m_ref[...] = m_new

    @pl.when(j == last)
    def _():
        lam = _diff_lambda(lq1_ref, lk1_ref, lq2_ref, lk2_ref, lam_init)
        o = acc_ref[...] / l_ref[...]
        att = o[0:nh] - lam * o[nh:2 * nh]
        o_ref[0] = (_rms(att, g_ref[...]) * (1.0 - lam_init)).astype(o_ref.dtype)


def _linear_step(idx, grid):
    step = idx[0]
    for i, n in zip(idx[1:], grid[1:]):
        step = step * n + i
    return step


def _call_with_rider(host_body, grid, in_specs, args, out_specs, out_shape, name, rider):
    if rider is None:
        return pl.pallas_call(host_body, grid=grid, in_specs=in_specs, out_specs=out_specs, out_shape=out_shape,
                              compiler_params=_params(("parallel",) * len(grid)), name=name)(*args)
    nh, vd = ATT_HEADS, ATT_VD
    page_table = rider['page_table']
    n_seq, seq0, base = rider['n_seq'], rider['seq0'], rider['page_base']
    per_seq = math.prod(grid) // n_seq
    pp = page_table.shape[1] // per_seq
    assert math.prod(grid) == n_seq * per_seq and page_table.shape[1] == pp * per_seq
    n_in, n_out = len(in_specs), len(out_specs)
    host_in = [pl.BlockSpec(sp.block_shape, lambda *a, f=sp.index_map: f(*a[:-1]), pipeline_mode=sp.pipeline_mode)
               for sp in in_specs]
    host_out = [pl.BlockSpec(sp.block_shape, lambda *a, f=sp.index_map: f(*a[:-1])) for sp in out_specs]
    seq = lambda a: _linear_step(a[:-1], grid) // per_seq
    sub = lambda a: _linear_step(a[:-1], grid) % per_seq
    row_spec = pl.BlockSpec((1, nh, vd), lambda *a: (seq0 + seq(a), 0, 0))
    small = lambda n: pl.BlockSpec((1, n), lambda *a: (0, 0))
    page_specs = [pl.BlockSpec((1, PAGE_SIZE, nh, vd),
                               lambda *a, i=i: (base + a[-1][seq0 + seq(a), sub(a) * pp + i], 0, 0, 0))
                  for i in range(pp)]
    rider_in = [row_spec, row_spec, row_spec, small(ATT_HD), small(ATT_HD), small(ATT_HD), small(ATT_HD), small(vd)]
    rider_in += page_specs + page_specs
    vec = lambda a: a.reshape(1, -1)
    rider_args = [rider['q'], rider['k_new'], rider['v_new'], *[vec(a) for a in rider['lam_p']], vec(rider['g']),
                  *([rider['cache_k']] * pp), *([rider['cache_v']] * pp)]

    def body(pt_ref, *refs):
        host_body(*refs[:n_in], *refs[n_in + len(rider_in):n_in + len(rider_in) + n_out])
        step = _linear_step([pl.program_id(i) for i in range(len(grid))], grid)
        _decode_step(step % per_seq, per_seq - 1, refs[n_in:n_in + len(rider_in)],
                     refs[n_in + len(rider_in) + n_out], refs[n_in + len(rider_in) + n_out + 1:], rider['lam_init'], pp)

    grid_spec = pltpu.PrefetchScalarGridSpec(
        num_scalar_prefetch=1,
        grid=grid,
        in_specs=host_in + rider_in,
        out_specs=host_out + [pl.BlockSpec((1, nh, vd), lambda *a: (seq(a), 0, 0))],
        scratch_shapes=[pltpu.VMEM((2 * nh, vd), BF16),
                        pltpu.VMEM((2 * nh, 1), F32),
                        pltpu.VMEM((2 * nh, 1), F32),
                        pltpu.VMEM((2 * nh, vd), F32)],
    )
    return pl.pallas_call(
        body,
        grid_spec=grid_spec,
        out_shape=list(out_shape) + [jax.ShapeDtypeStruct((n_seq, nh, vd), BF16)],
        compiler_params=_params(("arbitrary",) * len(grid)),
        name=name + "_paged",
    )(page_table, *args, *rider_args)


def _xattn_head(q, k, v):
    s = _dot_nt(q, k)
    p = jnp.exp(s - jnp.max(s, axis=-1, keepdims=True))
    p = p / jnp.sum(p, axis=-1, keepdims=True)
    return _dot(p.astype(BF16), v)


def _xattn_body(q_ref, k_ref, v_ref, o_ref):
    hd = XATT_HD
    heads = [slice(h * hd, (h + 1) * hd) for h in range(XATT_HEADS)]
    scores = [_dot_nt(q_ref[0, :, c], k_ref[0, :, c]) for c in heads]
    for c, s in zip(heads, scores):
        p = jnp.exp(s - jnp.max(s, axis=-1, keepdims=True))
        l = jnp.sum(p, axis=-1, keepdims=True)
        o_ref[0, :, c] = (_dot(p.astype(BF16), v_ref[0, :, c]) / l).astype(o_ref.dtype)


def _xattn(q, mem_k, mem_v, rider=None):
    b, t, d = q.shape
    tq = ROW_TILE
    kv_spec = pl.BlockSpec((1, N_MEM, d), lambda bi, i: (bi, 0, 0))
    q_spec = pl.BlockSpec((1, tq, d), lambda bi, i: (bi, i, 0))
    return _call_with_rider(_xattn_body, (b, t // tq), [q_spec, kv_spec, kv_spec], [q, mem_k, mem_v],
                            [q_spec], [jax.ShapeDtypeStruct((b, t, d), BF16)], "xattn", rider)


def _xattn_cached_body(q_ref, k_ref, v_ref, o_ref):
    nh, hd = XATT_HEADS, XATT_HD
    rows = N_MEM * nh
    head = lax.broadcasted_iota(jnp.int32, (nh, rows), 0)
    col = lax.broadcasted_iota(jnp.int32, (nh, rows), 1)
    own = col % nh == head
    for i in range(XS_BATCH):
        q = q_ref[i]
        qx = jnp.concatenate([q[:, h * hd:(h + 1) * hd] for h in range(nh)], axis=0)
        k = k_ref[0, i].reshape(rows, hd).astype(BF16)
        v = v_ref[0, i].reshape(rows, hd).astype(BF16)
        s = jnp.where(own, _dot_nt(qx, k), NEG_INF)
        p = jnp.exp(s - jnp.max(s, axis=-1, keepdims=True))
        p = p / jnp.sum(p, axis=-1, keepdims=True)
        o = _dot(p.astype(BF16), v)
        for h in range(nh):
            o_ref[i, :, h * hd:(h + 1) * hd] = o[h:h + 1, :].astype(o_ref.dtype)


def _xattn_cached(q, cache_k, cache_v, layer):
    b, t, d = q.shape
    bb = XS_BATCH
    kv_spec = pl.BlockSpec((1, bb, N_MEM, XATT_HEADS, XATT_HD), lambda i: (layer, i, 0, 0, 0))
    return pl.pallas_call(
        _xattn_cached_body,
        grid=(b // bb,),
        in_specs=[pl.BlockSpec((bb, t, d), lambda i: (i, 0, 0)), kv_spec, kv_spec],
        out_specs=pl.BlockSpec((bb, t, d), lambda i: (i, 0, 0)),
        out_shape=jax.ShapeDtypeStruct((b, t, d), BF16),
        compiler_params=_params(("parallel",)),
        name="xattn_cached",
    )(q, cache_k, cache_v)


def _memkv_body(x_ref, wk_ref, wv_ref, kn_ref, vn_ref, kb_ref, vb_ref):
    xb = x_ref[...].astype(BF16)
    for w_ref, n_ref, b_ref in ((wk_ref, kn_ref, kb_ref), (wv_ref, vn_ref, vb_ref)):
        y = _dot(xb, w_ref[...])
        b_ref[...] = y.astype(BF16)
        _store_heads(n_ref, y, XATT_HEADS)


def _memkv(x, w):
    m, d = x.shape
    tm = _row_tile(m)
    row = pl.BlockSpec((tm, d), lambda i: (i, 0))
    heads = pl.BlockSpec((tm, XATT_HEADS, XATT_HD), lambda i: (i, 0, 0))
    return pl.pallas_call(
        _memkv_body,
        grid=(m // tm,),
        in_specs=[row, pl.BlockSpec((d, d), lambda i: (0, 0)), pl.BlockSpec((d, d), lambda i: (0, 1))],
        out_specs=[heads, heads, row, row],
        out_shape=[jax.ShapeDtypeStruct((m, XATT_HEADS, XATT_HD), F32),
                   jax.ShapeDtypeStruct((m, XATT_HEADS, XATT_HD), F32),
                   jax.ShapeDtypeStruct((m, d), BF16),
                   jax.ShapeDtypeStruct((m, d), BF16)],
        compiler_params=_params(("parallel",)),
        name="memkv",
    )(x, w, w)


def _block_diag(w):
    nb, n, _ = w.shape
    eye = jnp.eye(nb, dtype=w.dtype)
    return jnp.einsum('kij,kl->kilj', w, eye).reshape(nb * n, nb * n)


def _stack(arrays):
    return arrays[0][None] if len(arrays) == 1 else jnp.stack(arrays)


def _trunk(x, bt, xattn, p, even_mix, odd_mix, layers, mixed=None, ffn=None):
    d = x.shape[1]
    depth = p['ffn1_g'].shape[0]
    ffn = ffn or (lambda *a, **k: _ffn(*a, **k)[0])
    for l in layers:
        if mixed is None:
            x = ffn(x, p['ffn1_g'][l], p['ffn1_w_in'], p['ffn1_w_out'], l)
            post = (p['xattn_g'][l], p['xattn_w_q'][l], XATT_HD ** -0.5)
            x, q = even_mix(x, l // 2, post) if l % 2 == 0 else odd_mix(x, l // 2, l, post)
        else:
            x, q = mixed
            mixed = None
        o = xattn(q.reshape(*bt, d), l)
        last = l == depth - 1
        x = ffn(x, p['ffn2_g'][l], p['ffn2_w_in'], p['ffn2_w_out'], l,
                final_g=p['final_g'] if last else None, pre=(o.reshape(-1, d), p['xattn_w_out']))
    return x


def kernel(x_prompt, x_sample, state_conv_a, state_conv_b, state_lru, cache_k, cache_v, cache_mem_k, cache_mem_v, page_table, mem_prompt, ffn1_g, ffn1_w_in, ffn1_w_out, mix_g, even_w_in, conv_a_w, conv_a_b, conv_a_ln_g, conv_a_ln_b, conv_b_w, conv_b_b, lru_w_a, lru_b_a, lru_w_x, lru_b_x, lru_lambda, even_w_out, attn_w_in, lam_q1, lam_k1, lam_q2, lam_k2, attn_subln_g, attn_w_out, xattn_g, xattn_w_q, xattn_w_kv, xattn_w_out, ffn2_g, ffn2_w_in, ffn2_w_out, final_g):
    bsz, seq, d = x_prompt.shape
    dbsz, dseq, _ = x_sample.shape
    depth = ffn1_g.shape[0]
    n_even = even_w_in.shape[0]
    n_odd = attn_w_in.shape[0]
    bf = lambda w: w.astype(BF16)
    p = {
        'ffn1_g': ffn1_g, 'ffn1_w_in': bf(ffn1_w_in), 'ffn1_w_out': bf(ffn1_w_out),
        'ffn2_g': ffn2_g, 'ffn2_w_in': bf(ffn2_w_in), 'ffn2_w_out': bf(ffn2_w_out),
        'xattn_g': xattn_g, 'xattn_w_q': bf(xattn_w_q), 'xattn_w_out': bf(xattn_w_out),
        'final_g': final_g,
    }
    even_w_in_b = bf(even_w_in)
    even_w_out_b = bf(even_w_out)
    attn_w_in_b = bf(attn_w_in)
    attn_w_out_b = bf(attn_w_out)
    xattn_w_kv_b = bf(xattn_w_kv)
    lru_wg = [bf(jnp.concatenate([_block_diag(lru_w_a[e]), _block_diag(lru_w_x[e])], axis=1)) for e in range(n_even)]
    lru_bg = [jnp.concatenate([lru_b_a[e], lru_b_x[e]]) for e in range(n_even)]
    c = CONV_CH
    lam_inits = [0.8 - 0.6 * math.exp(-0.3 * (2 * o + 1)) for o in range(n_odd)]

    mem2 = mem_prompt.reshape(bsz * N_MEM, d)
    p_mem_k, p_mem_v, p_mem_kb, p_mem_vb = [], [], [], []
    for l in range(depth):
        mk, mv, mkb, mvb = _memkv(mem2, xattn_w_kv_b[l])
        p_mem_k.append(mk.reshape(bsz, N_MEM, XATT_HEADS, XATT_HD))
        p_mem_v.append(mv.reshape(bsz, N_MEM, XATT_HEADS, XATT_HD))
        p_mem_kb.append(mkb.reshape(bsz, N_MEM, d))
        p_mem_vb.append(mvb.reshape(bsz, N_MEM, d))

    assert n_odd == 1, "the paged attention is carried by the prompt kernels of a single odd layer"
    l_odd = 1
    s_conv_a, s_conv_b, s_lru, s_k, s_v = [], [], [], [], []
    pool = cache_k.shape[1]
    pages_k = cache_k.reshape(n_odd * pool, PAGE_SIZE, ATT_HEADS, ATT_VD)
    pages_v = cache_v.reshape(n_odd * pool, PAGE_SIZE, ATT_HEADS, ATT_VD)

    def s_even(x, e, post):
        u, br, gb = _even_in(x, mix_g[2 * e], even_w_in_b[e])
        ya, yb, h_new = _sample_mix(
            u, br, gb, jnp.swapaxes(state_conv_a[e], 0, 1), jnp.swapaxes(state_conv_b[e], 0, 1), state_lru[e],
            conv_a_w[e], conv_a_b[e], conv_a_ln_g[e], conv_a_ln_b[e],
            conv_b_w[e], conv_b_b[e], lru_wg[e], lru_bg[e], lru_lambda[e])
        s_conv_a.append(jnp.concatenate([state_conv_a[e][:, 1:], u[:, None, :]], axis=1))
        s_conv_b.append(jnp.concatenate([state_conv_b[e][:, 1:], br[:, None, :]], axis=1))
        s_lru.append(h_new)
        return _outproj(x, [ya, yb], [even_w_out_b[e][:c], even_w_out_b[e][c:]], post)

    s_xattn = lambda q, l: _xattn_cached(q, cache_mem_k, cache_mem_v, l)
    xs = _trunk(x_sample.reshape(dbsz * dseq, d), (dbsz, dseq), s_xattn, p, s_even, None, range(l_odd))
    (xs,) = _ffn(xs, ffn1_g[l_odd], p['ffn1_w_in'], p['ffn1_w_out'], l_odd)
    sq, sk, sv = _proj(xs, mix_g[l_odd], attn_w_in_b[0], d, [0, 1, 2],
                       [(0, BF16, Q_SCALE), (1, F32, 1.0), (2, F32, 1.0)])
    s_k.append(sk.reshape(dbsz, dseq, ATT_HEADS, ATT_VD))
    s_v.append(sv.reshape(dbsz, dseq, ATT_HEADS, ATT_VD))
    heads = (dbsz, ATT_HEADS, ATT_VD)
    hosts = []
    for l in range(depth):
        hosts += [('ffn', FFN_RIDER_SEQS), ('out', OUT_RIDER_SEQS), ('ffn', FFN_RIDER_SEQS)]
    assert sum(n for _, n in hosts) == dbsz
    riders = {'ffn': [], 'out': []}
    seq0 = 0
    for kind, n_seq in hosts:
        riders[kind].append(dict(
            seq0=seq0, n_seq=n_seq, q=sq.reshape(heads), k_new=sk.reshape(heads), v_new=sv.reshape(heads),
            cache_k=pages_k, cache_v=pages_v, page_base=0, page_table=page_table,
            lam_p=(lam_q1[0], lam_k1[0], lam_q2[0], lam_k2[0]), g=attn_subln_g[0], lam_init=lam_inits[0]))
        seq0 += n_seq
    att_parts = []

    def carried(outs, rider):
        if rider is None:
            return outs
        att_parts.append(outs[-1])
        return outs[:-1]

    p_conv_a, p_conv_b, p_lru, p_k, p_v = [], [], [], [], []

    def p_ffn(*a, **k):
        rider = riders['ffn'].pop(0)
        return carried(_ffn(*a, rider=rider, **k), rider)[0]

    def p_even(x, e, post):
        u, br, gb = _even_in(x, mix_g[2 * e], even_w_in_b[e])
        u3, br3, gb3 = (a.reshape(bsz, seq, c) for a in (u, br, gb))
        ya = _conv_a(u3, conv_a_w[e], conv_a_b[e], conv_a_ln_g[e], conv_a_ln_b[e])
        yb, h_last = _lru(br3, gb3, conv_b_w[e], conv_b_b[e], lru_wg[e], lru_bg[e], lru_lambda[e])
        p_conv_a.append(u3[:, seq - (CONV_A_WIDTH - 1):])
        p_conv_b.append(br3[:, seq - (CONV_B_WIDTH - 1):])
        p_lru.append(h_last.reshape(bsz, c))
        rider = riders['out'].pop(0)
        return carried(_outproj(x, [ya.reshape(-1, c), yb.reshape(-1, c)],
                                [even_w_out_b[e][:c], even_w_out_b[e][c:]], post, rider), rider)

    def p_odd(x, o, l, post):
        qt, k, v, kb, vt = _qkv(x, mix_g[l], attn_w_in_b[o], bsz, seq)
        p_k.append(k.reshape(bsz, seq, ATT_HEADS, ATT_VD))
        p_v.append(v.reshape(bsz, seq, ATT_HEADS, ATT_VD))
        att = _flash(qt, kb.reshape(bsz, seq, d), vt,
                     (lam_q1[o], lam_k1[o], lam_q2[o], lam_k2[o]), attn_subln_g[o], lam_inits[o])
        rider = riders['out'].pop(0)
        return carried(_outproj(x, [att.reshape(-1, d)], [attn_w_out_b[o]], post, rider), rider)

    p_xattn = lambda q, l: _xattn(q, p_mem_kb[l], p_mem_vb[l])[0]
    y_prompt = _trunk(x_prompt.reshape(bsz * seq, d), (bsz, seq), p_xattn, p, p_even, p_odd, range(depth), ffn=p_ffn)
    y_prompt = y_prompt.reshape(bsz, seq, d)
    assert not riders['ffn'] and not riders['out'] and len(att_parts) == len(hosts)

    att = jnp.concatenate(att_parts, axis=0).reshape(dbsz * dseq, d)
    post = (xattn_g[l_odd], p['xattn_w_q'][l_odd], XATT_HD ** -0.5)
    mixed = _outproj(xs, [att], [attn_w_out_b[0]], post)
    y_sample = _trunk(xs, (dbsz, dseq), s_xattn, p, s_even, None, range(l_odd, depth), mixed=tuple(mixed))
    y_sample = y_sample.reshape(dbsz, dseq, d)

    return (y_prompt, y_sample,
            _stack(p_conv_a), _stack(p_conv_b), _stack(p_lru),
            _stack(p_k), _stack(p_v),
            _stack(p_mem_k), _stack(p_mem_v),
            _stack(s_conv_a), _stack(s_conv_b), _stack(s_lru),
            _stack(s_k), _stack(s_v))
```

```python
import functools
import math

import jax
import jax.numpy as jnp
from jax import lax
from jax.experimental import pallas as pl
from jax.experimental.pallas import tpu as pltpu

F32 = jnp.float32
BF16 = jnp.bfloat16

D_MODEL = 1024
D_FF = 2816
CONV_CH = 512
CONV_A_WIDTH = 31
LRU_CH = 512
LRU_BLOCKS = 8
CONV_B_WIDTH = 4
LRU_C = 8.0
ATT_HEADS = 8
ATT_HD = 64
ATT_VD = 128
N_MEM = 256
XATT_HEADS = 4
XATT_HD = 256
PAGE_SIZE = 128
EPS = 1e-6
NEG_INF = -1e30

SUBLANES = 8
VMEM_LIMIT = 56 * 1024 * 1024

MXU_COLS = 256
FFN_CHUNK = 4 * MXU_COLS
XS_BATCH = 4
ROW_TILE = 512
SEQ_TILE_CONV = 512
SEQ_TILE_LRU = 256
ATT_TQ = 512
RIDER_SEQS = 4


def _params(sem):
    return pltpu.CompilerParams(dimension_semantics=sem, vmem_limit_bytes=VMEM_LIMIT)


def _rms(x, g):
    return x * lax.rsqrt(jnp.mean(x * x, axis=-1, keepdims=True) + EPS) * g


def _dot(a, b):
    return jnp.dot(a, b, preferred_element_type=F32)


def _dot_nt(a, b):
    return lax.dot_general(a, b, (((1,), (1,)), ((), ())), preferred_element_type=F32)


def _row_tile(m):
    return ROW_TILE if m % ROW_TILE == 0 else m


def _store_heads(o_ref, y, n_heads):
    o_ref[...] = y.reshape(y.shape[0], n_heads, y.shape[1] // n_heads)


def _ffn_chunks(f):
    return [(c0, min(FFN_CHUNK, f - c0)) for c0 in range(0, f, FFN_CHUNK)]


def _ffn_body(*refs, final, pre):
    refs = list(refs)
    x_ref, g_ref, wi_ref, wo_ref = refs[:4]
    pos = 4
    if pre:
        y_ref, wp_ref = refs[pos:pos + 2]
        pos += 2
    if final:
        gf_ref = refs[pos]
        pos += 1
    o_ref = refs[pos]
    f = wo_ref.shape[0]
    x = x_ref[...]
    if pre:
        x = x + _dot(y_ref[...], wp_ref[...])
    xn = _rms(x, g_ref[...]).astype(BF16)
    acc = None
    for c0, cw in _ffn_chunks(f):
        gate = _dot(xn, wi_ref[:, c0:c0 + cw])
        up = _dot(xn, wi_ref[:, f + c0:f + c0 + cw])
        h = (gate * jax.nn.sigmoid(gate) * up).astype(BF16)
        part = _dot(h, wo_ref[c0:c0 + cw, :])
        acc = part if acc is None else acc + part
    y = x + 0.5 * acc
    if final:
        y = _rms(y, gf_ref[...])
    o_ref[...] = y


def _ffn(x, g, w_in, w_out, layer, final_g=None, pre=None, rider=None):
    m, d = x.shape
    f = w_out.shape[1]
    tm = _row_tile(m)
    final = final_g is not None
    const = lambda shape: pl.BlockSpec(shape, lambda i: (0, 0), pipeline_mode=pl.Buffered(1))
    of_layer = lambda w: pl.BlockSpec((None,) + w.shape[1:], lambda i: (layer, 0, 0), pipeline_mode=pl.Buffered(1))
    in_specs = [pl.BlockSpec((tm, d), lambda i: (i, 0)), const((1, d)), of_layer(w_in), of_layer(w_out)]
    args = [x, g.reshape(1, d), w_in, w_out]
    if pre is not None:
        y, wp = pre
        in_specs += [pl.BlockSpec((tm, y.shape[1]), lambda i: (i, 0)), of_layer(wp)]
        args += [y, wp]
    if final:
        in_specs.append(const((1, d)))
        args.append(final_g.reshape(1, d))
    return _call_with_rider(functools.partial(_ffn_body, final=final, pre=pre is not None), (m // tm,),
                            in_specs, args, [pl.BlockSpec((tm, d), lambda i: (i, 0))],
                            [jax.ShapeDtypeStruct((m, d), F32)], "ffn", rider)


def _proj_body(*refs, has_g, n_w, out_meta):
    x_ref = refs[0]
    pos = 1
    if has_g:
        g_ref = refs[pos]
        pos += 1
    w_refs = refs[pos:pos + n_w]
    o_refs = refs[pos + n_w:]
    x = x_ref[...]
    if has_g:
        x = _rms(x, g_ref[...])
    xb = x.astype(BF16)
    ys = [_dot(xb, w_ref[...]) for w_ref in w_refs]
    for o_ref, (wi, scale) in zip(o_refs, out_meta):
        y = ys[wi]
        if scale != 1.0:
            y = y * scale
        o_ref[...] = y.astype(o_ref.dtype)


def _proj(x, g, w, width, blocks, outs):
    m, k = x.shape
    tm = _row_tile(m)
    has_g = g is not None
    in_specs = [pl.BlockSpec((tm, k), lambda i: (i, 0))]
    args = [x]
    if has_g:
        in_specs.append(pl.BlockSpec((1, k), lambda i: (0, 0)))
        args.append(g.reshape(1, k))
    for b in blocks:
        in_specs.append(pl.BlockSpec((k, width), lambda i, b=b: (0, b)))
        args.append(w)
    return pl.pallas_call(
        functools.partial(_proj_body, has_g=has_g, n_w=len(blocks),
                          out_meta=tuple((wi, sc) for wi, _, sc in outs)),
        grid=(m // tm,),
        in_specs=in_specs,
        out_specs=[pl.BlockSpec((tm, width), lambda i: (i, 0)) for _ in outs],
        out_shape=[jax.ShapeDtypeStruct((m, width), dt) for _, dt, _ in outs],
        compiler_params=_params(("parallel",)),
        name="proj",
    )(*args)


def _outproj_body(*refs, n, post_scale):
    x_ref = refs[0]
    y_refs = refs[1:1 + n]
    w_refs = refs[1 + n:1 + 2 * n]
    pos = 1 + 2 * n
    if post_scale is not None:
        g_ref, wq_ref = refs[pos:pos + 2]
        pos += 2
    o_ref = refs[pos]
    acc = x_ref[...]
    for y_ref, w_ref in zip(y_refs, w_refs):
        acc = acc + _dot(y_ref[...], w_ref[...])
    o_ref[...] = acc
    if post_scale is not None:
        q_ref = refs[pos + 1]
        q = _dot(_rms(acc, g_ref[...]).astype(BF16), wq_ref[...]) * post_scale
        q_ref[...] = q.astype(q_ref.dtype)


def _outproj(x, ys, ws, post=None, rider=None):
    m, d = x.shape
    tm = _row_tile(m)
    n = len(ys)
    row = pl.BlockSpec((tm, d), lambda i: (i, 0))
    in_specs = [row]
    in_specs += [pl.BlockSpec((tm, y.shape[1]), lambda i: (i, 0)) for y in ys]
    in_specs += [pl.BlockSpec(w.shape, lambda i: (0, 0)) for w in ws]
    args = [x, *ys, *ws]
    out_specs = [row]
    out_shape = [jax.ShapeDtypeStruct((m, d), F32)]
    post_scale = None
    if post is not None:
        g, wq, post_scale = post
        in_specs += [pl.BlockSpec((1, d), lambda i: (0, 0)), pl.BlockSpec(wq.shape, lambda i: (0, 0))]
        args += [g.reshape(1, d), wq]
        out_specs.append(pl.BlockSpec((tm, wq.shape[1]), lambda i: (i, 0)))
        out_shape.append(jax.ShapeDtypeStruct((m, wq.shape[1]), BF16))
    return _call_with_rider(functools.partial(_outproj_body, n=n, post_scale=post_scale), (m // tm,),
                            in_specs, args, out_specs, out_shape, "outproj", rider)


def _even_in_body(x_ref, g_ref, w_ref, u_ref, br_ref, gb_ref):
    xn = _rms(x_ref[...], g_ref[...]).astype(BF16)
    z = _dot(xn, w_ref[...])
    c = CONV_CH
    u_ref[...] = z[:, :c] * jax.nn.sigmoid(z[:, c:2 * c])
    br_ref[...] = z[:, 2 * c:2 * c + LRU_CH]
    gb_ref[...] = jax.nn.gelu(z[:, 2 * c + LRU_CH:])


def _even_in(x, g, w, rider=None):
    m, d = x.shape
    tm = _row_tile(m)
    n = w.shape[1]
    in_specs = [pl.BlockSpec((tm, d), lambda i: (i, 0)),
                pl.BlockSpec((1, d), lambda i: (0, 0)),
                pl.BlockSpec((d, n), lambda i: (0, 0))]
    return _call_with_rider(_even_in_body, (m // tm,), in_specs, [x, g.reshape(1, d), w],
                            [pl.BlockSpec((tm, CONV_CH), lambda i: (i, 0)) for _ in range(3)],
                            [jax.ShapeDtypeStruct((m, CONV_CH), F32) for _ in range(3)], "even_in", rider)


def _layer_norm_silu(y, g, b):
    mu = jnp.mean(y, axis=-1, keepdims=True)
    yc = y - mu
    var = jnp.mean(yc * yc, axis=-1, keepdims=True)
    yn = yc * lax.rsqrt(var + EPS) * g + b
    return yn * jax.nn.sigmoid(yn)


CONV_HALO = 32
CONV_CHUNK = 32


def _conv_a_body(u_ref, prev_ref, cw_ref, cb_ref, lg_ref, lb_ref, o_ref, ext_ref, rot_ref, *, tt):
    i = pl.program_id(1)
    c = CONV_CH
    n_ext = tt + CONV_HALO

    @pl.when(i == 0)
    def _():
        ext_ref[0:CONV_HALO, :] = jnp.zeros((CONV_HALO, c), F32)

    @pl.when(i != 0)
    def _():
        ext_ref[0:CONV_HALO, :] = prev_ref[0]

    ext_ref[CONV_HALO:n_ext, :] = u_ref[0]
    ext_ref[n_ext:n_ext + SUBLANES, :] = jnp.zeros((SUBLANES, c), F32)
    for r in range(SUBLANES):
        rot_ref[r] = ext_ref[pl.ds(r, n_ext), :]
    first = CONV_HALO - (CONV_A_WIDTH - 1)
    cb = cb_ref[...]
    lg = lg_ref[...]
    lb = lb_ref[...]
    for r0 in range(0, tt, CONV_CHUNK):
        acc = jnp.zeros((CONV_CHUNK // SUBLANES, SUBLANES, c), F32)
        for w in range(CONV_A_WIDTH):
            s = first + w
            tap = rot_ref[s % SUBLANES, pl.ds(r0 + (s // SUBLANES) * SUBLANES, CONV_CHUNK), :]
            acc = acc + tap.reshape(acc.shape) * cw_ref[w * SUBLANES:(w + 1) * SUBLANES, :][None]
        y = acc.reshape(CONV_CHUNK, c) + cb
        o_ref[0, r0:r0 + CONV_CHUNK, :] = _layer_norm_silu(y, lg, lb).astype(o_ref.dtype)


def _conv_a(u, cw, cb, lg, lb):
    b, t, c = u.shape
    tt = SEQ_TILE_CONV
    halo_blocks = tt // CONV_HALO
    vec = lambda a: a.reshape(1, c)
    return pl.pallas_call(
        functools.partial(_conv_a_body, tt=tt),
        grid=(b, t // tt),
        in_specs=[pl.BlockSpec((1, tt, c), lambda bi, i: (bi, i, 0)),
                  pl.BlockSpec((1, CONV_HALO, c), lambda bi, i: (bi, jnp.maximum(i * halo_blocks - 1, 0), 0)),
                  pl.BlockSpec((CONV_A_WIDTH * SUBLANES, c), lambda bi, i: (0, 0)),
                  pl.BlockSpec((1, c), lambda bi, i: (0, 0)),
                  pl.BlockSpec((1, c), lambda bi, i: (0, 0)),
                  pl.BlockSpec((1, c), lambda bi, i: (0, 0))],
        out_specs=pl.BlockSpec((1, tt, c), lambda bi, i: (bi, i, 0)),
        out_shape=jax.ShapeDtypeStruct((b, t, c), BF16),
        scratch_shapes=[pltpu.VMEM((tt + CONV_HALO + SUBLANES, c), F32),
                        pltpu.VMEM((SUBLANES, tt + CONV_HALO, c), F32)],
        compiler_params=_params(("parallel", "arbitrary")),
        name="conv_a",
    )(u, u, jnp.repeat(cw, SUBLANES, axis=0), vec(cb), vec(lg), vec(lb))


def _lru_gates(xr, wg_ref, bg_ref, lam_ref):
    gates = _dot(xr.astype(BF16), wg_ref[...]) + bg_ref[...]
    r = jax.nn.sigmoid(gates[:, :LRU_CH])
    ig = jax.nn.sigmoid(gates[:, LRU_CH:])
    nl = -lam_ref[...]
    softplus = jnp.maximum(nl, 0.0) + jnp.log1p(jnp.exp(-jnp.abs(nl)))
    log_a = -LRU_C * r * softplus
    a = jnp.exp(log_a)
    beta = jnp.sqrt(jnp.maximum(-jnp.tanh(log_a) * (a * a + 1.0), 0.0))
    return a, beta * ig * xr


def _lru_body(br_ref, prev_ref, gb_ref, cw_ref, cb_ref, wg_ref, bg_ref, lam_ref,
              yb_ref, hl_ref, ext_ref, h_ref, *, tt):
    i = pl.program_id(1)
    c = LRU_CH

    @pl.when(i == 0)
    def _():
        h_ref[...] = jnp.zeros((1, c), F32)
        ext_ref[0:SUBLANES, :] = jnp.zeros((SUBLANES, c), F32)

    @pl.when(i != 0)
    def _():
        ext_ref[0:SUBLANES, :] = prev_ref[0]

    ext_ref[SUBLANES:SUBLANES + tt, :] = br_ref[0]
    first = SUBLANES - (CONV_B_WIDTH - 1)
    xr = cb_ref[...]
    for w in range(CONV_B_WIDTH):
        xr = xr + ext_ref[pl.ds(first + w, tt), :] * cw_ref[w:w + 1, :]
    a, u = _lru_gates(xr, wg_ref, bg_ref, lam_ref)

    row = lax.broadcasted_iota(jnp.int32, (tt, c), 0)
    d = 1
    while d < tt:
        if d < SUBLANES:
            keep = row >= d
            a_sh = jnp.where(keep, pltpu.roll(a, d, 0), 1.0)
            u_sh = jnp.where(keep, pltpu.roll(u, d, 0), 0.0)
        else:
            a_sh = jnp.concatenate([jnp.ones((d, c), F32), a[:tt - d]], axis=0)
            u_sh = jnp.concatenate([jnp.zeros((d, c), F32), u[:tt - d]], axis=0)
        u = a * u_sh + u
        a = a * a_sh
        d *= 2
    h = a * h_ref[...] + u
    yb_ref[0] = (h * gb_ref[0]).astype(yb_ref.dtype)
    h_last = h[tt - 1:tt, :]
    h_ref[...] = h_last
    hl_ref[0] = h_last


def _lru(br, gb, cw, cb, wg, bg, lam):
    b, t, c = br.shape
    tt = SEQ_TILE_LRU
    halo_blocks = tt // SUBLANES
    vec = lambda a: a.reshape(1, -1)
    return pl.pallas_call(
        functools.partial(_lru_body, tt=tt),
        grid=(b, t // tt),
        in_specs=[pl.BlockSpec((1, tt, c), lambda bi, i: (bi, i, 0)),
                  pl.BlockSpec((1, SUBLANES, c), lambda bi, i: (bi, jnp.maximum(i * halo_blocks - 1, 0), 0)),
                  pl.BlockSpec((1, tt, c), lambda bi, i: (bi, i, 0)),
                  pl.BlockSpec((CONV_B_WIDTH, c), lambda bi, i: (0, 0)),
                  pl.BlockSpec((1, c), lambda bi, i: (0, 0)),
                  pl.BlockSpec((c, 2 * c), lambda bi, i: (0, 0)),
                  pl.BlockSpec((1, 2 * c), lambda bi, i: (0, 0)),
                  pl.BlockSpec((1, c), lambda bi, i: (0, 0))],
        out_specs=[pl.BlockSpec((1, tt, c), lambda bi, i: (bi, i, 0)),
                   pl.BlockSpec((1, 1, c), lambda bi, i: (bi, 0, 0))],
        out_shape=[jax.ShapeDtypeStruct((b, t, c), BF16),
                   jax.ShapeDtypeStruct((b, 1, c), F32)],
        scratch_shapes=[pltpu.VMEM((tt + SUBLANES, c), F32), pltpu.VMEM((1, c), F32)],
        compiler_params=_params(("parallel", "arbitrary")),
        name="lru",
    )(br, br, gb, cw, vec(cb), wg, vec(bg), vec(lam))


def _sample_mix_body(u_ref, br_ref, gb_ref, ha_ref, hb_ref, h0_ref,
                     cwa_ref, cba_ref, lg_ref, lb_ref, cwb_ref, cbb_ref, wg_ref, bg_ref, lam_ref,
                     ya_ref, yb_ref, h_ref):
    u = u_ref[...]
    acc = cba_ref[...] + u * cwa_ref[CONV_A_WIDTH - 1:CONV_A_WIDTH, :]
    for w in range(CONV_A_WIDTH - 1):
        acc = acc + ha_ref[w] * cwa_ref[w:w + 1, :]
    ya_ref[...] = _layer_norm_silu(acc, lg_ref[...], lb_ref[...]).astype(ya_ref.dtype)

    br = br_ref[...]
    xr = cbb_ref[...] + br * cwb_ref[CONV_B_WIDTH - 1:CONV_B_WIDTH, :]
    for w in range(CONV_B_WIDTH - 1):
        xr = xr + hb_ref[w] * cwb_ref[w:w + 1, :]
    a, x_in = _lru_gates(xr, wg_ref, bg_ref, lam_ref)
    h = a * h0_ref[...] + x_in
    h_ref[...] = h
    yb_ref[...] = (h * gb_ref[...]).astype(yb_ref.dtype)


def _sample_mix(u, br, gb, hist_a_t, hist_b_t, h0, cwa, cba, lg, lb, cwb, cbb, wg, bg, lam):
    n, c = u.shape
    vec = lambda a: a.reshape(1, -1)
    args = [u, br, gb, hist_a_t, hist_b_t, h0, cwa, vec(cba), vec(lg), vec(lb), cwb, vec(cbb), wg, vec(bg), vec(lam)]
    full = lambda a: pl.BlockSpec(a.shape, lambda i, nd=a.ndim: (0,) * nd)
    return pl.pallas_call(
        _sample_mix_body,
        grid=(1,),
        in_specs=[full(a) for a in args],
        out_specs=[pl.BlockSpec((n, c), lambda i: (0, 0)) for _ in range(3)],
        out_shape=[jax.ShapeDtypeStruct((n, c), BF16), jax.ShapeDtypeStruct((n, c), BF16),
                   jax.ShapeDtypeStruct((n, c), F32)],
        compiler_params=_params(("arbitrary",)),
        name="sample_mix",
    )(*args)


def _diff_lambda(lq1_ref, lk1_ref, lq2_ref, lk2_ref, lam_init):
    e1 = jnp.exp(jnp.sum(lq1_ref[...] * lk1_ref[...], axis=-1, keepdims=True))
    e2 = jnp.exp(jnp.sum(lq2_ref[...] * lk2_ref[...], axis=-1, keepdims=True))
    return e1 - e2 + lam_init


Q_SCALE = ATT_HD ** -0.5 * math.log2(math.e)
VT_ROWS = ATT_VD + 16


def _qkv_body(x_ref, g_ref, wq_ref, wk_ref, wv_ref, qt_ref, k_ref, v_ref, kb_ref, vt_ref):
    xn = _rms(x_ref[...], g_ref[...]).astype(BF16)
    q = _dot(xn, wq_ref[...])
    k = _dot(xn, wk_ref[...])
    v = _dot(xn, wv_ref[...])
    qt_ref[0] = (q * Q_SCALE).T.astype(BF16)
    _store_heads(k_ref, k, ATT_HEADS)
    kb_ref[...] = k.astype(BF16)
    _store_heads(v_ref, v, ATT_HEADS)
    vt = v.T.astype(BF16)
    ones = jnp.ones((VT_ROWS - ATT_VD, vt.shape[1]), BF16)
    for h in range(ATT_HEADS):
        vt_ref[0, h * VT_ROWS:h * VT_ROWS + ATT_VD, :] = vt[h * ATT_VD:(h + 1) * ATT_VD, :]
        vt_ref[0, h * VT_ROWS + ATT_VD:(h + 1) * VT_ROWS, :] = ones


def _qkv(x, g, w, bsz, seq, rider=None):
    m, d = x.shape
    tm = ROW_TILE
    per_seq = seq // tm
    row = pl.BlockSpec((tm, d), lambda i: (i, 0))
    heads = pl.BlockSpec((tm, ATT_HEADS, ATT_VD), lambda i: (i, 0, 0))
    col = lambda rows: pl.BlockSpec((1, rows, tm), lambda i: (i // per_seq, 0, i % per_seq))
    wspec = lambda b: pl.BlockSpec((d, d), lambda i: (0, b))
    return _call_with_rider(
        _qkv_body, (m // tm,),
        [row, pl.BlockSpec((1, d), lambda i: (0, 0)), wspec(0), wspec(1), wspec(2)],
        [x, g.reshape(1, d), w, w, w],
        [col(d), heads, heads, row, col(ATT_HEADS * VT_ROWS)],
        [jax.ShapeDtypeStruct((bsz, d, seq), BF16),
         jax.ShapeDtypeStruct((m, ATT_HEADS, ATT_VD), F32),
         jax.ShapeDtypeStruct((m, ATT_HEADS, ATT_VD), F32),
         jax.ShapeDtypeStruct((m, d), BF16),
         jax.ShapeDtypeStruct((bsz, ATT_HEADS * VT_ROWS, seq), BF16)],
        "qkv", rider)


def _flash_body(qi_ref, ki_ref, qt_ref, k_ref, vt_ref, lq1_ref, lk1_ref, lq2_ref, lk2_ref, g_ref, o_ref,
                qs_ref, m_ref, acc_ref, s_ref, *, tq, lam_init):
    step_id = pl.program_id(1)
    qi = qi_ref[step_id]
    ki = ki_ref[step_id]
    vd = ATT_VD

    @pl.when(ki == 0)
    def _():
        dim = lax.broadcasted_iota(jnp.int32, (vd, tq), 0)
        for h in range(ATT_HEADS):
            qh = qt_ref[0, h * vd:(h + 1) * vd, :]
            zero = jnp.zeros_like(qh)
            qs_ref[h, :, 0:tq] = jnp.where(dim < ATT_HD, qh, zero)
            qs_ref[h, :, tq:2 * tq] = jnp.where(dim >= ATT_HD, qh, zero)
        m_ref[...] = jnp.full(m_ref.shape, NEG_INF, F32)
        acc_ref[...] = jnp.zeros(acc_ref.shape, F32)

    def scores(h):
        s_ref[h % 2] = _dot(k_ref[0, :, h * vd:(h + 1) * vd], qs_ref[h])

    def step(diagonal):
        if diagonal:
            key = lax.broadcasted_iota(jnp.int32, (tq, 2 * tq), 0)
            qry = lax.broadcasted_iota(jnp.int32, (tq, 2 * tq), 1)
            visible = key <= jnp.where(qry >= tq, qry - tq, qry)
        scores(0)
        for h in range(ATT_HEADS):
            if h + 1 < ATT_HEADS:
                scores(h + 1)
            s = s_ref[h % 2]
            if diagonal:
                s = jnp.where(visible, s, NEG_INF)
            m_prev = m_ref[h]
            m_new = jnp.maximum(m_prev, jnp.max(s, axis=0, keepdims=True))
            alpha = jnp.exp2(m_prev - m_new)
            p = jnp.exp2(s - m_new).astype(BF16)
            vth = vt_ref[0, h * VT_ROWS:(h + 1) * VT_ROWS, :]
            acc_ref[h] = alpha * acc_ref[h] + _dot(vth, p)
            m_ref[h] = m_new

    @pl.when(ki < qi)
    def _():
        step(False)

    @pl.when(ki == qi)
    def _():
        step(True)
        lam = _diff_lambda(lq1_ref, lk1_ref, lq2_ref, lk2_ref, lam_init)
        g = g_ref[...]
        for h in range(ATT_HEADS):
            o = acc_ref[h, 0:vd, :] / acc_ref[h, vd:vd + 1, :]
            att = o[:, 0:tq] - lam * o[:, tq:2 * tq]
            ms = jnp.mean(att * att, axis=0, keepdims=True)
            att = att * lax.rsqrt(ms + EPS) * g * (1.0 - lam_init)
            o_ref[0, :, h * vd:(h + 1) * vd] = att.T.astype(o_ref.dtype)


def _flash(qt, k, vt, lam_p, g, lam_init):
    b, d, t = qt.shape
    tq = ATT_TQ
    nq = t // tq
    pairs = [(qi, ki) for qi in range(nq) for ki in range(qi + 1)]
    qi_tab = jnp.asarray([p[0] for p in pairs], jnp.int32)
    ki_tab = jnp.asarray([p[1] for p in pairs], jnp.int32)
    vec = lambda a: a.reshape(1, -1)
    small = lambda n: pl.BlockSpec((1, n), lambda bi, s, qi, ki: (0, 0))
    grid_spec = pltpu.PrefetchScalarGridSpec(
        num_scalar_prefetch=2,
        grid=(b, len(pairs)),
        in_specs=[pl.BlockSpec((1, d, tq), lambda bi, s, qi, ki: (bi, 0, qi[s])),
                  pl.BlockSpec((1, tq, d), lambda bi, s, qi, ki: (bi, ki[s], 0)),
                  pl.BlockSpec((1, ATT_HEADS * VT_ROWS, tq), lambda bi, s, qi, ki: (bi, 0, ki[s])),
                  small(ATT_HD), small(ATT_HD), small(ATT_HD), small(ATT_HD),
                  pl.BlockSpec((ATT_VD, 1), lambda bi, s, qi, ki: (0, 0))],
        out_specs=pl.BlockSpec((1, tq, d), lambda bi, s, qi, ki: (bi, qi[s], 0)),
        scratch_shapes=[pltpu.VMEM((ATT_HEADS, ATT_VD, 2 * tq), BF16),
                        pltpu.VMEM((ATT_HEADS, 1, 2 * tq), F32),
                        pltpu.VMEM((ATT_HEADS, VT_ROWS, 2 * tq), F32),
                        pltpu.VMEM((2, tq, 2 * tq), F32)],
    )
    return pl.pallas_call(
        functools.partial(_flash_body, tq=tq, lam_init=lam_init),
        grid_spec=grid_spec,
        out_shape=jax.ShapeDtypeStruct((b, t, d), BF16),
        compiler_params=_params(("parallel", "arbitrary")),
        name="flash_diff_attn",
    )(qi_tab, ki_tab, qt, k, vt, *[vec(a) for a in lam_p], g.reshape(ATT_VD, 1))


def _decode_step(j, last, refs_in, o_ref, scratch, lam_init, pp):
    q_ref, kn_ref, vn_ref, lq1_ref, lk1_ref, lq2_ref, lk2_ref, g_ref = refs_in[:8]
    k_refs = refs_in[8:8 + pp]
    v_refs = refs_in[8 + pp:8 + 2 * pp]
    qx_ref, m_ref, l_ref, acc_ref = scratch
    nh = ATT_HEADS
    rows = PAGE_SIZE * nh

    @pl.when(j == 0)
    def _():
        q8 = q_ref[0].astype(F32)
        lane = lax.broadcasted_iota(jnp.int32, (nh, ATT_VD), 1)
        qx = jnp.concatenate([jnp.where(lane < ATT_HD, q8, 0.0), jnp.where(lane >= ATT_HD, q8, 0.0)], axis=0)
        qx_ref[...] = qx.astype(BF16)
        kn = kn_ref[0].astype(BF16).astype(F32)
        vn = vn_ref[0].astype(BF16).astype(F32)
        m_ref[...] = jnp.sum(qx * jnp.concatenate([kn, kn], axis=0), axis=-1, keepdims=True)
        l_ref[...] = jnp.ones(l_ref.shape, F32)
        acc_ref[...] = jnp.concatenate([vn, vn], axis=0)

    qx = qx_ref[...]
    s = jnp.concatenate([_dot_nt(qx, k_ref[0].reshape(rows, ATT_VD).astype(BF16)) for k_ref in k_refs], axis=1)
    r = lax.broadcasted_iota(jnp.int32, s.shape, 0)
    c = lax.broadcasted_iota(jnp.int32, s.shape, 1)
    s = jnp.where(c % nh == r % nh, s, NEG_INF)
    m_prev = m_ref[...]
    m_new = jnp.maximum(m_prev, jnp.max(s, axis=-1, keepdims=True))
    alpha = jnp.exp2(m_prev - m_new)
    p = jnp.exp2(s - m_new)
    l_ref[...] = alpha * l_ref[...] + jnp.sum(p, axis=-1, keepdims=True)
    pb = p.astype(BF16)
    acc = alpha * acc_ref[...]
    for i, v_ref in enumerate(v_refs):
        acc = acc + _dot(pb[:, i * rows:(i + 1) * rows], v_ref[0].reshape(rows, ATT_VD).astype(BF16))
    acc_ref[...] = acc
    m_ref[...] = m_new

    @pl.when(j == last)
    def _():
        lam = _diff_lambda(lq1_ref, lk1_ref, lq2_ref, lk2_ref, lam_init)
        o = acc_ref[...] / l_ref[...]
        att = o[0:nh] - lam * o[nh:2 * nh]
        o_ref[0] = (_rms(att, g_ref[...]) * (1.0 - lam_init)).astype(o_ref.dtype)


def _linear_step(idx, grid):
    step = idx[0]
    for i, n in zip(idx[1:], grid[1:]):
        step = step * n + i
    return step


def _call_with_rider(host_body, grid, in_specs, args, out_specs, out_shape, name, rider):
    if rider is None:
        return pl.pallas_call(host_body, grid=grid, in_specs=in_specs, out_specs=out_specs, out_shape=out_shape,
                              compiler_params=_params(("parallel",) * len(grid)), name=name)(*args)
    nh, vd = ATT_HEADS, ATT_VD
    page_table = rider['page_table']
    n_seq, seq0, base = rider['n_seq'], rider['seq0'], rider['page_base']
    per_seq = math.prod(grid) // n_seq
    pp = page_table.shape[1] // per_seq
    assert math.prod(grid) == n_seq * per_seq and page_table.shape[1] == pp * per_seq
    n_in, n_out = len(in_specs), len(out_specs)
    host_in = [pl.BlockSpec(sp.block_shape, lambda *a, f=sp.index_map: f(*a[:-1]), pipeline_mode=sp.pipeline_mode)
               for sp in in_specs]
    host_out = [pl.BlockSpec(sp.block_shape, lambda *a, f=sp.index_map: f(*a[:-1])) for sp in out_specs]
    seq = lambda a: _linear_step(a[:-1], grid) // per_seq
    sub = lambda a: _linear_step(a[:-1], grid) % per_seq
    row_spec = pl.BlockSpec((1, nh, vd), lambda *a: (seq0 + seq(a), 0, 0))
    small = lambda n: pl.BlockSpec((1, n), lambda *a: (0, 0))
    page_specs = [pl.BlockSpec((1, PAGE_SIZE, nh, vd),
                               lambda *a, i=i: (base + a[-1][seq0 + seq(a), sub(a) * pp + i], 0, 0, 0))
                  for i in range(pp)]
    rider_in = [row_spec, row_spec, row_spec, small(ATT_HD), small(ATT_HD), small(ATT_HD), small(ATT_HD), small(vd)]
    rider_in += page_specs + page_specs
    vec = lambda a: a.reshape(1, -1)
    rider_args = [rider['q'], rider['k_new'], rider['v_new'], *[vec(a) for a in rider['lam_p']], vec(rider['g']),
                  *([rider['cache_k']] * pp), *([rider['cache_v']] * pp)]

    def body(pt_ref, *refs):
        host_body(*refs[:n_in], *refs[n_in + len(rider_in):n_in + len(rider_in) + n_out])
        step = _linear_step([pl.program_id(i) for i in range(len(grid))], grid)
        _decode_step(step % per_seq, per_seq - 1, refs[n_in:n_in + len(rider_in)],
                     refs[n_in + len(rider_in) + n_out], refs[n_in + len(rider_in) + n_out + 1:], rider['lam_init'], pp)

    grid_spec = pltpu.PrefetchScalarGridSpec(
        num_scalar_prefetch=1,
        grid=grid,
        in_specs=host_in + rider_in,
        out_specs=host_out + [pl.BlockSpec((1, nh, vd), lambda *a: (seq(a), 0, 0))],
        scratch_shapes=[pltpu.VMEM((2 * nh, vd), BF16),
                        pltpu.VMEM((2 * nh, 1), F32),
                        pltpu.VMEM((2 * nh, 1), F32),
                        pltpu.VMEM((2 * nh, vd), F32)],
    )
    return pl.pallas_call(
        body,
        grid_spec=grid_spec,
        out_shape=list(out_shape) + [jax.ShapeDtypeStruct((n_seq, nh, vd), BF16)],
        compiler_params=_params(("arbitrary",) * len(grid)),
        name=name + "_paged",
    )(page_table, *args, *rider_args)


def _xattn_head(q, k, v):
    s = _dot_nt(q, k)
    p = jnp.exp(s - jnp.max(s, axis=-1, keepdims=True))
    p = p / jnp.sum(p, axis=-1, keepdims=True)
    return _dot(p.astype(BF16), v)


def _xattn_body(q_ref, k_ref, v_ref, o_ref):
    hd = XATT_HD
    heads = [slice(h * hd, (h + 1) * hd) for h in range(XATT_HEADS)]
    scores = [_dot_nt(q_ref[0, :, c], k_ref[0, :, c]) for c in heads]
    for c, s in zip(heads, scores):
        p = jnp.exp(s - jnp.max(s, axis=-1, keepdims=True))
        l = jnp.sum(p, axis=-1, keepdims=True)
        o_ref[0, :, c] = (_dot(p.astype(BF16), v_ref[0, :, c]) / l).astype(o_ref.dtype)


def _xattn(q, mem_k, mem_v, rider=None):
    b, t, d = q.shape
    tq = ROW_TILE
    kv_spec = pl.BlockSpec((1, N_MEM, d), lambda bi, i: (bi, 0, 0))
    q_spec = pl.BlockSpec((1, tq, d), lambda bi, i: (bi, i, 0))
    return _call_with_rider(_xattn_body, (b, t // tq), [q_spec, kv_spec, kv_spec], [q, mem_k, mem_v],
                            [q_spec], [jax.ShapeDtypeStruct((b, t, d), BF16)], "xattn", rider)


def _xattn_cached_body(q_ref, k_ref, v_ref, o_ref):
    nh, hd = XATT_HEADS, XATT_HD
    rows = N_MEM * nh
    head = lax.broadcasted_iota(jnp.int32, (nh, rows), 0)
    col = lax.broadcasted_iota(jnp.int32, (nh, rows), 1)
    own = col % nh == head
    for i in range(XS_BATCH):
        q = q_ref[i]
        qx = jnp.concatenate([q[:, h * hd:(h + 1) * hd] for h in range(nh)], axis=0)
        k = k_ref[0, i].reshape(rows, hd).astype(BF16)
        v = v_ref[0, i].reshape(rows, hd).astype(BF16)
        s = jnp.where(own, _dot_nt(qx, k), NEG_INF)
        p = jnp.exp(s - jnp.max(s, axis=-1, keepdims=True))
        p = p / jnp.sum(p, axis=-1, keepdims=True)
        o = _dot(p.astype(BF16), v)
        for h in range(nh):
            o_ref[i, :, h * hd:(h + 1) * hd] = o[h:h + 1, :].astype(o_ref.dtype)


def _xattn_cached(q, cache_k, cache_v, layer):
    b, t, d = q.shape
    bb = XS_BATCH
    kv_spec = pl.BlockSpec((1, bb, N_MEM, XATT_HEADS, XATT_HD), lambda i: (layer, i, 0, 0, 0))
    return pl.pallas_call(
        _xattn_cached_body,
        grid=(b // bb,),
        in_specs=[pl.BlockSpec((bb, t, d), lambda i: (i, 0, 0)), kv_spec, kv_spec],
        out_specs=pl.BlockSpec((bb, t, d), lambda i: (i, 0, 0)),
        out_shape=jax.ShapeDtypeStruct((b, t, d), BF16),
        compiler_params=_params(("parallel",)),
        name="xattn_cached",
    )(q, cache_k, cache_v)


def _memkv_body(x_ref, wk_ref, wv_ref, kn_ref, vn_ref, kb_ref, vb_ref):
    xb = x_ref[...].astype(BF16)
    for w_ref, n_ref, b_ref in ((wk_ref, kn_ref, kb_ref), (wv_ref, vn_ref, vb_ref)):
        y = _dot(xb, w_ref[...])
        b_ref[...] = y.astype(BF16)
        _store_heads(n_ref, y, XATT_HEADS)


def _memkv(x, w):
    m, d = x.shape
    tm = _row_tile(m)
    row = pl.BlockSpec((tm, d), lambda i: (i, 0))
    heads = pl.BlockSpec((tm, XATT_HEADS, XATT_HD), lambda i: (i, 0, 0))
    return pl.pallas_call(
        _memkv_body,
        grid=(m // tm,),
        in_specs=[row, pl.BlockSpec((d, d), lambda i: (0, 0)), pl.BlockSpec((d, d), lambda i: (0, 1))],
        out_specs=[heads, heads, row, row],
        out_shape=[jax.ShapeDtypeStruct((m, XATT_HEADS, XATT_HD), F32),
                   jax.ShapeDtypeStruct((m, XATT_HEADS, XATT_HD), F32),
                   jax.ShapeDtypeStruct((m, d), BF16),
                   jax.ShapeDtypeStruct((m, d), BF16)],
        compiler_params=_params(("parallel",)),
        name="memkv",
    )(x, w, w)


def _block_diag(w):
    nb, n, _ = w.shape
    eye = jnp.eye(nb, dtype=w.dtype)
    return jnp.einsum('kij,kl->kilj', w, eye).reshape(nb * n, nb * n)


def _stack(arrays):
    return arrays[0][None] if len(arrays) == 1 else jnp.stack(arrays)


def _trunk(x, bt, xattn, p, even_mix, odd_mix, layers, mixed=None, ffn=None):
    d = x.shape[1]
    depth = p['ffn1_g'].shape[0]
    ffn = ffn or (lambda *a, **k: _ffn(*a, **k)[0])
    for l in layers:
        if mixed is None:
            x = ffn(x, p['ffn1_g'][l], p['ffn1_w_in'], p['ffn1_w_out'], l)
            post = (p['xattn_g'][l], p['xattn_w_q'][l], XATT_HD ** -0.5)
            x, q = even_mix(x, l // 2, post) if l % 2 == 0 else odd_mix(x, l // 2, l, post)
        else:
            x, q = mixed
            mixed = None
        o = xattn(q.reshape(*bt, d), l)
        last = l == depth - 1
        x = ffn(x, p['ffn2_g'][l], p['ffn2_w_in'], p['ffn2_w_out'], l,
                final_g=p['final_g'] if last else None, pre=(o.reshape(-1, d), p['xattn_w_out']))
    return x


def kernel(x_prompt, x_sample, state_conv_a, state_conv_b, state_lru, cache_k, cache_v, cache_mem_k, cache_mem_v, page_table, mem_prompt, ffn1_g, ffn1_w_in, ffn1_w_out, mix_g, even_w_in, conv_a_w, conv_a_b, conv_a_ln_g, conv_a_ln_b, conv_b_w, conv_b_b, lru_w_a, lru_b_a, lru_w_x, lru_b_x, lru_lambda, even_w_out, attn_w_in, lam_q1, lam_k1, lam_q2, lam_k2, attn_subln_g, attn_w_out, xattn_g, xattn_w_q, xattn_w_kv, xattn_w_out, ffn2_g, ffn2_w_in, ffn2_w_out, final_g):
    bsz, seq, d = x_prompt.shape
    dbsz, dseq, _ = x_sample.shape
    depth = ffn1_g.shape[0]
    n_even = even_w_in.shape[0]
    n_odd = attn_w_in.shape[0]
    bf = lambda w: w.astype(BF16)
    p = {
        'ffn1_g': ffn1_g, 'ffn1_w_in': bf(ffn1_w_in), 'ffn1_w_out': bf(ffn1_w_out),
        'ffn2_g': ffn2_g, 'ffn2_w_in': bf(ffn2_w_in), 'ffn2_w_out': bf(ffn2_w_out),
        'xattn_g': xattn_g, 'xattn_w_q': bf(xattn_w_q), 'xattn_w_out': bf(xattn_w_out),
        'final_g': final_g,
    }
    even_w_in_b = bf(even_w_in)
    even_w_out_b = bf(even_w_out)
    attn_w_in_b = bf(attn_w_in)
    attn_w_out_b = bf(attn_w_out)
    xattn_w_kv_b = bf(xattn_w_kv)
    lru_wg = [bf(jnp.concatenate([_block_diag(lru_w_a[e]), _block_diag(lru_w_x[e])], axis=1)) for e in range(n_even)]
    lru_bg = [jnp.concatenate([lru_b_a[e], lru_b_x[e]]) for e in range(n_even)]
    c = CONV_CH
    lam_inits = [0.8 - 0.6 * math.exp(-0.3 * (2 * o + 1)) for o in range(n_odd)]

    mem2 = mem_prompt.reshape(bsz * N_MEM, d)
    p_mem_k, p_mem_v, p_mem_kb, p_mem_vb = [], [], [], []
    for l in range(depth):
        mk, mv, mkb, mvb = _memkv(mem2, xattn_w_kv_b[l])
        p_mem_k.append(mk.reshape(bsz, N_MEM, XATT_HEADS, XATT_HD))
        p_mem_v.append(mv.reshape(bsz, N_MEM, XATT_HEADS, XATT_HD))
        p_mem_kb.append(mkb.reshape(bsz, N_MEM, d))
        p_mem_vb.append(mvb.reshape(bsz, N_MEM, d))

    assert n_odd == 1, "the paged attention is carried by the prompt kernels of a single odd layer"
    l_odd = 1
    s_conv_a, s_conv_b, s_lru, s_k, s_v = [], [], [], [], []
    pool = cache_k.shape[1]
    pages_k = cache_k.reshape(n_odd * pool, PAGE_SIZE, ATT_HEADS, ATT_VD)
    pages_v = cache_v.reshape(n_odd * pool, PAGE_SIZE, ATT_HEADS, ATT_VD)

    def s_even(x, e, post):
        u, br, gb = _even_in(x, mix_g[2 * e], even_w_in_b[e])
        ya, yb, h_new = _sample_mix(
            u, br, gb, jnp.swapaxes(state_conv_a[e], 0, 1), jnp.swapaxes(state_conv_b[e], 0, 1), state_lru[e],
            conv_a_w[e], conv_a_b[e], conv_a_ln_g[e], conv_a_ln_b[e],
            conv_b_w[e], conv_b_b[e], lru_wg[e], lru_bg[e], lru_lambda[e])
        s_conv_a.append(jnp.concatenate([state_conv_a[e][:, 1:], u[:, None, :]], axis=1))
        s_conv_b.append(jnp.concatenate([state_conv_b[e][:, 1:], br[:, None, :]], axis=1))
        s_lru.append(h_new)
        return _outproj(x, [ya, yb], [even_w_out_b[e][:c], even_w_out_b[e][c:]], post)

    s_xattn = lambda q, l: _xattn_cached(q, cache_mem_k, cache_mem_v, l)
    xs = _trunk(x_sample.reshape(dbsz * dseq, d), (dbsz, dseq), s_xattn, p, s_even, None, range(l_odd))
    (xs,) = _ffn(xs, ffn1_g[l_odd], p['ffn1_w_in'], p['ffn1_w_out'], l_odd)
    sq, sk, sv = _proj(xs, mix_g[l_odd], attn_w_in_b[0], d, [0, 1, 2],
                       [(0, BF16, Q_SCALE), (1, F32, 1.0), (2, F32, 1.0)])
    s_k.append(sk.reshape(dbsz, dseq, ATT_HEADS, ATT_VD))
    s_v.append(sv.reshape(dbsz, dseq, ATT_HEADS, ATT_VD))
    heads = (dbsz, ATT_HEADS, ATT_VD)
    hosts = []
    for l in range(depth):
        hosts += [('ffn', RIDER_SEQS), ('in', RIDER_SEQS), ('out', RIDER_SEQS), ('ffn', RIDER_SEQS)]
    assert sum(n for _, n in hosts) == dbsz
    riders = {'ffn': [], 'in': [], 'out': []}
    seq0 = 0
    for kind, n_seq in hosts:
        riders[kind].append(dict(
            seq0=seq0, n_seq=n_seq, q=sq.reshape(heads), k_new=sk.reshape(heads), v_new=sv.reshape(heads),
            cache_k=pages_k, cache_v=pages_v, page_base=0, page_table=page_table,
            lam_p=(lam_q1[0], lam_k1[0], lam_q2[0], lam_k2[0]), g=attn_subln_g[0], lam_init=lam_inits[0]))
        seq0 += n_seq
    att_parts = []

    def carried(outs, rider):
        if rider is None:
            return outs
        att_parts.append(outs[-1])
        return outs[:-1]

    p_conv_a, p_conv_b, p_lru, p_k, p_v = [], [], [], [], []

    def p_ffn(*a, **k):
        rider = riders['ffn'].pop(0)
        return carried(_ffn(*a, rider=rider, **k), rider)[0]

    def p_even(x, e, post):
        rider = riders['in'].pop(0)
        u, br, gb = carried(_even_in(x, mix_g[2 * e], even_w_in_b[e], rider), rider)
        u3, br3, gb3 = (a.reshape(bsz, seq, c) for a in (u, br, gb))
        ya = _conv_a(u3, conv_a_w[e], conv_a_b[e], conv_a_ln_g[e], conv_a_ln_b[e])
        yb, h_last = _lru(br3, gb3, conv_b_w[e], conv_b_b[e], lru_wg[e], lru_bg[e], lru_lambda[e])
        p_conv_a.append(u3[:, seq - (CONV_A_WIDTH - 1):])
        p_conv_b.append(br3[:, seq - (CONV_B_WIDTH - 1):])
        p_lru.append(h_last.reshape(bsz, c))
        rider = riders['out'].pop(0)
        return carried(_outproj(x, [ya.reshape(-1, c), yb.reshape(-1, c)],
                                [even_w_out_b[e][:c], even_w_out_b[e][c:]], post, rider), rider)

    def p_odd(x, o, l, post):
        rider = riders['in'].pop(0)
        qt, k, v, kb, vt = carried(_qkv(x, mix_g[l], attn_w_in_b[o], bsz, seq, rider), rider)
        p_k.append(k.reshape(bsz, seq, ATT_HEADS, ATT_VD))
        p_v.append(v.reshape(bsz, seq, ATT_HEADS, ATT_VD))
        att = _flash(qt, kb.reshape(bsz, seq, d), vt,
                     (lam_q1[o], lam_k1[o], lam_q2[o], lam_k2[o]), attn_subln_g[o], lam_inits[o])
        rider = riders['out'].pop(0)
        return carried(_outproj(x, [att.reshape(-1, d)], [attn_w_out_b[o]], post, rider), rider)

    p_xattn = lambda q, l: _xattn(q, p_mem_kb[l], p_mem_vb[l])[0]
    y_prompt = _trunk(x_prompt.reshape(bsz * seq, d), (bsz, seq), p_xattn, p, p_even, p_odd, range(depth), ffn=p_ffn)
    y_prompt = y_prompt.reshape(bsz, seq, d)
    assert not any(riders.values()) and len(att_parts) == len(hosts)

    att = jnp.concatenate(att_parts, axis=0).reshape(dbsz * dseq, d)
    post = (xattn_g[l_odd], p['xattn_w_q'][l_odd], XATT_HD ** -0.5)
    mixed = _outproj(xs, [att], [attn_w_out_b[0]], post)
    y_sample = _trunk(xs, (dbsz, dseq), s_xattn, p, s_even, None, range(l_odd, depth), mixed=tuple(mixed))
    y_sample = y_sample.reshape(dbsz, dseq, d)

    return (y_prompt, y_sample,
            _stack(p_conv_a), _stack(p_conv_b), _stack(p_lru),
            _stack(p_k), _stack(p_v),
            _stack(p_mem_k), _stack(p_mem_v),
            _stack(s_conv_a), _stack(s_conv_b), _stack(s_lru),
            _stack(s_k), _stack(s_v))
```
